```python
import jax
import jax.numpy as jnp
from jax import lax
import numpy as np

D_MODEL = 1024
BATCH = 8
SEQ = 2048
DEPTH = 1

MEM_LEN = 256
EPS = 1e-5

GLA_HEADS = 4
GLA_WIDTH_V = D_MODEL // 2
GLA_DV = GLA_WIDTH_V // GLA_HEADS
GLA_DK = GLA_DV // 2
GLA_WIDTH_K = GLA_HEADS * GLA_DK
GLA_GATE_RANK = 16
GLA_TAU = 16.0
GLA_CHUNK = 64

SGU_WIDTH = D_MODEL // 2
SGU_GROUPS = 4
SGU_GROUP_DIM = SGU_WIDTH // SGU_GROUPS
SGU_CHUNK = 128

N_BRANCH = 2

OFF_Q = 0
OFF_K = OFF_Q + GLA_WIDTH_K
OFF_V = OFF_K + GLA_WIDTH_K
OFF_R = OFF_V + GLA_WIDTH_V
OFF_A = OFF_R + GLA_WIDTH_V
OFF_U = OFF_A + GLA_GATE_RANK
OFF_SV = OFF_U + SGU_WIDTH
OFF_G = OFF_SV + SGU_WIDTH
IN_COLS = OFF_G + N_BRANCH * D_MODEL

XATTN_HEADS = 4
XATTN_DH = D_MODEL // XATTN_HEADS

N_EXPERTS = 32
TOP_K = 4
D_FF_EXPERT = D_MODEL
SWIGLU_ALPHA = 1.702
SWIGLU_LIMIT = 7.0
MOE_BLOCK = 128

kernel_name = 'hybrid_gla_sgu_xattn_moe_layer'


def _rmsnorm(x, g):
    xf = x.astype(jnp.float32)
    y = xf * lax.rsqrt(jnp.mean(xf * xf, axis=-1, keepdims=True) + EPS)
    return (y * g.astype(jnp.float32)).astype(x.dtype)


def _layernorm(x, g, b):
    xf = x.astype(jnp.float32)
    mu = jnp.mean(xf, axis=-1, keepdims=True)
    var = jnp.mean(jnp.square(xf - mu), axis=-1, keepdims=True)
    y = (xf - mu) * lax.rsqrt(var + EPS)
    return (y * g.astype(jnp.float32) + b.astype(jnp.float32)).astype(x.dtype)


def _gla(q, k, v, r, a_low, w_a2, b_a, out_g):
    B, S, _ = q.shape
    H, C = GLA_HEADS, GLA_CHUNK
    NC = S // C
    f32 = jnp.float32

    def to_chunks(t, d):
        return t.astype(f32).reshape(B, NC, C, H, d).transpose(0, 3, 1, 2, 4)

    log_a = jax.nn.log_sigmoid(a_low.astype(f32) @ w_a2.astype(f32) + b_a.astype(f32)) / GLA_TAU
    qc = to_chunks(q, GLA_DK) * (GLA_DK ** -0.5)
    kc = to_chunks(k, GLA_DK)
    vc = to_chunks(v, GLA_DV)
    cum = jnp.cumsum(to_chunks(log_a, GLA_DK), axis=3)
    cum_last = cum[:, :, :, -1:, :]
    q_in = qc * jnp.exp(cum)
    k_in = kc * jnp.exp(-cum)
    k_out = kc * jnp.exp(cum_last - cum)
    causal = jnp.tril(jnp.ones((C, C), dtype=bool))
    att = jnp.where(causal, jnp.einsum('bhncd,bhnsd->bhncs', q_in, k_in), 0.0)
    o_intra = jnp.einsum('bhncs,bhnse->bhnce', att, vc)
    kv = jnp.einsum('bhncd,bhnce->bhnde', k_out, vc)
    decay = jnp.exp(cum_last[:, :, :, 0, :])

    def step(state, inp):
        dec, kv_n = inp
        return dec[..., None] * state + kv_n, state

    s0 = jnp.zeros((B, H, GLA_DK, GLA_DV), f32)
    _, s_prev = lax.scan(step, s0, (jnp.moveaxis(decay, 2, 0), jnp.moveaxis(kv, 2, 0)))
    o_inter = jnp.einsum('bhncd,nbhde->bhnce', q_in, s_prev)
    o = (o_intra + o_inter).transpose(0, 2, 3, 1, 4).reshape(B, S, H, GLA_DV)
    o = o * lax.rsqrt(jnp.mean(o * o, axis=-1, keepdims=True) + EPS) * out_g.astype(f32).reshape(H, GLA_DV)
    o = o.reshape(B, S, GLA_WIDTH_V) * jax.nn.silu(r.astype(f32))
    return o.astype(q.dtype)


def _sgu(u_pre, v_pre, norm_g, norm_b, w_s, b_s):
    B, S, _ = u_pre.shape
    NC = S // SGU_CHUNK
    u = jax.nn.gelu(u_pre, approximate=False)
    v = _layernorm(jax.nn.gelu(v_pre, approximate=False), norm_g, norm_b)
    v = v.reshape(B, NC, SGU_CHUNK, SGU_GROUPS, SGU_GROUP_DIM)
    causal = jnp.tril(jnp.ones((SGU_CHUNK, SGU_CHUNK), dtype=bool))
    w = jnp.where(causal[None], w_s, 0.0).astype(v.dtype)
    mixed = jnp.einsum('gts,bnsgc->bntgc', w, v) + b_s.T[:, :, None].astype(v.dtype)
    return u * mixed.reshape(B, S, SGU_WIDTH)


def _cross_attn(h, mem_n, wq, wk, wv, wo):
    B, S, _ = h.shape
    M = mem_n.shape[1]
    q = (h @ wq).reshape(B, S, XATTN_HEADS, XATTN_DH)
    k = (mem_n @ wk).reshape(B, M, XATTN_HEADS, XATTN_DH)
    v = (mem_n @ wv).reshape(B, M, XATTN_HEADS, XATTN_DH)
    s = jnp.einsum('bshd,bmhd->bhsm', q, k).astype(jnp.float32) * (XATTN_DH ** -0.5)
    p = jax.nn.softmax(s, axis=-1).astype(v.dtype)
    o = jnp.einsum('bhsm,bmhd->bshd', p, v).reshape(B, S, D_MODEL)
    return o @ wo


def _moe(h, w_router, b_router, w1, b1, w2, b2):
    B, S, D = h.shape
    T = B * S
    xf = h.reshape(T, D)
    logits = (xf @ w_router + b_router).astype(jnp.float32)
    top_v, top_i = lax.top_k(logits, TOP_K)
    gates = jax.nn.softmax(top_v, axis=-1)
    n_assign = T * TOP_K
    flat_e = top_i.reshape(-1)
    flat_tok = jnp.arange(n_assign, dtype=jnp.int32) // TOP_K
    order = jnp.argsort(flat_e, stable=True)
    sorted_e = flat_e[order]
    counts = jnp.bincount(flat_e, length=N_EXPERTS)
    padded = ((counts + MOE_BLOCK - 1) // MOE_BLOCK) * MOE_BLOCK
    pad_end = jnp.cumsum(padded)
    pad_start = pad_end - padded
    start = jnp.cumsum(counts) - counts
    rank = jnp.arange(n_assign, dtype=jnp.int32) - start[sorted_e]
    dest = pad_start[sorted_e] + rank
    n_blocks = -(-n_assign // MOE_BLOCK) + N_EXPERTS
    tok_buf = jnp.full((n_blocks * MOE_BLOCK,), T, jnp.int32).at[dest].set(flat_tok[order])
    block_e = jnp.minimum(
        jnp.searchsorted(pad_end, jnp.arange(n_blocks, dtype=jnp.int32) * MOE_BLOCK, side='right'),
        N_EXPERTS - 1)
    x_pad = jnp.concatenate([xf, jnp.zeros((1, D), xf.dtype)], axis=0)
    xb = x_pad[tok_buf].reshape(n_blocks, MOE_BLOCK, D)

    def expert_block(args):
        xblk, e = args
        hcat = xblk @ w1[e] + b1[e]
        glu = jnp.minimum(hcat[:, :D_FF_EXPERT], SWIGLU_LIMIT)
        lin = jnp.clip(hcat[:, D_FF_EXPERT:], -SWIGLU_LIMIT, SWIGLU_LIMIT)
        act = (lin + 1.0) * glu * jax.nn.sigmoid(SWIGLU_ALPHA * glu)
        return act @ w2[e] + b2[e]

    yb = lax.map(expert_block, (xb, block_e)).reshape(n_blocks * MOE_BLOCK, D)
    y_assign = jnp.zeros((n_assign, D), yb.dtype).at[order].set(yb[dest])
    y = jnp.einsum('tkd,tk->td', y_assign.reshape(T, TOP_K, D), gates.astype(yb.dtype))
    return y.reshape(B, S, D)


def setup_inputs(seed: int = 0) -> dict:
    key = jax.random.key(seed)
    keys = iter(jax.random.split(key, 40))
    f32 = jnp.float32

    def nrm(shape, scale):
        return jax.random.normal(next(keys), shape, f32) * scale

    L = DEPTH
    return {
        'x': nrm((BATCH, SEQ, D_MODEL), 1.0),
        'mem': nrm((BATCH, MEM_LEN, D_MODEL), 1.0),
        'norm_mix_g': 1.0 + nrm((L, D_MODEL), 0.02),
        'w_in': nrm((L, D_MODEL, IN_COLS), D_MODEL ** -0.5),
        'gla_w_a2': nrm((L, GLA_GATE_RANK, GLA_WIDTH_K), GLA_GATE_RANK ** -0.5),
        'gla_b_a': nrm((L, GLA_WIDTH_K), 0.1),
        'gla_out_g': 1.0 + nrm((L, GLA_WIDTH_V), 0.02),
        'sgu_norm_g': 1.0 + nrm((L, SGU_WIDTH), 0.02),
        'sgu_norm_b': nrm((L, SGU_WIDTH), 0.02),
        'sgu_w': nrm((L, SGU_GROUPS, SGU_CHUNK, SGU_CHUNK), SGU_CHUNK ** -0.5),
        'sgu_b': 1.0 + nrm((L, SGU_GROUPS, SGU_CHUNK), 0.1),
        'w_proj_a': nrm((L, GLA_WIDTH_V, D_MODEL), GLA_WIDTH_V ** -0.5),
        'w_proj_b': nrm((L, SGU_WIDTH, D_MODEL), SGU_WIDTH ** -0.5),
        'w_mix_out': nrm((L, D_MODEL, D_MODEL), D_MODEL ** -0.5),
        'norm_x_g': 1.0 + nrm((L, D_MODEL), 0.02),
        'norm_mem_g': 1.0 + nrm((L, D_MODEL), 0.02),
        'w_xq': nrm((L, D_MODEL, D_MODEL), D_MODEL ** -0.5),
        'w_xk': nrm((L, D_MODEL, D_MODEL), D_MODEL ** -0.5),
        'w_xv': nrm((L, D_MODEL, D_MODEL), D_MODEL ** -0.5),
        'w_xo': nrm((L, D_MODEL, D_MODEL), D_MODEL ** -0.5),
        'norm_ffn_g': 1.0 + nrm((L, D_MODEL), 0.02),
        'w_router': nrm((L, D_MODEL, N_EXPERTS), D_MODEL ** -0.5),
        'b_router': nrm((L, N_EXPERTS), 0.01),
        'w_e1': nrm((L, N_EXPERTS, D_MODEL, 2 * D_FF_EXPERT), D_MODEL ** -0.5),
        'b_e1': nrm((L, N_EXPERTS, 2 * D_FF_EXPERT), 0.01),
        'w_e2': nrm((L, N_EXPERTS, D_FF_EXPERT, D_MODEL), D_FF_EXPERT ** -0.5),
        'b_e2': nrm((L, N_EXPERTS, D_MODEL), 0.01),
        'norm_final_g': 1.0 + nrm((D_MODEL,), 0.02),
    }


def reference(x, mem, norm_mix_g, w_in, gla_w_a2, gla_b_a, gla_out_g, sgu_norm_g, sgu_norm_b,
              sgu_w, sgu_b, w_proj_a, w_proj_b, w_mix_out, norm_x_g, norm_mem_g, w_xq, w_xk,
              w_xv, w_xo, norm_ffn_g, w_router, b_router, w_e1, b_e1, w_e2, b_e2, norm_final_g):
    for l in range(DEPTH):
        h = _rmsnorm(x, norm_mix_g[l])
        p = h @ w_in[l]
        y_a = _gla(p[..., OFF_Q:OFF_K], p[..., OFF_K:OFF_V], p[..., OFF_V:OFF_R],
                   p[..., OFF_R:OFF_A], p[..., OFF_A:OFF_U],
                   gla_w_a2[l], gla_b_a[l], gla_out_g[l]) @ w_proj_a[l]
        y_b = _sgu(p[..., OFF_U:OFF_SV], p[..., OFF_SV:OFF_G],
                   sgu_norm_g[l], sgu_norm_b[l], sgu_w[l], sgu_b[l]) @ w_proj_b[l]
        gate_a = jax.nn.sigmoid(p[..., OFF_G:OFF_G + D_MODEL])
        gate_b = jax.nn.sigmoid(p[..., OFF_G + D_MODEL:OFF_G + 2 * D_MODEL])
        x = x + (gate_a * y_a + gate_b * y_b) @ w_mix_out[l]
        mem_n = _rmsnorm(mem, norm_mem_g[l])
        x = x + _cross_attn(_rmsnorm(x, norm_x_g[l]), mem_n, w_xq[l], w_xk[l], w_xv[l], w_xo[l])
        x = x + _moe(_rmsnorm(x, norm_ffn_g[l]), w_router[l], b_router[l],
                     w_e1[l], b_e1[l], w_e2[l], b_e2[l])
    return _rmsnorm(x, norm_final_g)
```

```python
import functools

import jax
import jax.numpy as jnp
from jax import lax
from jax.experimental import pallas as pl
from jax.experimental.pallas import tpu as pltpu

F32 = jnp.float32
BF16 = jnp.bfloat16

D_MODEL = 1024
MEM_LEN = 256
EPS = 1e-5

GLA_HEADS = 4
GLA_DV = 128
GLA_DK = 64
GLA_WIDTH_K = GLA_HEADS * GLA_DK
GLA_WIDTH_V = GLA_HEADS * GLA_DV
GLA_GATE_RANK = 16
GLA_RANK_PAD = 128
GLA_TAU = 16.0
GLA_CHUNK = 64

SGU_WIDTH = 512
SGU_GROUPS = 4
SGU_GROUP_DIM = SGU_WIDTH // SGU_GROUPS
SGU_CHUNK = 128

OFF_Q = 0
OFF_K = OFF_Q + GLA_WIDTH_K
OFF_V = OFF_K + GLA_WIDTH_K
OFF_R = OFF_V + GLA_WIDTH_V
OFF_A = OFF_R + GLA_WIDTH_V
OFF_U = OFF_A + GLA_GATE_RANK
OFF_SV = OFF_U + SGU_WIDTH
OFF_G = OFF_SV + SGU_WIDTH

XATTN_HEADS = 4
XATTN_DH = D_MODEL // XATTN_HEADS

N_EXPERTS = 32
TOP_K = 4
D_FF = D_MODEL
SWIGLU_ALPHA = 1.702
SWIGLU_LIMIT = 7.0

SEQ_TILE = 512
ROW_BLOCK = 256
TOK_TILE = 256
VMEM_LIMIT = 56 * 1024 * 1024


def _rmsnorm(x, g):
    return x * lax.rsqrt(jnp.mean(x * x, axis=-1, keepdims=True) + EPS) * g


def _sigmoid(x):
    return 1.0 / (1.0 + jnp.exp(-x))


def _log_sigmoid(x):
    return jnp.minimum(x, 0.0) - jnp.log1p(jnp.exp(-jnp.abs(x)))


def _gelu(x):
    return 0.5 * x * (1.0 + lax.erf(x * (2.0 ** -0.5)))


def _dot(a, b):
    return jnp.dot(a, b, preferred_element_type=F32)


def _dot_nt(a, b):
    return lax.dot_general(a, b, (((1,), (1,)), ((), ())), preferred_element_type=F32)


def _dot_tn(a, b):
    return lax.dot_general(a, b, (((0,), (0,)), ((), ())), preferred_element_type=F32)


def _split_bf16(x):
    hi = x.astype(BF16)
    lo = (x - hi.astype(F32)).astype(BF16)
    return hi, lo


def _const_spec(shape):
    zeros = (0,) * len(shape)
    return pl.BlockSpec(shape, lambda *_: zeros, pipeline_mode=pl.Buffered(1))


def _xattn_kv_body(mem_ref, g_ref, wk_ref, wv_ref, k_ref, v_ref):
    m = _rmsnorm(mem_ref[0], g_ref[...]).astype(BF16)
    k_ref[0] = _dot(m, wk_ref[...]).astype(BF16)
    v_ref[0] = _dot(m, wv_ref[...]).astype(BF16)


def _xattn_kv(mem, g, wk, wv):
    B = mem.shape[0]
    return pl.pallas_call(
        _xattn_kv_body,
        grid=(B,),
        in_specs=[
            pl.BlockSpec((1, MEM_LEN, D_MODEL), lambda b: (b, 0, 0)),
            _const_spec((1, D_MODEL)),
            _const_spec((D_MODEL, D_MODEL)),
            _const_spec((D_MODEL, D_MODEL)),
        ],
        out_specs=[
            pl.BlockSpec((1, MEM_LEN, D_MODEL), lambda b: (b, 0, 0)),
            pl.BlockSpec((1, MEM_LEN, D_MODEL), lambda b: (b, 0, 0)),
        ],
        out_shape=[jax.ShapeDtypeStruct((B, MEM_LEN, D_MODEL), BF16)] * 2,
        compiler_params=pltpu.CompilerParams(
            dimension_semantics=("arbitrary",), vmem_limit_bytes=VMEM_LIMIT),
        name="xattn_kv",
    )(mem, g, wk, wv)


def _gla_branch(h, wqkvr_ref, wa_ref, wa2_ref, ba_ref, outg_ref, st_ref):
    ts = h.shape[0]
    n_chunks = ts // GLA_CHUNK
    p = _dot(h, wqkvr_ref[...])
    q = p[:, OFF_Q:OFF_K] * (GLA_DK ** -0.5)
    k = p[:, OFF_K:OFF_V]
    v = p[:, OFF_V:OFF_R].astype(BF16)
    r = p[:, OFF_R:OFF_A]

    a_low = _dot(h, wa_ref[...]).astype(BF16)
    log_a = _log_sigmoid(_dot(a_low, wa2_ref[...]) + ba_ref[...]) * (1.0 / GLA_TAU)

    row = lax.broadcasted_iota(jnp.int32, (ts, ts), 0)
    col = lax.broadcasted_iota(jnp.int32, (ts, ts), 1)
    tri = jnp.where((col <= row) & ((col >> 6) == (row >> 6)), 1.0, 0.0).astype(BF16)
    la_hi, la_lo = _split_bf16(log_a)
    cum = _dot(tri, la_hi) + _dot(tri, la_lo)

    q_in = (q * jnp.exp(cum)).astype(BF16)
    k_in = (k * jnp.exp(-cum)).astype(BF16)

    lane_k = lax.broadcasted_iota(jnp.int32, (1, GLA_WIDTH_K), 1)
    head_masks = [jnp.where((lane_k >> 6) == hh, 1.0, 0.0).astype(BF16) for hh in range(GLA_HEADS)]
    r4 = lax.broadcasted_iota(jnp.int32, (GLA_HEADS * GLA_CHUNK, GLA_CHUNK), 0)
    c4 = lax.broadcasted_iota(jnp.int32, (GLA_HEADS * GLA_CHUNK, GLA_CHUNK), 1)
    causal4 = c4 <= (r4 & (GLA_CHUNK - 1))
    sr = lax.broadcasted_iota(jnp.int32, (GLA_WIDTH_V, GLA_WIDTH_K), 0)
    sc = lax.broadcasted_iota(jnp.int32, (GLA_WIDTH_V, GLA_WIDTH_K), 1)
    state_mask = jnp.where((sr >> 7) == (sc >> 6), 1.0, 0.0)

    state = st_ref[...]
    outs = []
    for n in range(n_chunks):
        lo, hi = n * GLA_CHUNK, (n + 1) * GLA_CHUNK
        cum_c = cum[lo:hi]
        last = cum[hi - 1:hi]
        q_c = q_in[lo:hi]
        k_c = k_in[lo:hi]
        v_c = v[lo:hi]
        k_out = (k[lo:hi] * jnp.exp(last - cum_c)).astype(BF16)
        q4 = jnp.concatenate([q_c * head_masks[hh] for hh in range(GLA_HEADS)], axis=0)
        att4 = jnp.where(causal4, _dot_nt(q4, k_c), 0.0).astype(BF16)
        oi4 = _dot(att4, v_c)
        o_intra = jnp.concatenate(
            [oi4[hh * GLA_CHUNK:(hh + 1) * GLA_CHUNK, hh * GLA_DV:(hh + 1) * GLA_DV]
             for hh in range(GLA_HEADS)], axis=1)
        o_inter = _dot_nt(q_c, state.astype(BF16))
        outs.append(o_intra + o_inter)
        state = state * jnp.exp(last) + _dot_tn(v_c, k_out) * state_mask
    st_ref[...] = state

    o = jnp.concatenate(outs, axis=0)
    outg = outg_ref[...]
    normed = []
    for hh in range(GLA_HEADS):
        sl = slice(hh * GLA_DV, (hh + 1) * GLA_DV)
        normed.append(_rmsnorm(o[:, sl], outg[:, sl]))
    o = jnp.concatenate(normed, axis=1)
    return o * (r * _sigmoid(r))


def _sgu_branch(h, wu_ref, wsv_ref, ng_ref, nb_ref, sw_ref, sb_ref):
    ts = h.shape[0]
    n_chunks = ts // SGU_CHUNK
    u = _gelu(_dot(h, wu_ref[...]))
    v = _gelu(_dot(h, wsv_ref[...]))
    mu = jnp.mean(v, axis=-1, keepdims=True)
    vc = v - mu
    var = jnp.mean(vc * vc, axis=-1, keepdims=True)
    v = (vc * lax.rsqrt(var + EPS) * ng_ref[...] + nb_ref[...]).astype(BF16)

    row = lax.broadcasted_iota(jnp.int32, (SGU_CHUNK, SGU_CHUNK), 0)
    col = lax.broadcasted_iota(jnp.int32, (SGU_CHUNK, SGU_CHUNK), 1)
    causal = col <= row
    sb = sb_ref[...]
    per_group = []
    for g in range(SGU_GROUPS):
        w = jnp.where(causal, sw_ref[g], 0.0).astype(BF16)
        gs = slice(g * SGU_GROUP_DIM, (g + 1) * SGU_GROUP_DIM)
        vcat = jnp.concatenate(
            [v[c * SGU_CHUNK:(c + 1) * SGU_CHUNK, gs] for c in range(n_chunks)], axis=1)
        per_group.append(_dot(w, vcat) + sb[:, g:g + 1])
    rows = []
    for c in range(n_chunks):
        cs = slice(c * SGU_GROUP_DIM, (c + 1) * SGU_GROUP_DIM)
        rows.append(jnp.concatenate([per_group[g][:, cs] for g in range(SGU_GROUPS)], axis=1))
    mixed = jnp.concatenate(rows, axis=0)
    return u * mixed


def _mixer_body(x_ref, g_ref, wqkvr_ref, wa_ref, wa2_ref, ba_ref, outg_ref,
                wu_ref, wsv_ref, ng_ref, nb_ref, sw_ref, sb_ref,
                wg_ref, wpa_ref, wpb_ref, wmix_ref, o_ref, st_ref):
    @pl.when(pl.program_id(1) == 0)
    def _():
        st_ref[...] = jnp.zeros_like(st_ref)

    x = x_ref[0]
    h = _rmsnorm(x, g_ref[...]).astype(BF16)
    ya_in = _gla_branch(h, wqkvr_ref, wa_ref, wa2_ref, ba_ref, outg_ref, st_ref)
    y_a = _dot(ya_in.astype(BF16), wpa_ref[...])
    yb_in = _sgu_branch(h, wu_ref, wsv_ref, ng_ref, nb_ref, sw_ref, sb_ref)
    y_b = _dot(yb_in.astype(BF16), wpb_ref[...])
    gates = _sigmoid(_dot(h, wg_ref[...]))
    mix = gates[:, :D_MODEL] * y_a + gates[:, D_MODEL:] * y_b
    o_ref[0] = x + _dot(mix.astype(BF16), wmix_ref[...])


def _mixer(x, g, wqkvr, wa, wa2, ba, outg, wu, wsv, ng, nb, sw, sb, wg, wpa, wpb, wmix):
    B, S, D = x.shape
    consts = (g, wqkvr, wa, wa2, ba, outg, wu, wsv, ng, nb, sw, sb, wg, wpa, wpb, wmix)
    return pl.pallas_call(
        _mixer_body,
        grid=(B, S // SEQ_TILE),
        in_specs=[pl.BlockSpec((1, SEQ_TILE, D), lambda b, j: (b, j, 0))]
                 + [_const_spec(c.shape) for c in consts],
        out_specs=pl.BlockSpec((1, SEQ_TILE, D), lambda b, j: (b, j, 0)),
        out_shape=jax.ShapeDtypeStruct((B, S, D), F32),
        scratch_shapes=[pltpu.VMEM((GLA_WIDTH_V, GLA_WIDTH_K), F32)],
        compiler_params=pltpu.CompilerParams(
            dimension_semantics=("arbitrary", "arbitrary"), vmem_limit_bytes=VMEM_LIMIT),
        name="mixer",
    )(x, *consts)


def _xattn_body(x_ref, k_ref, v_ref, gx_ref, wq_ref, wo_ref, gf_ref, wr_hi_ref, wr_lo_ref, br_ref,
                x2_ref, h3_ref, topi_ref, gate_ref, rank_ref, cnt_ref, carry_ref):
    first = (pl.program_id(0) == 0) & (pl.program_id(1) == 0)

    @pl.when(first)
    def _():
        carry_ref[...] = jnp.zeros_like(carry_ref)

    x = x_ref[0]
    ts = x.shape[0]
    h = _rmsnorm(x, gx_ref[...]).astype(BF16)
    q = _dot(h, wq_ref[...]).astype(BF16)
    km = k_ref[0]
    vm = v_ref[0]
    heads = []
    for hh in range(XATTN_HEADS):
        sl = slice(hh * XATTN_DH, (hh + 1) * XATTN_DH)
        s = _dot_nt(q[:, sl], km[:, sl]) * (XATTN_DH ** -0.5)
        s = s - jnp.max(s, axis=-1, keepdims=True)
        e = jnp.exp(s)
        p = e / jnp.sum(e, axis=-1, keepdims=True)
        heads.append(_dot(p.astype(BF16), vm[:, sl]).astype(BF16))
    o = jnp.concatenate(heads, axis=1)
    x2 = x + _dot(o, wo_ref[...])
    x2_ref[0] = x2

    h3 = _rmsnorm(x2, gf_ref[...])
    h3_ref[0] = h3

    h_hi, h_lo = _split_bf16(h3)
    logits = (_dot_nt(wr_hi_ref[...], h_hi) + _dot_nt(wr_hi_ref[...], h_lo)
              + _dot_nt(wr_lo_ref[...], h_hi)) + br_ref[...]

    e_iota = lax.broadcasted_iota(jnp.int32, (N_EXPERTS, ts), 0)
    work = logits
    vals, idxs, hots = [], [], []
    for _ in range(TOP_K):
        m = jnp.max(work, axis=0, keepdims=True)
        idx = jnp.min(jnp.where(work == m, e_iota, N_EXPERTS), axis=0, keepdims=True)
        hot = e_iota == idx
        vals.append(m)
        idxs.append(idx)
        hots.append(hot)
        work = jnp.where(hot, -jnp.inf, work)
    exps = [jnp.exp(vv - vals[0]) for vv in vals]
    denom = exps[0] + exps[1] + exps[2] + exps[3]
    gate_ref[...] = jnp.concatenate([ee / denom for ee in exps], axis=0)
    topi_ref[...] = jnp.concatenate(idxs, axis=0)

    multi = jnp.where(hots[0] | hots[1] | hots[2] | hots[3], 1.0, 0.0)
    srow = lax.broadcasted_iota(jnp.int32, (ts, ts), 0)
    scol = lax.broadcasted_iota(jnp.int32, (ts, ts), 1)
    strict = jnp.where(srow < scol, 1.0, 0.0).astype(BF16)
    before = _dot(multi.astype(BF16), strict) + carry_ref[...]
    ranks = [jnp.sum(jnp.where(hot, before, 0.0), axis=0, keepdims=True) for hot in hots]
    rank_ref[...] = jnp.concatenate(ranks, axis=0).astype(jnp.int32)
    carry = carry_ref[...] + jnp.sum(multi, axis=1, keepdims=True)
    carry_ref[...] = carry
    cnt_ref[...] = carry.astype(jnp.int32)


def _xattn(x1, kmem, vmem, gx, wq, wo, gf, wr_hi, wr_lo, br):
    B, S, D = x1.shape
    T = B * S
    nj = S // SEQ_TILE
    tok_spec = pl.BlockSpec((1, SEQ_TILE, D), lambda b, j: (b, j, 0))
    mem_spec = pl.BlockSpec((1, MEM_LEN, D), lambda b, j: (b, 0, 0))
    lane_spec = pl.BlockSpec((TOP_K, SEQ_TILE), lambda b, j: (0, b * nj + j))
    consts = (gx, wq, wo, gf, wr_hi, wr_lo, br)
    return pl.pallas_call(
        _xattn_body,
        grid=(B, nj),
        in_specs=[tok_spec, mem_spec, mem_spec] + [_const_spec(c.shape) for c in consts],
        out_specs=[tok_spec, tok_spec, lane_spec, lane_spec, lane_spec,
                   pl.BlockSpec((N_EXPERTS, 1), lambda b, j: (0, 0))],
        out_shape=[
            jax.ShapeDtypeStruct((B, S, D), F32),
            jax.ShapeDtypeStruct((B, S, D), F32),
            jax.ShapeDtypeStruct((TOP_K, T), jnp.int32),
            jax.ShapeDtypeStruct((TOP_K, T), F32),
            jax.ShapeDtypeStruct((TOP_K, T), jnp.int32),
            jax.ShapeDtypeStruct((N_EXPERTS, 1), jnp.int32),
        ],
        scratch_shapes=[pltpu.VMEM((N_EXPERTS, 1), F32)],
        compiler_params=pltpu.CompilerParams(
            dimension_semantics=("arbitrary", "arbitrary"), vmem_limit_bytes=VMEM_LIMIT),
        name="xattn_router",
    )(x1, kmem, vmem, *consts)


def _row_copy(src_ref, src_row, dst_ref, dst_row, sem):
    return pltpu.make_async_copy(src_ref.at[pl.ds(src_row, 1)], dst_ref.at[pl.ds(dst_row, 1)], sem)


def _dispatch_body(zs_ref, nu_ref, dest_ref, h_ref, xb_ref, zbuf_ref, zsem, sem):
    @pl.when(pl.program_id(0) == 0)
    def _():
        zbuf_ref[...] = jnp.zeros_like(zbuf_ref)
        n_blocks = xb_ref.shape[0] // ROW_BLOCK

        def zero_copy(start):
            start = pl.multiple_of(start, ROW_BLOCK)
            return pltpu.make_async_copy(zbuf_ref, xb_ref.at[pl.ds(start, ROW_BLOCK)], zsem)

        for e in range(N_EXPERTS):
            zero_copy(zs_ref[e]).start()
        for e in range(N_EXPERTS):
            zero_copy(zs_ref[e]).wait()

        def tail(blk, c):
            cp = zero_copy(blk * ROW_BLOCK)
            cp.start()
            cp.wait()
            return c

        lax.fori_loop(nu_ref[0], n_blocks, tail, 0)

    def start(r, c):
        for kk in range(TOP_K):
            _row_copy(h_ref, r, xb_ref, dest_ref[0, 0, kk * TOK_TILE + r], sem).start()
        return c

    def wait(r, c):
        for kk in range(TOP_K):
            _row_copy(h_ref, r, xb_ref, dest_ref[0, 0, kk * TOK_TILE + r], sem).wait()
        return c

    lax.fori_loop(0, TOK_TILE, start, 0, unroll=8)
    lax.fori_loop(0, TOK_TILE, wait, 0, unroll=8)


def _dispatch(zero_start, n_used, dest_blocks, h3, n_slots):
    T, D = h3.shape
    return pl.pallas_call(
        _dispatch_body,
        grid_spec=pltpu.PrefetchScalarGridSpec(
            num_scalar_prefetch=2,
            grid=(T // TOK_TILE,),
            in_specs=[
                pl.BlockSpec((1, 1, TOP_K * TOK_TILE), lambda i, zs, nu: (i, 0, 0),
                             memory_space=pltpu.SMEM),
                pl.BlockSpec((TOK_TILE, D), lambda i, zs, nu: (i, 0)),
            ],
            out_specs=pl.BlockSpec(memory_space=pl.ANY),
            scratch_shapes=[pltpu.VMEM((ROW_BLOCK, D), F32),
                            pltpu.SemaphoreType.DMA, pltpu.SemaphoreType.DMA],
        ),
        out_shape=jax.ShapeDtypeStruct((n_slots, D), F32),
        compiler_params=pltpu.CompilerParams(
            dimension_semantics=("arbitrary",), vmem_limit_bytes=VMEM_LIMIT),
        name="dispatch",
    )(zero_start, n_used, dest_blocks, h3)


def _experts_body(be_ref, nu_ref, x_ref, w1_ref, b1_ref, w2_ref, b2_ref, y_ref, w1b_ref, w2b_ref):
    i = pl.program_id(0)

    @pl.when(i < nu_ref[0])
    def _():
        prev = be_ref[jnp.maximum(i - 1, 0)]

        @pl.when((i == 0) | (be_ref[i] != prev))
        def _():
            w1b_ref[...] = w1_ref[0].astype(BF16)
            w2b_ref[...] = w2_ref[0].astype(BF16)

        xb = x_ref[...].astype(BF16)
        hcat = _dot(xb, w1b_ref[...]) + b1_ref[0]
        glu = jnp.minimum(hcat[:, :D_FF], SWIGLU_LIMIT)
        lin = jnp.clip(hcat[:, D_FF:], -SWIGLU_LIMIT, SWIGLU_LIMIT)
        act = (lin + 1.0) * glu * _sigmoid(SWIGLU_ALPHA * glu)
        y_ref[...] = _dot(act.astype(BF16), w2b_ref[...]) + b2_ref[0]

    @pl.when(i >= nu_ref[0])
    def _():
        y_ref[...] = jnp.zeros_like(y_ref)


def _experts(block_e, n_used, xb, w1, b1, w2, b2):
    n_slots, D = xb.shape
    nb = n_slots // ROW_BLOCK

    def row_map(i, be, nu):
        return (jnp.minimum(i, nu[0] - 1), 0)

    def w_map(i, be, nu):
        return (be[i], 0, 0)

    return pl.pallas_call(
        _experts_body,
        grid_spec=pltpu.PrefetchScalarGridSpec(
            num_scalar_prefetch=2,
            grid=(nb,),
            in_specs=[
                pl.BlockSpec((ROW_BLOCK, D), row_map),
                pl.BlockSpec((1, D, 2 * D_FF), w_map),
                pl.BlockSpec((1, 1, 2 * D_FF), w_map),
                pl.BlockSpec((1, D_FF, D), w_map),
                pl.BlockSpec((1, 1, D), w_map),
            ],
            out_specs=pl.BlockSpec((ROW_BLOCK, D), lambda i, be, nu: (i, 0)),
            scratch_shapes=[pltpu.VMEM((D, 2 * D_FF), BF16), pltpu.VMEM((D_FF, D), BF16)],
        ),
        out_shape=jax.ShapeDtypeStruct((n_slots, D), F32),
        compiler_params=pltpu.CompilerParams(
            dimension_semantics=("arbitrary",), vmem_limit_bytes=VMEM_LIMIT),
        name="experts",
    )(block_e, n_used, xb, w1, b1, w2, b2)


def _combine_body(dest_ref, x_ref, gate_ref, g_ref, yb_ref, o_ref, rows_ref, sem):
    def start(r, c):
        for kk in range(TOP_K):
            _row_copy(yb_ref, dest_ref[0, 0, kk * TOK_TILE + r], rows_ref.at[kk], r, sem).start()
        return c

    def wait(r, c):
        for kk in range(TOP_K):
            _row_copy(yb_ref, dest_ref[0, 0, kk * TOK_TILE + r], rows_ref.at[kk], r, sem).wait()
        return c

    lax.fori_loop(0, TOK_TILE, start, 0, unroll=8)
    lax.fori_loop(0, TOK_TILE, wait, 0, unroll=8)

    gates = gate_ref[...]
    y = gates[:, 0:1] * rows_ref[0]
    for kk in range(1, TOP_K):
        y = y + gates[:, kk:kk + 1] * rows_ref[kk]
    o_ref[...] = _rmsnorm(x_ref[...] + y, g_ref[...])


def _combine(dest_blocks, x2, gates_t, g, yb):
    T, D = x2.shape
    return pl.pallas_call(
        _combine_body,
        grid=(T // TOK_TILE,),
        in_specs=[
            pl.BlockSpec((1, 1, TOP_K * TOK_TILE), lambda i: (i, 0, 0), memory_space=pltpu.SMEM),
            pl.BlockSpec((TOK_TILE, D), lambda i: (i, 0)),
            pl.BlockSpec((TOK_TILE, TOP_K), lambda i: (i, 0)),
            _const_spec((1, D)),
            pl.BlockSpec(memory_space=pl.ANY),
        ],
        out_specs=pl.BlockSpec((TOK_TILE, D), lambda i: (i, 0)),
        out_shape=jax.ShapeDtypeStruct((T, D), F32),
        scratch_shapes=[pltpu.VMEM((TOP_K, TOK_TILE, D), F32), pltpu.SemaphoreType.DMA],
        compiler_params=pltpu.CompilerParams(
            dimension_semantics=("arbitrary",), vmem_limit_bytes=VMEM_LIMIT),
        name="combine",
    )(dest_blocks, x2, gates_t, g, yb)


def _row(v):
    return v.reshape(1, -1).astype(F32)


def _layer(x, kv, p):
    B, S, D = x.shape
    T = B * S
    w_in = p["w_in"]
    wa = jnp.zeros((D, GLA_RANK_PAD), F32).at[:, :GLA_GATE_RANK].set(w_in[:, OFF_A:OFF_U])
    wa2 = jnp.zeros((GLA_RANK_PAD, GLA_WIDTH_K), F32).at[:GLA_GATE_RANK].set(p["gla_w_a2"])
    x1 = _mixer(
        x, _row(p["norm_mix_g"]),
        w_in[:, OFF_Q:OFF_A].astype(BF16), wa.astype(BF16), wa2.astype(BF16),
        _row(p["gla_b_a"]), _row(p["gla_out_g"]),
        w_in[:, OFF_U:OFF_SV].astype(BF16), w_in[:, OFF_SV:OFF_G].astype(BF16),
        _row(p["sgu_norm_g"]), _row(p["sgu_norm_b"]), p["sgu_w"].astype(F32), p["sgu_b"].T.astype(F32),
        w_in[:, OFF_G:].astype(BF16), p["w_proj_a"].astype(BF16), p["w_proj_b"].astype(BF16),
        p["w_mix_out"].astype(BF16))

    kmem, vmem = kv
    wr_t = p["w_router"].T.astype(F32)
    wr_hi = wr_t.astype(BF16)
    wr_lo = (wr_t - wr_hi.astype(F32)).astype(BF16)
    x2, h3, topi, gates, rank, counts = _xattn(
        x1, kmem, vmem, _row(p["norm_x_g"]), p["w_xq"].astype(BF16), p["w_xo"].astype(BF16),
        _row(p["norm_ffn_g"]), wr_hi, wr_lo, p["b_router"].reshape(N_EXPERTS, 1).astype(F32))

    counts = counts.reshape(N_EXPERTS)
    padded = ((counts + ROW_BLOCK - 1) // ROW_BLOCK) * ROW_BLOCK
    pad_end = jnp.cumsum(padded)
    pad_start = pad_end - padded
    n_blocks = (T * TOP_K) // ROW_BLOCK + N_EXPERTS
    n_slots = n_blocks * ROW_BLOCK
    n_used = (pad_end[-1] // ROW_BLOCK).astype(jnp.int32)
    blk = jnp.minimum(jnp.arange(n_blocks, dtype=jnp.int32), n_used - 1) * ROW_BLOCK
    block_e = jnp.minimum(jnp.searchsorted(pad_end, blk, side="right"), N_EXPERTS - 1).astype(jnp.int32)
    dest = pad_start[topi] + rank
    dest_blocks = (dest.reshape(TOP_K, T // TOK_TILE, TOK_TILE).transpose(1, 0, 2)
                   .reshape(T // TOK_TILE, 1, TOP_K * TOK_TILE).astype(jnp.int32))
    zero_start = jnp.maximum(pad_end - ROW_BLOCK, 0).astype(jnp.int32)

    n_used = n_used.reshape(1)
    xb = _dispatch(zero_start, n_used, dest_blocks, h3.reshape(T, D), n_slots)
    yb = _experts(block_e, n_used, xb,
                  p["w_e1"], p["b_e1"].reshape(N_EXPERTS, 1, 2 * D_FF),
                  p["w_e2"], p["b_e2"].reshape(N_EXPERTS, 1, D))
    return dest_blocks, x2.reshape(T, D), gates.T, yb


def kernel(x, mem, norm_mix_g, w_in, gla_w_a2, gla_b_a, gla_out_g, sgu_norm_g, sgu_norm_b, sgu_w, sgu_b, w_proj_a, w_proj_b, w_mix_out, norm_x_g, norm_mem_g, w_xq, w_xk, w_xv, w_xo, norm_ffn_g, w_router, b_router, w_e1, b_e1, w_e2, b_e2, norm_final_g):
    B, S, D = x.shape
    depth = w_in.shape[0]
    assert depth == 1, "the final norm is fused into the last layer's combine step"
    stacked = dict(norm_mix_g=norm_mix_g, w_in=w_in, gla_w_a2=gla_w_a2, gla_b_a=gla_b_a,
                   gla_out_g=gla_out_g, sgu_norm_g=sgu_norm_g, sgu_norm_b=sgu_norm_b, sgu_w=sgu_w,
                   sgu_b=sgu_b, w_proj_a=w_proj_a, w_proj_b=w_proj_b, w_mix_out=w_mix_out,
                   norm_x_g=norm_x_g, w_xq=w_xq, w_xo=w_xo, norm_ffn_g=norm_ffn_g,
                   w_router=w_router, b_router=b_router, w_e1=w_e1, b_e1=b_e1, w_e2=w_e2, b_e2=b_e2)
    p = {name: v[0] for name, v in stacked.items()}
    kv = _xattn_kv(mem, _row(norm_mem_g[0]), w_xk[0].astype(BF16), w_xv[0].astype(BF16))
    dest_blocks, x2, gates_t, yb = _layer(x, kv, p)
    out = _combine(dest_blocks, x2, gates_t, _row(norm_final_g), yb)
    return out.reshape(B, S, D)
```

```python
import functools

import jax
import jax.numpy as jnp
from jax import lax
from jax.experimental import pallas as pl
from jax.experimental.pallas import tpu as pltpu

F32 = jnp.float32
BF16 = jnp.bfloat16

D_MODEL = 1024
MEM_LEN = 256
EPS = 1e-5

GLA_HEADS = 4
GLA_DV = 128
GLA_DK = 64
GLA_WIDTH_K = GLA_HEADS * GLA_DK
GLA_WIDTH_V = GLA_HEADS * GLA_DV
GLA_GATE_RANK = 16
GLA_RANK_PAD = 128
GLA_TAU = 16.0
GLA_CHUNK = 64

SGU_WIDTH = 512
SGU_GROUPS = 4
SGU_GROUP_DIM = SGU_WIDTH // SGU_GROUPS
SGU_CHUNK = 128

OFF_Q = 0
OFF_K = OFF_Q + GLA_WIDTH_K
OFF_V = OFF_K + GLA_WIDTH_K
OFF_R = OFF_V + GLA_WIDTH_V
OFF_A = OFF_R + GLA_WIDTH_V
OFF_U = OFF_A + GLA_GATE_RANK
OFF_SV = OFF_U + SGU_WIDTH
OFF_G = OFF_SV + SGU_WIDTH

XATTN_HEADS = 4
XATTN_DH = D_MODEL // XATTN_HEADS

N_EXPERTS = 32
TOP_K = 4
D_FF = D_MODEL
SWIGLU_ALPHA = 1.702
SWIGLU_LIMIT = 7.0

SEQ_TILE = 512
ROW_BLOCK = 256
TOK_TILE = 256
VMEM_LIMIT = 56 * 1024 * 1024


def _rmsnorm(x, g):
    return x * lax.rsqrt(jnp.mean(x * x, axis=-1, keepdims=True) + EPS) * g


def _sigmoid(x):
    return 1.0 / (1.0 + jnp.exp(-x))


def _log_sigmoid(x):
    return jnp.minimum(x, 0.0) - jnp.log1p(jnp.exp(-jnp.abs(x)))


def _gelu(x):
    return 0.5 * x * (1.0 + lax.erf(x * (2.0 ** -0.5)))


def _dot(a, b):
    return jnp.dot(a, b, preferred_element_type=F32)


def _dot_nt(a, b):
    return lax.dot_general(a, b, (((1,), (1,)), ((), ())), preferred_element_type=F32)


def _dot_tn(a, b):
    return lax.dot_general(a, b, (((0,), (0,)), ((), ())), preferred_element_type=F32)


def _split_bf16(x):
    hi = x.astype(BF16)
    lo = (x - hi.astype(F32)).astype(BF16)
    return hi, lo


def _const_spec(shape):
    zeros = (0,) * len(shape)
    return pl.BlockSpec(shape, lambda *_: zeros, pipeline_mode=pl.Buffered(1))


def _xattn_kv_body(mem_ref, g_ref, wk_ref, wv_ref, k_ref, v_ref):
    m = _rmsnorm(mem_ref[0], g_ref[...]).astype(BF16)
    k_ref[0] = _dot(m, wk_ref[...]).astype(BF16)
    v_ref[0] = _dot(m, wv_ref[...]).astype(BF16)


def _xattn_kv(mem, g, wk, wv):
    B = mem.shape[0]
    return pl.pallas_call(
        _xattn_kv_body,
        grid=(B,),
        in_specs=[
            pl.BlockSpec((1, MEM_LEN, D_MODEL), lambda b: (b, 0, 0)),
            _const_spec((1, D_MODEL)),
            _const_spec((D_MODEL, D_MODEL)),
            _const_spec((D_MODEL, D_MODEL)),
        ],
        out_specs=[
            pl.BlockSpec((1, MEM_LEN, D_MODEL), lambda b: (b, 0, 0)),
            pl.BlockSpec((1, MEM_LEN, D_MODEL), lambda b: (b, 0, 0)),
        ],
        out_shape=[jax.ShapeDtypeStruct((B, MEM_LEN, D_MODEL), BF16)] * 2,
        compiler_params=pltpu.CompilerParams(
            dimension_semantics=("arbitrary",), vmem_limit_bytes=VMEM_LIMIT),
        name="xattn_kv",
    )(mem, g, wk, wv)


def _gla_branch(h, wqkvr_ref, wa_ref, wa2_ref, ba_ref, outg_ref, st_ref):
    ts = h.shape[0]
    n_chunks = ts // GLA_CHUNK
    p = _dot(h, wqkvr_ref[...])
    q = p[:, OFF_Q:OFF_K] * (GLA_DK ** -0.5)
    k = p[:, OFF_K:OFF_V]
    v = p[:, OFF_V:OFF_R].astype(BF16)
    r = p[:, OFF_R:OFF_A]

    a_low = _dot(h, wa_ref[...]).astype(BF16)
    log_a = _log_sigmoid(_dot(a_low, wa2_ref[...]) + ba_ref[...]) * (1.0 / GLA_TAU)

    row = lax.broadcasted_iota(jnp.int32, (ts, ts), 0)
    col = lax.broadcasted_iota(jnp.int32, (ts, ts), 1)
    tri = jnp.where((col <= row) & ((col >> 6) == (row >> 6)), 1.0, 0.0).astype(BF16)
    la_hi, la_lo = _split_bf16(log_a)
    cum = _dot(tri, la_hi) + _dot(tri, la_lo)

    q_in = (q * jnp.exp(cum)).astype(BF16)
    k_in = (k * jnp.exp(-cum)).astype(BF16)

    lane_k = lax.broadcasted_iota(jnp.int32, (1, GLA_WIDTH_K), 1)
    head_masks = [jnp.where((lane_k >> 6) == hh, 1.0, 0.0).astype(BF16) for hh in range(GLA_HEADS)]
    r4 = lax.broadcasted_iota(jnp.int32, (GLA_HEADS * GLA_CHUNK, GLA_CHUNK), 0)
    c4 = lax.broadcasted_iota(jnp.int32, (GLA_HEADS * GLA_CHUNK, GLA_CHUNK), 1)
    causal4 = c4 <= (r4 & (GLA_CHUNK - 1))
    sr = lax.broadcasted_iota(jnp.int32, (GLA_WIDTH_V, GLA_WIDTH_K), 0)
    sc = lax.broadcasted_iota(jnp.int32, (GLA_WIDTH_V, GLA_WIDTH_K), 1)
    state_mask = jnp.where((sr >> 7) == (sc >> 6), 1.0, 0.0)

    state = st_ref[...]
    outs = []
    for n in range(n_chunks):
        lo, hi = n * GLA_CHUNK, (n + 1) * GLA_CHUNK
        cum_c = cum[lo:hi]
        last = cum[hi - 1:hi]
        q_c = q_in[lo:hi]
        k_c = k_in[lo:hi]
        v_c = v[lo:hi]
        k_out = (k[lo:hi] * jnp.exp(last - cum_c)).astype(BF16)
        q4 = jnp.concatenate([q_c * head_masks[hh] for hh in range(GLA_HEADS)], axis=0)
        att4 = jnp.where(causal4, _dot_nt(q4, k_c), 0.0).astype(BF16)
        oi4 = _dot(att4, v_c)
        o_intra = jnp.concatenate(
            [oi4[hh * GLA_CHUNK:(hh + 1) * GLA_CHUNK, hh * GLA_DV:(hh + 1) * GLA_DV]
             for hh in range(GLA_HEADS)], axis=1)
        o_inter = _dot_nt(q_c, state.astype(BF16))
        outs.append(o_intra + o_inter)
        state = state * jnp.exp(last) + _dot_tn(v_c, k_out) * state_mask
    st_ref[...] = state

    o = jnp.concatenate(outs, axis=0)
    outg = outg_ref[...]
    normed = []
    for hh in range(GLA_HEADS):
        sl = slice(hh * GLA_DV, (hh + 1) * GLA_DV)
        normed.append(_rmsnorm(o[:, sl], outg[:, sl]))
    o = jnp.concatenate(normed, axis=1)
    return o * (r * _sigmoid(r))


def _sgu_branch(h, wu_ref, wsv_ref, ng_ref, nb_ref, sw_ref, sb_ref):
    ts = h.shape[0]
    n_chunks = ts // SGU_CHUNK
    u = _gelu(_dot(h, wu_ref[...]))
    v = _gelu(_dot(h, wsv_ref[...]))
    mu = jnp.mean(v, axis=-1, keepdims=True)
    vc = v - mu
    var = jnp.mean(vc * vc, axis=-1, keepdims=True)
    v = (vc * lax.rsqrt(var + EPS) * ng_ref[...] + nb_ref[...]).astype(BF16)

    row = lax.broadcasted_iota(jnp.int32, (SGU_CHUNK, SGU_CHUNK), 0)
    col = lax.broadcasted_iota(jnp.int32, (SGU_CHUNK, SGU_CHUNK), 1)
    causal = col <= row
    sb = sb_ref[...]
    per_group = []
    for g in range(SGU_GROUPS):
        w = jnp.where(causal, sw_ref[g], 0.0).astype(BF16)
        gs = slice(g * SGU_GROUP_DIM, (g + 1) * SGU_GROUP_DIM)
        vcat = jnp.concatenate(
            [v[c * SGU_CHUNK:(c + 1) * SGU_CHUNK, gs] for c in range(n_chunks)], axis=1)
        per_group.append(_dot(w, vcat) + sb[:, g:g + 1])
    rows = []
    for c in range(n_chunks):
        cs = slice(c * SGU_GROUP_DIM, (c + 1) * SGU_GROUP_DIM)
        rows.append(jnp.concatenate([per_group[g][:, cs] for g in range(SGU_GROUPS)], axis=1))
    mixed = jnp.concatenate(rows, axis=0)
    return u * mixed


def _mixer_body(x_ref, g_ref, wqkvr_ref, wa_ref, wa2_ref, ba_ref, outg_ref,
                wu_ref, wsv_ref, ng_ref, nb_ref, sw_ref, sb_ref,
                wg_ref, wpa_ref, wpb_ref, wmix_ref, o_ref, st_ref):
    @pl.when(pl.program_id(1) == 0)
    def _():
        st_ref[...] = jnp.zeros_like(st_ref)

    x = x_ref[0]
    h = _rmsnorm(x, g_ref[...]).astype(BF16)
    ya_in = _gla_branch(h, wqkvr_ref, wa_ref, wa2_ref, ba_ref, outg_ref, st_ref)
    y_a = _dot(ya_in.astype(BF16), wpa_ref[...])
    yb_in = _sgu_branch(h, wu_ref, wsv_ref, ng_ref, nb_ref, sw_ref, sb_ref)
    y_b = _dot(yb_in.astype(BF16), wpb_ref[...])
    gates = _sigmoid(_dot(h, wg_ref[...]))
    mix = gates[:, :D_MODEL] * y_a + gates[:, D_MODEL:] * y_b
    o_ref[0] = x + _dot(mix.astype(BF16), wmix_ref[...])


def _mixer(x, g, wqkvr, wa, wa2, ba, outg, wu, wsv, ng, nb, sw, sb, wg, wpa, wpb, wmix):
    B, S, D = x.shape
    consts = (g, wqkvr, wa, wa2, ba, outg, wu, wsv, ng, nb, sw, sb, wg, wpa, wpb, wmix)
    return pl.pallas_call(
        _mixer_body,
        grid=(B, S // SEQ_TILE),
        in_specs=[pl.BlockSpec((1, SEQ_TILE, D), lambda b, j: (b, j, 0))]
                 + [_const_spec(c.shape) for c in consts],
        out_specs=pl.BlockSpec((1, SEQ_TILE, D), lambda b, j: (b, j, 0)),
        out_shape=jax.ShapeDtypeStruct((B, S, D), F32),
        scratch_shapes=[pltpu.VMEM((GLA_WIDTH_V, GLA_WIDTH_K), F32)],
        compiler_params=pltpu.CompilerParams(
            dimension_semantics=("arbitrary", "arbitrary"), vmem_limit_bytes=VMEM_LIMIT),
        name="mixer",
    )(x, *consts)


def _xattn_body(x_ref, k_ref, v_ref, gx_ref, wq_ref, wo_ref, gf_ref, wr_hi_ref, wr_lo_ref, br_ref,
                x2_ref, h3_ref, topi_ref, gate_ref, rank_ref, cnt_ref, carry_ref):
    first = (pl.program_id(0) == 0) & (pl.program_id(1) == 0)

    @pl.when(first)
    def _():
        carry_ref[...] = jnp.zeros_like(carry_ref)

    x = x_ref[0]
    ts = x.shape[0]
    h = _rmsnorm(x, gx_ref[...]).astype(BF16)
    q = _dot(h, wq_ref[...]).astype(BF16)
    km = k_ref[0]
    vm = v_ref[0]
    heads = []
    for hh in range(XATTN_HEADS):
        sl = slice(hh * XATTN_DH, (hh + 1) * XATTN_DH)
        s = _dot_nt(q[:, sl], km[:, sl]) * (XATTN_DH ** -0.5)
        s = s - jnp.max(s, axis=-1, keepdims=True)
        e = jnp.exp(s)
        p = e / jnp.sum(e, axis=-1, keepdims=True)
        heads.append(_dot(p.astype(BF16), vm[:, sl]).astype(BF16))
    o = jnp.concatenate(heads, axis=1)
    x2 = x + _dot(o, wo_ref[...])
    x2_ref[0] = x2

    h3 = _rmsnorm(x2, gf_ref[...])
    h3_ref[0] = h3

    h_hi, h_lo = _split_bf16(h3)
    logits = (_dot_nt(wr_hi_ref[...], h_hi) + _dot_nt(wr_hi_ref[...], h_lo)
              + _dot_nt(wr_lo_ref[...], h_hi)) + br_ref[...]

    e_iota = lax.broadcasted_iota(jnp.int32, (N_EXPERTS, ts), 0)
    work = logits
    vals, idxs, hots = [], [], []
    for _ in range(TOP_K):
        m = jnp.max(work, axis=0, keepdims=True)
        idx = jnp.min(jnp.where(work == m, e_iota, N_EXPERTS), axis=0, keepdims=True)
        hot = e_iota == idx
        vals.append(m)
        idxs.append(idx)
        hots.append(hot)
        work = jnp.where(hot, -jnp.inf, work)
    exps = [jnp.exp(vv - vals[0]) for vv in vals]
    denom = exps[0] + exps[1] + exps[2] + exps[3]
    gate_ref[...] = jnp.concatenate([ee / denom for ee in exps], axis=0)
    topi_ref[...] = jnp.concatenate(idxs, axis=0)

    multi = jnp.where(hots[0] | hots[1] | hots[2] | hots[3], 1.0, 0.0)
    srow = lax.broadcasted_iota(jnp.int32, (ts, ts), 0)
    scol = lax.broadcasted_iota(jnp.int32, (ts, ts), 1)
    strict = jnp.where(srow < scol, 1.0, 0.0).astype(BF16)
    before = _dot(multi.astype(BF16), strict) + carry_ref[...]
    ranks = [jnp.sum(jnp.where(hot, before, 0.0), axis=0, keepdims=True) for hot in hots]
    rank_ref[...] = jnp.concatenate(ranks, axis=0).astype(jnp.int32)
    carry = carry_ref[...] + jnp.sum(multi, axis=1, keepdims=True)
    carry_ref[...] = carry
    cnt_ref[...] = carry.astype(jnp.int32)


def _xattn(x1, kmem, vmem, gx, wq, wo, gf, wr_hi, wr_lo, br):
    B, S, D = x1.shape
    T = B * S
    nj = S // SEQ_TILE
    tok_spec = pl.BlockSpec((1, SEQ_TILE, D), lambda b, j: (b, j, 0))
    mem_spec = pl.BlockSpec((1, MEM_LEN, D), lambda b, j: (b, 0, 0))
    lane_spec = pl.BlockSpec((TOP_K, SEQ_TILE), lambda b, j: (0, b * nj + j))
    consts = (gx, wq, wo, gf, wr_hi, wr_lo, br)
    return pl.pallas_call(
        _xattn_body,
        grid=(B, nj),
        in_specs=[tok_spec, mem_spec, mem_spec] + [_const_spec(c.shape) for c in consts],
        out_specs=[tok_spec, tok_spec, lane_spec, lane_spec, lane_spec,
                   pl.BlockSpec((N_EXPERTS, 1), lambda b, j: (0, 0))],
        out_shape=[
            jax.ShapeDtypeStruct((B, S, D), F32),
            jax.ShapeDtypeStruct((B, S, D), F32),
            jax.ShapeDtypeStruct((TOP_K, T), jnp.int32),
            jax.ShapeDtypeStruct((TOP_K, T), F32),
            jax.ShapeDtypeStruct((TOP_K, T), jnp.int32),
            jax.ShapeDtypeStruct((N_EXPERTS, 1), jnp.int32),
        ],
        scratch_shapes=[pltpu.VMEM((N_EXPERTS, 1), F32)],
        compiler_params=pltpu.CompilerParams(
            dimension_semantics=("arbitrary", "arbitrary"), vmem_limit_bytes=VMEM_LIMIT),
        name="xattn_router",
    )(x1, kmem, vmem, *consts)


def _row_copy(src_ref, src_row, dst_ref, dst_row, sem):
    return pltpu.make_async_copy(src_ref.at[pl.ds(src_row, 1)], dst_ref.at[pl.ds(dst_row, 1)], sem)


def _dispatch_body(zs_ref, nu_ref, dest_ref, h_ref, xb_ref, zbuf_ref, zsem, sem):
    @pl.when(pl.program_id(0) == 0)
    def _():
        zbuf_ref[...] = jnp.zeros_like(zbuf_ref)
        n_blocks = xb_ref.shape[0] // ROW_BLOCK

        def zero_copy(start):
            start = pl.multiple_of(start, ROW_BLOCK)
            return pltpu.make_async_copy(zbuf_ref, xb_ref.at[pl.ds(start, ROW_BLOCK)], zsem)

        for e in range(N_EXPERTS):
            zero_copy(zs_ref[e]).start()
        for e in range(N_EXPERTS):
            zero_copy(zs_ref[e]).wait()

        def tail(blk, c):
            cp = zero_copy(blk * ROW_BLOCK)
            cp.start()
            cp.wait()
            return c

        lax.fori_loop(nu_ref[0], n_blocks, tail, 0)

    def start(r, c):
        for kk in range(TOP_K):
            _row_copy(h_ref, r, xb_ref, dest_ref[0, 0, kk * TOK_TILE + r], sem).start(priority=kk % 2)
        return c

    def wait(r, c):
        for kk in range(TOP_K):
            _row_copy(h_ref, r, xb_ref, dest_ref[0, 0, kk * TOK_TILE + r], sem).wait()
        return c

    lax.fori_loop(0, TOK_TILE, start, 0, unroll=8)
    lax.fori_loop(0, TOK_TILE, wait, 0, unroll=8)


def _dispatch(zero_start, n_used, dest_blocks, h3, n_slots):
    T, D = h3.shape
    return pl.pallas_call(
        _dispatch_body,
        grid_spec=pltpu.PrefetchScalarGridSpec(
            num_scalar_prefetch=2,
            grid=(T // TOK_TILE,),
            in_specs=[
                pl.BlockSpec((1, 1, TOP_K * TOK_TILE), lambda i, zs, nu: (i, 0, 0),
                             memory_space=pltpu.SMEM),
                pl.BlockSpec((TOK_TILE, D), lambda i, zs, nu: (i, 0)),
            ],
            out_specs=pl.BlockSpec(memory_space=pl.ANY),
            scratch_shapes=[pltpu.VMEM((ROW_BLOCK, D), F32),
                            pltpu.SemaphoreType.DMA, pltpu.SemaphoreType.DMA],
        ),
        out_shape=jax.ShapeDtypeStruct((n_slots, D), F32),
        compiler_params=pltpu.CompilerParams(
            dimension_semantics=("arbitrary",), vmem_limit_bytes=VMEM_LIMIT),
        name="dispatch",
    )(zero_start, n_used, dest_blocks, h3)


def _experts_body(be_ref, nu_ref, x_ref, w1_ref, b1_ref, w2_ref, b2_ref, y_ref, w1b_ref, w2b_ref):
    i = pl.program_id(0)

    @pl.when(i < nu_ref[0])
    def _():
        prev = be_ref[jnp.maximum(i - 1, 0)]

        @pl.when((i == 0) | (be_ref[i] != prev))
        def _():
            w1b_ref[...] = w1_ref[0].astype(BF16)
            w2b_ref[...] = w2_ref[0].astype(BF16)

        xb = x_ref[...].astype(BF16)
        hcat = _dot(xb, w1b_ref[...]) + b1_ref[0]
        glu = jnp.minimum(hcat[:, :D_FF], SWIGLU_LIMIT)
        lin = jnp.clip(hcat[:, D_FF:], -SWIGLU_LIMIT, SWIGLU_LIMIT)
        act = (lin + 1.0) * glu * _sigmoid(SWIGLU_ALPHA * glu)
        y_ref[...] = _dot(act.astype(BF16), w2b_ref[...]) + b2_ref[0]

    @pl.when(i >= nu_ref[0])
    def _():
        y_ref[...] = jnp.zeros_like(y_ref)


def _experts(block_e, n_used, xb, w1, b1, w2, b2):
    n_slots, D = xb.shape
    nb = n_slots // ROW_BLOCK

    def row_map(i, be, nu):
        return (jnp.minimum(i, nu[0] - 1), 0)

    def w_map(i, be, nu):
        return (be[i], 0, 0)

    return pl.pallas_call(
        _experts_body,
        grid_spec=pltpu.PrefetchScalarGridSpec(
            num_scalar_prefetch=2,
            grid=(nb,),
            in_specs=[
                pl.BlockSpec((ROW_BLOCK, D), row_map),
                pl.BlockSpec((1, D, 2 * D_FF), w_map),
                pl.BlockSpec((1, 1, 2 * D_FF), w_map),
                pl.BlockSpec((1, D_FF, D), w_map),
                pl.BlockSpec((1, 1, D), w_map),
            ],
            out_specs=pl.BlockSpec((ROW_BLOCK, D), lambda i, be, nu: (i, 0)),
            scratch_shapes=[pltpu.VMEM((D, 2 * D_FF), BF16), pltpu.VMEM((D_FF, D), BF16)],
        ),
        out_shape=jax.ShapeDtypeStruct((n_slots, D), F32),
        compiler_params=pltpu.CompilerParams(
            dimension_semantics=("arbitrary",), vmem_limit_bytes=VMEM_LIMIT),
        name="experts",
    )(block_e, n_used, xb, w1, b1, w2, b2)


def _combine_body(dest_ref, x_ref, gate_ref, g_ref, yb_ref, o_ref, rows_ref, sem):
    def start(r, c):
        for kk in range(TOP_K):
            _row_copy(yb_ref, dest_ref[0, 0, kk * TOK_TILE + r], rows_ref.at[kk], r, sem).start(priority=kk % 2)
        return c

    def wait(r, c):
        for kk in range(TOP_K):
            _row_copy(yb_ref, dest_ref[0, 0, kk * TOK_TILE + r], rows_ref.at[kk], r, sem).wait()
        return c

    lax.fori_loop(0, TOK_TILE, start, 0, unroll=8)
    lax.fori_loop(0, TOK_TILE, wait, 0, unroll=8)

    gates = gate_ref[...]
    y = gates[:, 0:1] * rows_ref[0]
    for kk in range(1, TOP_K):
        y = y + gates[:, kk:kk + 1] * rows_ref[kk]
    o_ref[...] = _rmsnorm(x_ref[...] + y, g_ref[...])


def _combine(dest_blocks, x2, gates_t, g, yb):
    T, D = x2.shape
    return pl.pallas_call(
        _combine_body,
        grid=(T // TOK_TILE,),
        in_specs=[
            pl.BlockSpec((1, 1, TOP_K * TOK_TILE), lambda i: (i, 0, 0), memory_space=pltpu.SMEM),
            pl.BlockSpec((TOK_TILE, D), lambda i: (i, 0)),
            pl.BlockSpec((TOK_TILE, TOP_K), lambda i: (i, 0)),
            _const_spec((1, D)),
            pl.BlockSpec(memory_space=pl.ANY),
        ],
        out_specs=pl.BlockSpec((TOK_TILE, D), lambda i: (i, 0)),
        out_shape=jax.ShapeDtypeStruct((T, D), F32),
        scratch_shapes=[pltpu.VMEM((TOP_K, TOK_TILE, D), F32), pltpu.SemaphoreType.DMA],
        compiler_params=pltpu.CompilerParams(
            dimension_semantics=("arbitrary",), vmem_limit_bytes=VMEM_LIMIT),
        name="combine",
    )(dest_blocks, x2, gates_t, g, yb)


def _row(v):
    return v.reshape(1, -1).astype(F32)


def _layer(x, kv, p):
    B, S, D = x.shape
    T = B * S
    w_in = p["w_in"]
    wa = jnp.zeros((D, GLA_RANK_PAD), F32).at[:, :GLA_GATE_RANK].set(w_in[:, OFF_A:OFF_U])
    wa2 = jnp.zeros((GLA_RANK_PAD, GLA_WIDTH_K), F32).at[:GLA_GATE_RANK].set(p["gla_w_a2"])
    x1 = _mixer(
        x, _row(p["norm_mix_g"]),
        w_in[:, OFF_Q:OFF_A].astype(BF16), wa.astype(BF16), wa2.astype(BF16),
        _row(p["gla_b_a"]), _row(p["gla_out_g"]),
        w_in[:, OFF_U:OFF_SV].astype(BF16), w_in[:, OFF_SV:OFF_G].astype(BF16),
        _row(p["sgu_norm_g"]), _row(p["sgu_norm_b"]), p["sgu_w"].astype(F32), p["sgu_b"].T.astype(F32),
        w_in[:, OFF_G:].astype(BF16), p["w_proj_a"].astype(BF16), p["w_proj_b"].astype(BF16),
        p["w_mix_out"].astype(BF16))

    kmem, vmem = kv
    wr_t = p["w_router"].T.astype(F32)
    wr_hi = wr_t.astype(BF16)
    wr_lo = (wr_t - wr_hi.astype(F32)).astype(BF16)
    x2, h3, topi, gates, rank, counts = _xattn(
        x1, kmem, vmem, _row(p["norm_x_g"]), p["w_xq"].astype(BF16), p["w_xo"].astype(BF16),
        _row(p["norm_ffn_g"]), wr_hi, wr_lo, p["b_router"].reshape(N_EXPERTS, 1).astype(F32))

    counts = counts.reshape(N_EXPERTS)
    padded = ((counts + ROW_BLOCK - 1) // ROW_BLOCK) * ROW_BLOCK
    pad_end = jnp.cumsum(padded)
    pad_start = pad_end - padded
    n_blocks = (T * TOP_K) // ROW_BLOCK + N_EXPERTS
    n_slots = n_blocks * ROW_BLOCK
    n_used = (pad_end[-1] // ROW_BLOCK).astype(jnp.int32)
    blk = jnp.minimum(jnp.arange(n_blocks, dtype=jnp.int32), n_used - 1) * ROW_BLOCK
    block_e = jnp.minimum(jnp.sum(pad_end[None, :] <= blk[:, None], axis=1), N_EXPERTS - 1).astype(jnp.int32)
    e_ids = jnp.arange(N_EXPERTS, dtype=jnp.int32)[:, None, None]
    dest = jnp.sum(jnp.where(topi[None] == e_ids, pad_start[:, None, None], 0), axis=0) + rank
    dest_blocks = (dest.reshape(TOP_K, T // TOK_TILE, TOK_TILE).transpose(1, 0, 2)
                   .reshape(T // TOK_TILE, 1, TOP_K * TOK_TILE).astype(jnp.int32))
    zero_start = jnp.maximum(pad_end - ROW_BLOCK, 0).astype(jnp.int32)

    n_used = n_used.reshape(1)
    xb = _dispatch(zero_start, n_used, dest_blocks, h3.reshape(T, D), n_slots)
    yb = _experts(block_e, n_used, xb,
                  p["w_e1"], p["b_e1"].reshape(N_EXPERTS, 1, 2 * D_FF),
                  p["w_e2"], p["b_e2"].reshape(N_EXPERTS, 1, D))
    return dest_blocks, x2.reshape(T, D), gates.T, yb


def kernel(x, mem, norm_mix_g, w_in, gla_w_a2, gla_b_a, gla_out_g, sgu_norm_g, sgu_norm_b, sgu_w, sgu_b, w_proj_a, w_proj_b, w_mix_out, norm_x_g, norm_mem_g, w_xq, w_xk, w_xv, w_xo, norm_ffn_g, w_router, b_router, w_e1, b_e1, w_e2, b_e2, norm_final_g):
    B, S, D = x.shape
    depth = w_in.shape[0]
    assert depth == 1, "the final norm is fused into the last layer's combine step"
    stacked = dict(norm_mix_g=norm_mix_g, w_in=w_in, gla_w_a2=gla_w_a2, gla_b_a=gla_b_a,
                   gla_out_g=gla_out_g, sgu_norm_g=sgu_norm_g, sgu_norm_b=sgu_norm_b, sgu_w=sgu_w,
                   sgu_b=sgu_b, w_proj_a=w_proj_a, w_proj_b=w_proj_b, w_mix_out=w_mix_out,
                   norm_x_g=norm_x_g, w_xq=w_xq, w_xo=w_xo, norm_ffn_g=norm_ffn_g,
                   w_router=w_router, b_router=b_router, w_e1=w_e1, b_e1=b_e1, w_e2=w_e2, b_e2=b_e2)
    p = {name: v[0] for name, v in stacked.items()}
    kv = _xattn_kv(mem, _row(norm_mem_g[0]), w_xk[0].astype(BF16), w_xv[0].astype(BF16))
    dest_blocks, x2, gates_t, yb = _layer(x, kv, p)
    out = _combine(dest_blocks, x2, gates_t, _row(norm_final_g), yb)
    return out.reshape(B, S, D)
```

```python
import functools

import jax
import jax.numpy as jnp
from jax import lax
from jax.experimental import pallas as pl
from jax.experimental.pallas import tpu as pltpu

F32 = jnp.float32
BF16 = jnp.bfloat16

D_MODEL = 1024
MEM_LEN = 256
EPS = 1e-5

GLA_HEADS = 4
GLA_DV = 128
GLA_DK = 64
GLA_WIDTH_K = GLA_HEADS * GLA_DK
GLA_WIDTH_V = GLA_HEADS * GLA_DV
GLA_GATE_RANK = 16
GLA_RANK_PAD = 128
GLA_TAU = 16.0
GLA_CHUNK = 64

SGU_WIDTH = 512
SGU_GROUPS = 4
SGU_GROUP_DIM = SGU_WIDTH // SGU_GROUPS
SGU_CHUNK = 128

OFF_Q = 0
OFF_K = OFF_Q + GLA_WIDTH_K
OFF_V = OFF_K + GLA_WIDTH_K
OFF_R = OFF_V + GLA_WIDTH_V
OFF_A = OFF_R + GLA_WIDTH_V
OFF_U = OFF_A + GLA_GATE_RANK
OFF_SV = OFF_U + SGU_WIDTH
OFF_G = OFF_SV + SGU_WIDTH

XATTN_HEADS = 4
XATTN_DH = D_MODEL // XATTN_HEADS

N_EXPERTS = 32
TOP_K = 4
D_FF = D_MODEL
SWIGLU_ALPHA = 1.702
SWIGLU_LIMIT = 7.0

SEQ_TILE = 512
ROW_BLOCK = 256
TOK_TILE = 256
VMEM_LIMIT = 56 * 1024 * 1024


def _rmsnorm(x, g):
    return x * lax.rsqrt(jnp.mean(x * x, axis=-1, keepdims=True) + EPS) * g


def _sigmoid(x):
    return 1.0 / (1.0 + jnp.exp(-x))


def _log_sigmoid(x):
    return jnp.minimum(x, 0.0) - jnp.log1p(jnp.exp(-jnp.abs(x)))


def _gelu(x):
    return 0.5 * x * (1.0 + lax.erf(x * (2.0 ** -0.5)))


def _dot(a, b):
    return jnp.dot(a, b, preferred_element_type=F32)


def _dot_nt(a, b):
    return lax.dot_general(a, b, (((1,), (1,)), ((), ())), preferred_element_type=F32)


def _dot_tn(a, b):
    return lax.dot_general(a, b, (((0,), (0,)), ((), ())), preferred_element_type=F32)


def _split_bf16(x):
    hi = x.astype(BF16)
    lo = (x - hi.astype(F32)).astype(BF16)
    return hi, lo


def _const_spec(shape):
    zeros = (0,) * len(shape)
    return pl.BlockSpec(shape, lambda *_: zeros, pipeline_mode=pl.Buffered(1))


def _xattn_kv_body(mem_ref, g_ref, wk_ref, wv_ref, k_ref, v_ref):
    m = _rmsnorm(mem_ref[0], g_ref[...]).astype(BF16)
    k_ref[0] = _dot(m, wk_ref[...]).astype(BF16)
    v_ref[0] = _dot(m, wv_ref[...]).astype(BF16)


def _xattn_kv(mem, g, wk, wv):
    B = mem.shape[0]
    return pl.pallas_call(
        _xattn_kv_body,
        grid=(B,),
        in_specs=[
            pl.BlockSpec((1, MEM_LEN, D_MODEL), lambda b: (b, 0, 0)),
            _const_spec((1, D_MODEL)),
            _const_spec((D_MODEL, D_MODEL)),
            _const_spec((D_MODEL, D_MODEL)),
        ],
        out_specs=[
            pl.BlockSpec((1, MEM_LEN, D_MODEL), lambda b: (b, 0, 0)),
            pl.BlockSpec((1, MEM_LEN, D_MODEL), lambda b: (b, 0, 0)),
        ],
        out_shape=[jax.ShapeDtypeStruct((B, MEM_LEN, D_MODEL), BF16)] * 2,
        compiler_params=pltpu.CompilerParams(
            dimension_semantics=("arbitrary",), vmem_limit_bytes=VMEM_LIMIT),
        name="xattn_kv",
    )(mem, g, wk, wv)


def _gla_branch(h, wqkvr_ref, wa_ref, wa2_ref, ba_ref, outg_ref, st_ref):
    ts = h.shape[0]
    n_chunks = ts // GLA_CHUNK
    p = _dot(h, wqkvr_ref[...])
    q = p[:, OFF_Q:OFF_K] * (GLA_DK ** -0.5)
    k = p[:, OFF_K:OFF_V]
    v = p[:, OFF_V:OFF_R].astype(BF16)
    r = p[:, OFF_R:OFF_A]

    a_low = _dot(h, wa_ref[...]).astype(BF16)
    log_a = _log_sigmoid(_dot(a_low, wa2_ref[...]) + ba_ref[...]) * (1.0 / GLA_TAU)

    row = lax.broadcasted_iota(jnp.int32, (ts, ts), 0)
    col = lax.broadcasted_iota(jnp.int32, (ts, ts), 1)
    tri = jnp.where((col <= row) & ((col >> 6) == (row >> 6)), 1.0, 0.0).astype(BF16)
    la_hi, la_lo = _split_bf16(log_a)
    cum = _dot(tri, la_hi) + _dot(tri, la_lo)

    q_in = (q * jnp.exp(cum)).astype(BF16)
    k_in = (k * jnp.exp(-cum)).astype(BF16)

    lane_k = lax.broadcasted_iota(jnp.int32, (1, GLA_WIDTH_K), 1)
    head_masks = [jnp.where((lane_k >> 6) == hh, 1.0, 0.0).astype(BF16) for hh in range(GLA_HEADS)]
    r4 = lax.broadcasted_iota(jnp.int32, (GLA_HEADS * GLA_CHUNK, GLA_CHUNK), 0)
    c4 = lax.broadcasted_iota(jnp.int32, (GLA_HEADS * GLA_CHUNK, GLA_CHUNK), 1)
    causal4 = c4 <= (r4 & (GLA_CHUNK - 1))
    sr = lax.broadcasted_iota(jnp.int32, (GLA_WIDTH_V, GLA_WIDTH_K), 0)
    sc = lax.broadcasted_iota(jnp.int32, (GLA_WIDTH_V, GLA_WIDTH_K), 1)
    state_mask = jnp.where((sr >> 7) == (sc >> 6), 1.0, 0.0)

    state = st_ref[...]
    outs = []
    for n in range(n_chunks):
        lo, hi = n * GLA_CHUNK, (n + 1) * GLA_CHUNK
        cum_c = cum[lo:hi]
        last = cum[hi - 1:hi]
        q_c = q_in[lo:hi]
        k_c = k_in[lo:hi]
        v_c = v[lo:hi]
        k_out = (k[lo:hi] * jnp.exp(last - cum_c)).astype(BF16)
        q4 = jnp.concatenate([q_c * head_masks[hh] for hh in range(GLA_HEADS)], axis=0)
        att4 = jnp.where(causal4, _dot_nt(q4, k_c), 0.0).astype(BF16)
        oi4 = _dot(att4, v_c)
        o_intra = jnp.concatenate(
            [oi4[hh * GLA_CHUNK:(hh + 1) * GLA_CHUNK, hh * GLA_DV:(hh + 1) * GLA_DV]
             for hh in range(GLA_HEADS)], axis=1)
        o_inter = _dot_nt(q_c, state.astype(BF16))
        outs.append(o_intra + o_inter)
        state = state * jnp.exp(last) + _dot_tn(v_c, k_out) * state_mask
    st_ref[...] = state

    o = jnp.concatenate(outs, axis=0)
    outg = outg_ref[...]
    normed = []
    for hh in range(GLA_HEADS):
        sl = slice(hh * GLA_DV, (hh + 1) * GLA_DV)
        normed.append(_rmsnorm(o[:, sl], outg[:, sl]))
    o = jnp.concatenate(normed, axis=1)
    return o * (r * _sigmoid(r))


def _sgu_branch(h, wu_ref, wsv_ref, ng_ref, nb_ref, sw_ref, sb_ref):
    ts = h.shape[0]
    n_chunks = ts // SGU_CHUNK
    u = _gelu(_dot(h, wu_ref[...]))
    v = _gelu(_dot(h, wsv_ref[...]))
    mu = jnp.mean(v, axis=-1, keepdims=True)
    vc = v - mu
    var = jnp.mean(vc * vc, axis=-1, keepdims=True)
    v = (vc * lax.rsqrt(var + EPS) * ng_ref[...] + nb_ref[...]).astype(BF16)

    row = lax.broadcasted_iota(jnp.int32, (SGU_CHUNK, SGU_CHUNK), 0)
    col = lax.broadcasted_iota(jnp.int32, (SGU_CHUNK, SGU_CHUNK), 1)
    causal = col <= row
    sb = sb_ref[...]
    per_group = []
    for g in range(SGU_GROUPS):
        w = jnp.where(causal, sw_ref[g], 0.0).astype(BF16)
        gs = slice(g * SGU_GROUP_DIM, (g + 1) * SGU_GROUP_DIM)
        vcat = jnp.concatenate(
            [v[c * SGU_CHUNK:(c + 1) * SGU_CHUNK, gs] for c in range(n_chunks)], axis=1)
        per_group.append(_dot(w, vcat) + sb[:, g:g + 1])
    rows = []
    for c in range(n_chunks):
        cs = slice(c * SGU_GROUP_DIM, (c + 1) * SGU_GROUP_DIM)
        rows.append(jnp.concatenate([per_group[g][:, cs] for g in range(SGU_GROUPS)], axis=1))
    mixed = jnp.concatenate(rows, axis=0)
    return u * mixed


def _mixer_body(x_ref, g_ref, wqkvr_ref, wa_ref, wa2_ref, ba_ref, outg_ref,
                wu_ref, wsv_ref, ng_ref, nb_ref, sw_ref, sb_ref,
                wg_ref, wpa_ref, wpb_ref, wmix_ref, o_ref, st_ref):
    @pl.when(pl.program_id(1) == 0)
    def _():
        st_ref[...] = jnp.zeros_like(st_ref)

    x = x_ref[0]
    h = _rmsnorm(x, g_ref[...]).astype(BF16)
    ya_in = _gla_branch(h, wqkvr_ref, wa_ref, wa2_ref, ba_ref, outg_ref, st_ref)
    y_a = _dot(ya_in.astype(BF16), wpa_ref[...])
    yb_in = _sgu_branch(h, wu_ref, wsv_ref, ng_ref, nb_ref, sw_ref, sb_ref)
    y_b = _dot(yb_in.astype(BF16), wpb_ref[...])
    gates = _sigmoid(_dot(h, wg_ref[...]))
    mix = gates[:, :D_MODEL] * y_a + gates[:, D_MODEL:] * y_b
    o_ref[0] = x + _dot(mix.astype(BF16), wmix_ref[...])


def _mixer(x, g, wqkvr, wa, wa2, ba, outg, wu, wsv, ng, nb, sw, sb, wg, wpa, wpb, wmix):
    B, S, D = x.shape
    consts = (g, wqkvr, wa, wa2, ba, outg, wu, wsv, ng, nb, sw, sb, wg, wpa, wpb, wmix)
    return pl.pallas_call(
        _mixer_body,
        grid=(B, S // SEQ_TILE),
        in_specs=[pl.BlockSpec((1, SEQ_TILE, D), lambda b, j: (b, j, 0))]
                 + [_const_spec(c.shape) for c in consts],
        out_specs=pl.BlockSpec((1, SEQ_TILE, D), lambda b, j: (b, j, 0)),
        out_shape=jax.ShapeDtypeStruct((B, S, D), F32),
        scratch_shapes=[pltpu.VMEM((GLA_WIDTH_V, GLA_WIDTH_K), F32)],
        compiler_params=pltpu.CompilerParams(
            dimension_semantics=("arbitrary", "arbitrary"), vmem_limit_bytes=VMEM_LIMIT),
        name="mixer",
    )(x, *consts)


def _xattn_body(x_ref, k_ref, v_ref, gx_ref, wq_ref, wo_ref, gf_ref, wr_hi_ref, wr_lo_ref, br_ref,
                x2_ref, h3_ref, topi_ref, gate_ref, rank_ref, cnt_ref, carry_ref):
    first = (pl.program_id(0) == 0) & (pl.program_id(1) == 0)

    @pl.when(first)
    def _():
        carry_ref[...] = jnp.zeros_like(carry_ref)

    x = x_ref[0]
    ts = x.shape[0]
    h = _rmsnorm(x, gx_ref[...]).astype(BF16)
    q = _dot(h, wq_ref[...]).astype(BF16)
    km = k_ref[0]
    vm = v_ref[0]
    heads = []
    for hh in range(XATTN_HEADS):
        sl = slice(hh * XATTN_DH, (hh + 1) * XATTN_DH)
        s = _dot_nt(q[:, sl], km[:, sl]) * (XATTN_DH ** -0.5)
        s = s - jnp.max(s, axis=-1, keepdims=True)
        e = jnp.exp(s)
        p = e / jnp.sum(e, axis=-1, keepdims=True)
        heads.append(_dot(p.astype(BF16), vm[:, sl]).astype(BF16))
    o = jnp.concatenate(heads, axis=1)
    x2 = x + _dot(o, wo_ref[...])
    x2_ref[0] = x2

    h3 = _rmsnorm(x2, gf_ref[...])
    h3_ref[0] = h3

    h_hi, h_lo = _split_bf16(h3)
    logits = (_dot_nt(wr_hi_ref[...], h_hi) + _dot_nt(wr_hi_ref[...], h_lo)
              + _dot_nt(wr_lo_ref[...], h_hi)) + br_ref[...]

    e_iota = lax.broadcasted_iota(jnp.int32, (N_EXPERTS, ts), 0)
    work = logits
    vals, idxs, hots = [], [], []
    for _ in range(TOP_K):
        m = jnp.max(work, axis=0, keepdims=True)
        idx = jnp.min(jnp.where(work == m, e_iota, N_EXPERTS), axis=0, keepdims=True)
        hot = e_iota == idx
        vals.append(m)
        idxs.append(idx)
        hots.append(hot)
        work = jnp.where(hot, -jnp.inf, work)
    exps = [jnp.exp(vv - vals[0]) for vv in vals]
    denom = exps[0] + exps[1] + exps[2] + exps[3]
    gate_ref[...] = jnp.concatenate([ee / denom for ee in exps], axis=0)
    topi_ref[...] = jnp.concatenate(idxs, axis=0)

    multi = jnp.where(hots[0] | hots[1] | hots[2] | hots[3], 1.0, 0.0)
    srow = lax.broadcasted_iota(jnp.int32, (ts, ts), 0)
    scol = lax.broadcasted_iota(jnp.int32, (ts, ts), 1)
    strict = jnp.where(srow < scol, 1.0, 0.0).astype(BF16)
    before = _dot(multi.astype(BF16), strict) + carry_ref[...]
    ranks = [jnp.sum(jnp.where(hot, before, 0.0), axis=0, keepdims=True) for hot in hots]
    rank_ref[...] = jnp.concatenate(ranks, axis=0).astype(jnp.int32)
    carry = carry_ref[...] + jnp.sum(multi, axis=1, keepdims=True)
    carry_ref[...] = carry
    cnt_ref[...] = carry.astype(jnp.int32)


def _xattn(x1, kmem, vmem, gx, wq, wo, gf, wr_hi, wr_lo, br):
    B, S, D = x1.shape
    T = B * S
    nj = S // SEQ_TILE
    tok_spec = pl.BlockSpec((1, SEQ_TILE, D), lambda b, j: (b, j, 0))
    mem_spec = pl.BlockSpec((1, MEM_LEN, D), lambda b, j: (b, 0, 0))
    lane_spec = pl.BlockSpec((TOP_K, SEQ_TILE), lambda b, j: (0, b * nj + j))
    consts = (gx, wq, wo, gf, wr_hi, wr_lo, br)
    return pl.pallas_call(
        _xattn_body,
        grid=(B, nj),
        in_specs=[tok_spec, mem_spec, mem_spec] + [_const_spec(c.shape) for c in consts],
        out_specs=[tok_spec, tok_spec, lane_spec, lane_spec, lane_spec,
                   pl.BlockSpec((N_EXPERTS, 1), lambda b, j: (0, 0))],
        out_shape=[
            jax.ShapeDtypeStruct((B, S, D), F32),
            jax.ShapeDtypeStruct((B, S, D), F32),
            jax.ShapeDtypeStruct((TOP_K, T), jnp.int32),
            jax.ShapeDtypeStruct((TOP_K, T), F32),
            jax.ShapeDtypeStruct((TOP_K, T), jnp.int32),
            jax.ShapeDtypeStruct((N_EXPERTS, 1), jnp.int32),
        ],
        scratch_shapes=[pltpu.VMEM((N_EXPERTS, 1), F32)],
        compiler_params=pltpu.CompilerParams(
            dimension_semantics=("arbitrary", "arbitrary"), vmem_limit_bytes=VMEM_LIMIT),
        name="xattn_router",
    )(x1, kmem, vmem, *consts)


def _row_copy(src_ref, src_row, dst_ref, dst_row, sem):
    return pltpu.make_async_copy(src_ref.at[pl.ds(src_row, 1)], dst_ref.at[pl.ds(dst_row, 1)], sem)


def _dispatch_body(zs_ref, nu_ref, dest_ref, h_ref, xb_ref, zbuf_ref, zsem, sem):
    @pl.when(pl.program_id(0) == 0)
    def _():
        zbuf_ref[...] = jnp.zeros_like(zbuf_ref)
        n_blocks = xb_ref.shape[0] // ROW_BLOCK

        def zero_copy(start):
            start = pl.multiple_of(start, ROW_BLOCK)
            return pltpu.make_async_copy(zbuf_ref, xb_ref.at[pl.ds(start, ROW_BLOCK)], zsem)

        for e in range(N_EXPERTS):
            zero_copy(zs_ref[e]).start()
        for e in range(N_EXPERTS):
            zero_copy(zs_ref[e]).wait()

        def tail(blk, c):
            cp = zero_copy(blk * ROW_BLOCK)
            cp.start()
            cp.wait()
            return c

        lax.fori_loop(nu_ref[0], n_blocks, tail, 0)

    def start(r, c):
        for kk in range(TOP_K):
            _row_copy(h_ref, r, xb_ref, dest_ref[0, 0, kk * TOK_TILE + r], sem).start(priority=kk % 2)
        return c

    lax.fori_loop(0, TOK_TILE, start, 0, unroll=8)
    for kk in range(TOP_K):
        pltpu.make_async_copy(h_ref, xb_ref.at[pl.ds(0, TOK_TILE)], sem).wait()


def _dispatch(zero_start, n_used, dest_blocks, h3, n_slots):
    T, D = h3.shape
    return pl.pallas_call(
        _dispatch_body,
        grid_spec=pltpu.PrefetchScalarGridSpec(
            num_scalar_prefetch=2,
            grid=(T // TOK_TILE,),
            in_specs=[
                pl.BlockSpec((1, 1, TOP_K * TOK_TILE), lambda i, zs, nu: (i, 0, 0),
                             memory_space=pltpu.SMEM),
                pl.BlockSpec((TOK_TILE, D), lambda i, zs, nu: (i, 0)),
            ],
            out_specs=pl.BlockSpec(memory_space=pl.ANY),
            scratch_shapes=[pltpu.VMEM((ROW_BLOCK, D), F32),
                            pltpu.SemaphoreType.DMA, pltpu.SemaphoreType.DMA],
        ),
        out_shape=jax.ShapeDtypeStruct((n_slots, D), F32),
        compiler_params=pltpu.CompilerParams(
            dimension_semantics=("arbitrary",), vmem_limit_bytes=VMEM_LIMIT),
        name="dispatch",
    )(zero_start, n_used, dest_blocks, h3)


def _experts_body(be_ref, nu_ref, x_ref, w1_ref, b1_ref, w2_ref, b2_ref, y_ref, w1b_ref, w2b_ref):
    i = pl.program_id(0)

    @pl.when(i < nu_ref[0])
    def _():
        prev = be_ref[jnp.maximum(i - 1, 0)]

        @pl.when((i == 0) | (be_ref[i] != prev))
        def _():
            w1b_ref[...] = w1_ref[0].astype(BF16)
            w2b_ref[...] = w2_ref[0].astype(BF16)

        xb = x_ref[...].astype(BF16)
        hcat = _dot(xb, w1b_ref[...]) + b1_ref[0]
        glu = jnp.minimum(hcat[:, :D_FF], SWIGLU_LIMIT)
        lin = jnp.clip(hcat[:, D_FF:], -SWIGLU_LIMIT, SWIGLU_LIMIT)
        act = (lin + 1.0) * glu * _sigmoid(SWIGLU_ALPHA * glu)
        y_ref[...] = _dot(act.astype(BF16), w2b_ref[...]) + b2_ref[0]

    @pl.when(i >= nu_ref[0])
    def _():
        y_ref[...] = jnp.zeros_like(y_ref)


def _experts(block_e, n_used, xb, w1, b1, w2, b2):
    n_slots, D = xb.shape
    nb = n_slots // ROW_BLOCK

    def row_map(i, be, nu):
        return (jnp.minimum(i, nu[0] - 1), 0)

    def w_map(i, be, nu):
        return (be[i], 0, 0)

    return pl.pallas_call(
        _experts_body,
        grid_spec=pltpu.PrefetchScalarGridSpec(
            num_scalar_prefetch=2,
            grid=(nb,),
            in_specs=[
                pl.BlockSpec((ROW_BLOCK, D), row_map),
                pl.BlockSpec((1, D, 2 * D_FF), w_map),
                pl.BlockSpec((1, 1, 2 * D_FF), w_map),
                pl.BlockSpec((1, D_FF, D), w_map),
                pl.BlockSpec((1, 1, D), w_map),
            ],
            out_specs=pl.BlockSpec((ROW_BLOCK, D), lambda i, be, nu: (i, 0)),
            scratch_shapes=[pltpu.VMEM((D, 2 * D_FF), BF16), pltpu.VMEM((D_FF, D), BF16)],
        ),
        out_shape=jax.ShapeDtypeStruct((n_slots, D), F32),
        compiler_params=pltpu.CompilerParams(
            dimension_semantics=("arbitrary",), vmem_limit_bytes=VMEM_LIMIT),
        name="experts",
    )(block_e, n_used, xb, w1, b1, w2, b2)


def _combine_body(dest_ref, x_ref, gate_ref, g_ref, yb_ref, o_ref, rows_ref, sem):
    def start(r, c):
        for kk in range(TOP_K):
            _row_copy(yb_ref, dest_ref[0, 0, kk * TOK_TILE + r], rows_ref.at[kk], r, sem).start(priority=kk % 2)
        return c

    lax.fori_loop(0, TOK_TILE, start, 0, unroll=8)
    for kk in range(TOP_K):
        pltpu.make_async_copy(yb_ref.at[pl.ds(0, TOK_TILE)], rows_ref.at[kk], sem).wait()

    gates = gate_ref[...]
    y = gates[:, 0:1] * rows_ref[0]
    for kk in range(1, TOP_K):
        y = y + gates[:, kk:kk + 1] * rows_ref[kk]
    o_ref[...] = _rmsnorm(x_ref[...] + y, g_ref[...])


def _combine(dest_blocks, x2, gates_t, g, yb):
    T, D = x2.shape
    return pl.pallas_call(
        _combine_body,
        grid=(T // TOK_TILE,),
        in_specs=[
            pl.BlockSpec((1, 1, TOP_K * TOK_TILE), lambda i: (i, 0, 0), memory_space=pltpu.SMEM),
            pl.BlockSpec((TOK_TILE, D), lambda i: (i, 0)),
            pl.BlockSpec((TOK_TILE, TOP_K), lambda i: (i, 0)),
            _const_spec((1, D)),
            pl.BlockSpec(memory_space=pl.ANY),
        ],
        out_specs=pl.BlockSpec((TOK_TILE, D), lambda i: (i, 0)),
        out_shape=jax.ShapeDtypeStruct((T, D), F32),
        scratch_shapes=[pltpu.VMEM((TOP_K, TOK_TILE, D), F32), pltpu.SemaphoreType.DMA],
        compiler_params=pltpu.CompilerParams(
            dimension_semantics=("arbitrary",), vmem_limit_bytes=VMEM_LIMIT),
        name="combine",
    )(dest_blocks, x2, gates_t, g, yb)


def _row(v):
    return v.reshape(1, -1).astype(F32)


def _layer(x, kv, p):
    B, S, D = x.shape
    T = B * S
    w_in = p["w_in"]
    wa = jnp.zeros((D, GLA_RANK_PAD), F32).at[:, :GLA_GATE_RANK].set(w_in[:, OFF_A:OFF_U])
    wa2 = jnp.zeros((GLA_RANK_PAD, GLA_WIDTH_K), F32).at[:GLA_GATE_RANK].set(p["gla_w_a2"])
    x1 = _mixer(
        x, _row(p["norm_mix_g"]),
        w_in[:, OFF_Q:OFF_A].astype(BF16), wa.astype(BF16), wa2.astype(BF16),
        _row(p["gla_b_a"]), _row(p["gla_out_g"]),
        w_in[:, OFF_U:OFF_SV].astype(BF16), w_in[:, OFF_SV:OFF_G].astype(BF16),
        _row(p["sgu_norm_g"]), _row(p["sgu_norm_b"]), p["sgu_w"].astype(F32), p["sgu_b"].T.astype(F32),
        w_in[:, OFF_G:].astype(BF16), p["w_proj_a"].astype(BF16), p["w_proj_b"].astype(BF16),
        p["w_mix_out"].astype(BF16))

    kmem, vmem = kv
    wr_t = p["w_router"].T.astype(F32)
    wr_hi = wr_t.astype(BF16)
    wr_lo = (wr_t - wr_hi.astype(F32)).astype(BF16)
    x2, h3, topi, gates, rank, counts = _xattn(
        x1, kmem, vmem, _row(p["norm_x_g"]), p["w_xq"].astype(BF16), p["w_xo"].astype(BF16),
        _row(p["norm_ffn_g"]), wr_hi, wr_lo, p["b_router"].reshape(N_EXPERTS, 1).astype(F32))

    counts = counts.reshape(N_EXPERTS)
    padded = ((counts + ROW_BLOCK - 1) // ROW_BLOCK) * ROW_BLOCK
    pad_end = jnp.cumsum(padded)
    pad_start = pad_end - padded
    n_blocks = (T * TOP_K) // ROW_BLOCK + N_EXPERTS
    n_slots = n_blocks * ROW_BLOCK
    n_used = (pad_end[-1] // ROW_BLOCK).astype(jnp.int32)
    blk = jnp.minimum(jnp.arange(n_blocks, dtype=jnp.int32), n_used - 1) * ROW_BLOCK
    block_e = jnp.minimum(jnp.sum(pad_end[None, :] <= blk[:, None], axis=1), N_EXPERTS - 1).astype(jnp.int32)
    e_ids = jnp.arange(N_EXPERTS, dtype=jnp.int32)[:, None, None]
    dest = jnp.sum(jnp.where(topi[None] == e_ids, pad_start[:, None, None], 0), axis=0) + rank
    dest_blocks = (dest.reshape(TOP_K, T // TOK_TILE, TOK_TILE).transpose(1, 0, 2)
                   .reshape(T // TOK_TILE, 1, TOP_K * TOK_TILE).astype(jnp.int32))
    zero_start = jnp.maximum(pad_end - ROW_BLOCK, 0).astype(jnp.int32)

    n_used = n_used.reshape(1)
    xb = _dispatch(zero_start, n_used, dest_blocks, h3.reshape(T, D), n_slots)
    yb = _experts(block_e, n_used, xb,
                  p["w_e1"], p["b_e1"].reshape(N_EXPERTS, 1, 2 * D_FF),
                  p["w_e2"], p["b_e2"].reshape(N_EXPERTS, 1, D))
    return dest_blocks, x2.reshape(T, D), gates.T, yb


def kernel(x, mem, norm_mix_g, w_in, gla_w_a2, gla_b_a, gla_out_g, sgu_norm_g, sgu_norm_b, sgu_w, sgu_b, w_proj_a, w_proj_b, w_mix_out, norm_x_g, norm_mem_g, w_xq, w_xk, w_xv, w_xo, norm_ffn_g, w_router, b_router, w_e1, b_e1, w_e2, b_e2, norm_final_g):
    B, S, D = x.shape
    depth = w_in.shape[0]
    assert depth == 1, "the final norm is fused into the last layer's combine step"
    stacked = dict(norm_mix_g=norm_mix_g, w_in=w_in, gla_w_a2=gla_w_a2, gla_b_a=gla_b_a,
                   gla_out_g=gla_out_g, sgu_norm_g=sgu_norm_g, sgu_norm_b=sgu_norm_b, sgu_w=sgu_w,
                   sgu_b=sgu_b, w_proj_a=w_proj_a, w_proj_b=w_proj_b, w_mix_out=w_mix_out,
                   norm_x_g=norm_x_g, w_xq=w_xq, w_xo=w_xo, norm_ffn_g=norm_ffn_g,
                   w_router=w_router, b_router=b_router, w_e1=w_e1, b_e1=b_e1, w_e2=w_e2, b_e2=b_e2)
    p = {name: v[0] for name, v in stacked.items()}
    kv = _xattn_kv(mem, _row(norm_mem_g[0]), w_xk[0].astype(BF16), w_xv[0].astype(BF16))
    dest_blocks, x2, gates_t, yb = _layer(x, kv, p)
    out = _combine(dest_blocks, x2, gates_t, _row(norm_final_g), yb)
    return out.reshape(B, S, D)
```

```python
import functools

import jax
import jax.numpy as jnp
from jax import lax
from jax.experimental import pallas as pl
from jax.experimental.pallas import tpu as pltpu

F32 = jnp.float32
BF16 = jnp.bfloat16

D_MODEL = 1024
MEM_LEN = 256
EPS = 1e-5

GLA_HEADS = 4
GLA_DV = 128
GLA_DK = 64
GLA_WIDTH_K = GLA_HEADS * GLA_DK
GLA_WIDTH_V = GLA_HEADS * GLA_DV
GLA_GATE_RANK = 16
GLA_RANK_PAD = 128
GLA_TAU = 16.0
GLA_CHUNK = 64

SGU_WIDTH = 512
SGU_GROUPS = 4
SGU_GROUP_DIM = SGU_WIDTH // SGU_GROUPS
SGU_CHUNK = 128

OFF_Q = 0
OFF_K = OFF_Q + GLA_WIDTH_K
OFF_V = OFF_K + GLA_WIDTH_K
OFF_R = OFF_V + GLA_WIDTH_V
OFF_A = OFF_R + GLA_WIDTH_V
OFF_U = OFF_A + GLA_GATE_RANK
OFF_SV = OFF_U + SGU_WIDTH
OFF_G = OFF_SV + SGU_WIDTH

XATTN_HEADS = 4
XATTN_DH = D_MODEL // XATTN_HEADS

N_EXPERTS = 32
TOP_K = 4
D_FF = D_MODEL
SWIGLU_ALPHA = 1.702
SWIGLU_LIMIT = 7.0

SEQ_TILE = 512
ROW_BLOCK = 256
TOK_TILE = 512
DISPATCH_TILE = 1024
VMEM_LIMIT = 56 * 1024 * 1024


def _rmsnorm(x, g):
    return x * lax.rsqrt(jnp.mean(x * x, axis=-1, keepdims=True) + EPS) * g


def _sigmoid(x):
    return 1.0 / (1.0 + jnp.exp(-x))


def _log_sigmoid(x):
    return jnp.minimum(x, 0.0) - jnp.log1p(jnp.exp(-jnp.abs(x)))


def _gelu(x):
    return 0.5 * x * (1.0 + lax.erf(x * (2.0 ** -0.5)))


def _dot(a, b):
    return jnp.dot(a, b, preferred_element_type=F32)


def _dot_nt(a, b):
    return lax.dot_general(a, b, (((1,), (1,)), ((), ())), preferred_element_type=F32)


def _dot_tn(a, b):
    return lax.dot_general(a, b, (((0,), (0,)), ((), ())), preferred_element_type=F32)


def _split_bf16(x):
    hi = x.astype(BF16)
    lo = (x - hi.astype(F32)).astype(BF16)
    return hi, lo


def _const_spec(shape):
    zeros = (0,) * len(shape)
    return pl.BlockSpec(shape, lambda *_: zeros, pipeline_mode=pl.Buffered(1))


def _xattn_kv_body(mem_ref, g_ref, wk_ref, wv_ref, k_ref, v_ref):
    m = _rmsnorm(mem_ref[0], g_ref[...]).astype(BF16)
    k_ref[0] = _dot(m, wk_ref[...]).astype(BF16)
    v_ref[0] = _dot(m, wv_ref[...]).astype(BF16)


def _xattn_kv(mem, g, wk, wv):
    B = mem.shape[0]
    return pl.pallas_call(
        _xattn_kv_body,
        grid=(B,),
        in_specs=[
            pl.BlockSpec((1, MEM_LEN, D_MODEL), lambda b: (b, 0, 0)),
            _const_spec((1, D_MODEL)),
            _const_spec((D_MODEL, D_MODEL)),
            _const_spec((D_MODEL, D_MODEL)),
        ],
        out_specs=[
            pl.BlockSpec((1, MEM_LEN, D_MODEL), lambda b: (b, 0, 0)),
            pl.BlockSpec((1, MEM_LEN, D_MODEL), lambda b: (b, 0, 0)),
        ],
        out_shape=[jax.ShapeDtypeStruct((B, MEM_LEN, D_MODEL), BF16)] * 2,
        compiler_params=pltpu.CompilerParams(
            dimension_semantics=("arbitrary",), vmem_limit_bytes=VMEM_LIMIT),
        name="xattn_kv",
    )(mem, g, wk, wv)


def _gla_branch(h, wqkvr_ref, wa_ref, wa2_ref, ba_ref, outg_ref, st_ref):
    ts = h.shape[0]
    n_chunks = ts // GLA_CHUNK
    p = _dot(h, wqkvr_ref[...])
    q = p[:, OFF_Q:OFF_K] * (GLA_DK ** -0.5)
    k = p[:, OFF_K:OFF_V]
    v = p[:, OFF_V:OFF_R].astype(BF16)
    r = p[:, OFF_R:OFF_A]

    a_low = _dot(h, wa_ref[...]).astype(BF16)
    log_a = _log_sigmoid(_dot(a_low, wa2_ref[...]) + ba_ref[...]) * (1.0 / GLA_TAU)

    row = lax.broadcasted_iota(jnp.int32, (ts, ts), 0)
    col = lax.broadcasted_iota(jnp.int32, (ts, ts), 1)
    tri = jnp.where((col <= row) & ((col >> 6) == (row >> 6)), 1.0, 0.0).astype(BF16)
    la_hi, la_lo = _split_bf16(log_a)
    cum = _dot(tri, la_hi) + _dot(tri, la_lo)

    q_in = (q * jnp.exp(cum)).astype(BF16)
    k_in = (k * jnp.exp(-cum)).astype(BF16)

    lane_k = lax.broadcasted_iota(jnp.int32, (1, GLA_WIDTH_K), 1)
    head_masks = [jnp.where((lane_k >> 6) == hh, 1.0, 0.0).astype(BF16) for hh in range(GLA_HEADS)]
    r4 = lax.broadcasted_iota(jnp.int32, (GLA_HEADS * GLA_CHUNK, GLA_CHUNK), 0)
    c4 = lax.broadcasted_iota(jnp.int32, (GLA_HEADS * GLA_CHUNK, GLA_CHUNK), 1)
    causal4 = c4 <= (r4 & (GLA_CHUNK - 1))
    sr = lax.broadcasted_iota(jnp.int32, (GLA_WIDTH_V, GLA_WIDTH_K), 0)
    sc = lax.broadcasted_iota(jnp.int32, (GLA_WIDTH_V, GLA_WIDTH_K), 1)
    state_mask = jnp.where((sr >> 7) == (sc >> 6), 1.0, 0.0)

    state = st_ref[...]
    outs = []
    for n in range(n_chunks):
        lo, hi = n * GLA_CHUNK, (n + 1) * GLA_CHUNK
        cum_c = cum[lo:hi]
        last = cum[hi - 1:hi]
        q_c = q_in[lo:hi]
        k_c = k_in[lo:hi]
        v_c = v[lo:hi]
        k_out = (k[lo:hi] * jnp.exp(last - cum_c)).astype(BF16)
        q4 = jnp.concatenate([q_c * head_masks[hh] for hh in range(GLA_HEADS)], axis=0)
        att4 = jnp.where(causal4, _dot_nt(q4, k_c), 0.0).astype(BF16)
        oi4 = _dot(att4, v_c)
        o_intra = jnp.concatenate(
            [oi4[hh * GLA_CHUNK:(hh + 1) * GLA_CHUNK, hh * GLA_DV:(hh + 1) * GLA_DV]
             for hh in range(GLA_HEADS)], axis=1)
        o_inter = _dot_nt(q_c, state.astype(BF16))
        outs.append(o_intra + o_inter)
        state = state * jnp.exp(last) + _dot_tn(v_c, k_out) * state_mask
    st_ref[...] = state

    o = jnp.concatenate(outs, axis=0)
    outg = outg_ref[...]
    normed = []
    for hh in range(GLA_HEADS):
        sl = slice(hh * GLA_DV, (hh + 1) * GLA_DV)
        normed.append(_rmsnorm(o[:, sl], outg[:, sl]))
    o = jnp.concatenate(normed, axis=1)
    return o * (r * _sigmoid(r))


def _sgu_branch(h, wu_ref, wsv_ref, ng_ref, nb_ref, sw_ref, sb_ref):
    ts = h.shape[0]
    n_chunks = ts // SGU_CHUNK
    u = _gelu(_dot(h, wu_ref[...]))
    v = _gelu(_dot(h, wsv_ref[...]))
    mu = jnp.mean(v, axis=-1, keepdims=True)
    vc = v - mu
    var = jnp.mean(vc * vc, axis=-1, keepdims=True)
    v = (vc * lax.rsqrt(var + EPS) * ng_ref[...] + nb_ref[...]).astype(BF16)

    row = lax.broadcasted_iota(jnp.int32, (SGU_CHUNK, SGU_CHUNK), 0)
    col = lax.broadcasted_iota(jnp.int32, (SGU_CHUNK, SGU_CHUNK), 1)
    causal = col <= row
    sb = sb_ref[...]
    per_group = []
    for g in range(SGU_GROUPS):
        w = jnp.where(causal, sw_ref[g], 0.0).astype(BF16)
        gs = slice(g * SGU_GROUP_DIM, (g + 1) * SGU_GROUP_DIM)
        vcat = jnp.concatenate(
            [v[c * SGU_CHUNK:(c + 1) * SGU_CHUNK, gs] for c in range(n_chunks)], axis=1)
        per_group.append(_dot(w, vcat) + sb[:, g:g + 1])
    rows = []
    for c in range(n_chunks):
        cs = slice(c * SGU_GROUP_DIM, (c + 1) * SGU_GROUP_DIM)
        rows.append(jnp.concatenate([per_group[g][:, cs] for g in range(SGU_GROUPS)], axis=1))
    mixed = jnp.concatenate(rows, axis=0)
    return u * mixed


def _mixer_body(x_ref, g_ref, wqkvr_ref, wa_ref, wa2_ref, ba_ref, outg_ref,
                wu_ref, wsv_ref, ng_ref, nb_ref, sw_ref, sb_ref,
                wg_ref, wpa_ref, wpb_ref, wmix_ref, o_ref, st_ref):
    @pl.when(pl.program_id(1) == 0)
    def _():
        st_ref[...] = jnp.zeros_like(st_ref)

    x = x_ref[0]
    h = _rmsnorm(x, g_ref[...]).astype(BF16)
    ya_in = _gla_branch(h, wqkvr_ref, wa_ref, wa2_ref, ba_ref, outg_ref, st_ref)
    y_a = _dot(ya_in.astype(BF16), wpa_ref[...])
    yb_in = _sgu_branch(h, wu_ref, wsv_ref, ng_ref, nb_ref, sw_ref, sb_ref)
    y_b = _dot(yb_in.astype(BF16), wpb_ref[...])
    gates = _sigmoid(_dot(h, wg_ref[...]))
    mix = gates[:, :D_MODEL] * y_a + gates[:, D_MODEL:] * y_b
    o_ref[0] = x + _dot(mix.astype(BF16), wmix_ref[...])


def _mixer(x, g, wqkvr, wa, wa2, ba, outg, wu, wsv, ng, nb, sw, sb, wg, wpa, wpb, wmix):
    B, S, D = x.shape
    consts = (g, wqkvr, wa, wa2, ba, outg, wu, wsv, ng, nb, sw, sb, wg, wpa, wpb, wmix)
    return pl.pallas_call(
        _mixer_body,
        grid=(B, S // SEQ_TILE),
        in_specs=[pl.BlockSpec((1, SEQ_TILE, D), lambda b, j: (b, j, 0))]
                 + [_const_spec(c.shape) for c in consts],
        out_specs=pl.BlockSpec((1, SEQ_TILE, D), lambda b, j: (b, j, 0)),
        out_shape=jax.ShapeDtypeStruct((B, S, D), F32),
        scratch_shapes=[pltpu.VMEM((GLA_WIDTH_V, GLA_WIDTH_K), F32)],
        compiler_params=pltpu.CompilerParams(
            dimension_semantics=("arbitrary", "arbitrary"), vmem_limit_bytes=VMEM_LIMIT),
        name="mixer",
    )(x, *consts)


def _xattn_body(x_ref, k_ref, v_ref, gx_ref, wq_ref, wo_ref, gf_ref, wr_hi_ref, wr_lo_ref, br_ref,
                x2_ref, h3_ref, topi_ref, gate_ref, rank_ref, cnt_ref, carry_ref):
    first = (pl.program_id(0) == 0) & (pl.program_id(1) == 0)

    @pl.when(first)
    def _():
        carry_ref[...] = jnp.zeros_like(carry_ref)

    x = x_ref[0]
    ts = x.shape[0]
    h = _rmsnorm(x, gx_ref[...]).astype(BF16)
    q = _dot(h, wq_ref[...]).astype(BF16)
    km = k_ref[0]
    vm = v_ref[0]
    heads = []
    for hh in range(XATTN_HEADS):
        sl = slice(hh * XATTN_DH, (hh + 1) * XATTN_DH)
        s = _dot_nt(q[:, sl], km[:, sl]) * (XATTN_DH ** -0.5)
        s = s - jnp.max(s, axis=-1, keepdims=True)
        e = jnp.exp(s)
        p = e / jnp.sum(e, axis=-1, keepdims=True)
        heads.append(_dot(p.astype(BF16), vm[:, sl]).astype(BF16))
    o = jnp.concatenate(heads, axis=1)
    x2 = x + _dot(o, wo_ref[...])
    x2_ref[0] = x2

    h3 = _rmsnorm(x2, gf_ref[...])
    h3_ref[0] = h3

    h_hi, h_lo = _split_bf16(h3)
    logits = (_dot_nt(wr_hi_ref[...], h_hi) + _dot_nt(wr_hi_ref[...], h_lo)
              + _dot_nt(wr_lo_ref[...], h_hi)) + br_ref[...]

    e_iota = lax.broadcasted_iota(jnp.int32, (N_EXPERTS, ts), 0)
    work = logits
    vals, idxs, hots = [], [], []
    for _ in range(TOP_K):
        m = jnp.max(work, axis=0, keepdims=True)
        idx = jnp.min(jnp.where(work == m, e_iota, N_EXPERTS), axis=0, keepdims=True)
        hot = e_iota == idx
        vals.append(m)
        idxs.append(idx)
        hots.append(hot)
        work = jnp.where(hot, -jnp.inf, work)
    exps = [jnp.exp(vv - vals[0]) for vv in vals]
    denom = exps[0] + exps[1] + exps[2] + exps[3]
    gate_ref[...] = jnp.concatenate([ee / denom for ee in exps], axis=0)
    topi_ref[...] = jnp.concatenate(idxs, axis=0)

    multi = jnp.where(hots[0] | hots[1] | hots[2] | hots[3], 1.0, 0.0)
    srow = lax.broadcasted_iota(jnp.int32, (ts, ts), 0)
    scol = lax.broadcasted_iota(jnp.int32, (ts, ts), 1)
    strict = jnp.where(srow < scol, 1.0, 0.0).astype(BF16)
    before = _dot(multi.astype(BF16), strict) + carry_ref[...]
    ranks = [jnp.sum(jnp.where(hot, before, 0.0), axis=0, keepdims=True) for hot in hots]
    rank_ref[...] = jnp.concatenate(ranks, axis=0).astype(jnp.int32)
    carry = carry_ref[...] + jnp.sum(multi, axis=1, keepdims=True)
    carry_ref[...] = carry
    cnt_ref[...] = carry.astype(jnp.int32)


def _xattn(x1, kmem, vmem, gx, wq, wo, gf, wr_hi, wr_lo, br):
    B, S, D = x1.shape
    T = B * S
    nj = S // SEQ_TILE
    tok_spec = pl.BlockSpec((1, SEQ_TILE, D), lambda b, j: (b, j, 0))
    mem_spec = pl.BlockSpec((1, MEM_LEN, D), lambda b, j: (b, 0, 0))
    lane_spec = pl.BlockSpec((TOP_K, SEQ_TILE), lambda b, j: (0, b * nj + j))
    consts = (gx, wq, wo, gf, wr_hi, wr_lo, br)
    return pl.pallas_call(
        _xattn_body,
        grid=(B, nj),
        in_specs=[tok_spec, mem_spec, mem_spec] + [_const_spec(c.shape) for c in consts],
        out_specs=[tok_spec, tok_spec, lane_spec, lane_spec, lane_spec,
                   pl.BlockSpec((N_EXPERTS, 1), lambda b, j: (0, 0))],
        out_shape=[
            jax.ShapeDtypeStruct((B, S, D), F32),
            jax.ShapeDtypeStruct((B, S, D), F32),
            jax.ShapeDtypeStruct((TOP_K, T), jnp.int32),
            jax.ShapeDtypeStruct((TOP_K, T), F32),
            jax.ShapeDtypeStruct((TOP_K, T), jnp.int32),
            jax.ShapeDtypeStruct((N_EXPERTS, 1), jnp.int32),
        ],
        scratch_shapes=[pltpu.VMEM((N_EXPERTS, 1), F32)],
        compiler_params=pltpu.CompilerParams(
            dimension_semantics=("arbitrary", "arbitrary"), vmem_limit_bytes=VMEM_LIMIT),
        name="xattn_router",
    )(x1, kmem, vmem, *consts)


def _row_copy(src_ref, src_row, dst_ref, dst_row, sem):
    return pltpu.make_async_copy(src_ref.at[pl.ds(src_row, 1)], dst_ref.at[pl.ds(dst_row, 1)], sem)


def _dispatch_body(zs_ref, nu_ref, dest_ref, h_ref, xb_ref, zbuf_ref, zsem, sem):
    @pl.when(pl.program_id(0) == 0)
    def _():
        zbuf_ref[...] = jnp.zeros_like(zbuf_ref)
        n_blocks = xb_ref.shape[0] // ROW_BLOCK

        def zero_copy(start):
            start = pl.multiple_of(start, ROW_BLOCK)
            return pltpu.make_async_copy(zbuf_ref, xb_ref.at[pl.ds(start, ROW_BLOCK)], zsem)

        for e in range(N_EXPERTS):
            zero_copy(zs_ref[e]).start()
        for e in range(N_EXPERTS):
            zero_copy(zs_ref[e]).wait()

        def tail(blk, c):
            cp = zero_copy(blk * ROW_BLOCK)
            cp.start()
            cp.wait()
            return c

        lax.fori_loop(nu_ref[0], n_blocks, tail, 0)

    def start(r, c):
        for kk in range(TOP_K):
            _row_copy(h_ref, r, xb_ref, dest_ref[0, 0, kk * DISPATCH_TILE + r], sem).start(
                priority=kk % 2)
        return c

    lax.fori_loop(0, DISPATCH_TILE, start, 0, unroll=8)
    for kk in range(TOP_K):
        pltpu.make_async_copy(h_ref, xb_ref.at[pl.ds(0, DISPATCH_TILE)], sem).wait()


def _dispatch(zero_start, n_used, dest_blocks, h3, n_slots):
    T, D = h3.shape
    return pl.pallas_call(
        _dispatch_body,
        grid_spec=pltpu.PrefetchScalarGridSpec(
            num_scalar_prefetch=2,
            grid=(T // DISPATCH_TILE,),
            in_specs=[
                pl.BlockSpec((1, 1, TOP_K * DISPATCH_TILE), lambda i, zs, nu: (i, 0, 0),
                             memory_space=pltpu.SMEM),
                pl.BlockSpec((DISPATCH_TILE, D), lambda i, zs, nu: (i, 0)),
            ],
            out_specs=pl.BlockSpec(memory_space=pl.ANY),
            scratch_shapes=[pltpu.VMEM((ROW_BLOCK, D), F32),
                            pltpu.SemaphoreType.DMA, pltpu.SemaphoreType.DMA],
        ),
        out_shape=jax.ShapeDtypeStruct((n_slots, D), F32),
        compiler_params=pltpu.CompilerParams(
            dimension_semantics=("arbitrary",), vmem_limit_bytes=VMEM_LIMIT),
        name="dispatch",
    )(zero_start, n_used, dest_blocks, h3)


def _expert_mlp(xb, w1b_ref, b1_ref, w2b_ref, b2_ref):
    hcat = _dot(xb.astype(BF16), w1b_ref[...]) + b1_ref[0]
    glu = jnp.minimum(hcat[:, :D_FF], SWIGLU_LIMIT)
    lin = jnp.clip(hcat[:, D_FF:], -SWIGLU_LIMIT, SWIGLU_LIMIT)
    act = (lin + 1.0) * glu * _sigmoid(SWIGLU_ALPHA * glu)
    return _dot(act.astype(BF16), w2b_ref[...]) + b2_ref[0]


def _experts_body(blk_ref, pair_ref, se_ref, ns_ref, nu_ref, x_ref, w1_ref, b1_ref, w2_ref, b2_ref,
                  y_hbm, w1b_ref, w2b_ref, ybuf_ref, zbuf_ref, sems, zsem):
    s = pl.program_id(0)
    n_steps = ns_ref[0]
    last = pl.num_programs(0) - 1
    total_blocks = y_hbm.shape[0] // ROW_BLOCK

    def tail_copy(blk):
        row0 = pl.multiple_of(blk * ROW_BLOCK, ROW_BLOCK)
        return pltpu.make_async_copy(zbuf_ref, y_hbm.at[pl.ds(row0, ROW_BLOCK)], zsem)

    @pl.when(s == 0)
    def _():
        zbuf_ref[...] = jnp.zeros_like(zbuf_ref)

        def go(blk, c):
            tail_copy(blk).start()
            return c

        lax.fori_loop(nu_ref[0], total_blocks, go, 0)

    @pl.when(s == last)
    def _():
        def done(blk, c):
            tail_copy(blk).wait()
            return c

        lax.fori_loop(nu_ref[0], total_blocks, done, 0)

    def out_copy(step, half):
        slot = step % 2
        row0 = pl.multiple_of((blk_ref[step] + half) * ROW_BLOCK, ROW_BLOCK)
        return pltpu.make_async_copy(
            ybuf_ref.at[slot, pl.ds(half * ROW_BLOCK, ROW_BLOCK)],
            y_hbm.at[pl.ds(row0, ROW_BLOCK)], sems.at[slot])

    def wait_step(step):
        out_copy(step, 0).wait()

        @pl.when(pair_ref[step] == 1)
        def _():
            out_copy(step, 1).wait()

    @pl.when(s < n_steps)
    def _():
        prev = se_ref[jnp.maximum(s - 1, 0)]

        @pl.when((s == 0) | (se_ref[s] != prev))
        def _():
            w1b_ref[...] = w1_ref[0].astype(BF16)
            w2b_ref[...] = w2_ref[0].astype(BF16)

        slot = s % 2

        @pl.when(pair_ref[s] == 1)
        def _():
            ybuf_ref[slot] = _expert_mlp(x_ref[...], w1b_ref, b1_ref, w2b_ref, b2_ref)

        @pl.when(pair_ref[s] == 0)
        def _():
            ybuf_ref[slot, :ROW_BLOCK] = _expert_mlp(x_ref[:ROW_BLOCK], w1b_ref, b1_ref, w2b_ref, b2_ref)

    @pl.when((s >= 1) & (s - 1 < n_steps))
    def _():
        wait_step(s - 1)

    @pl.when(s < n_steps)
    def _():
        out_copy(s, 0).start()

        @pl.when(pair_ref[s] == 1)
        def _():
            out_copy(s, 1).start()

        @pl.when(s == last)
        def _():
            wait_step(s)


def _experts(step_blk, step_pair, step_e, n_steps, n_used, xb, w1, b1, w2, b2):
    n_rows, D = xb.shape

    def x_map(s, blk, pair, se, ns, nu):
        return (blk[s] * ROW_BLOCK, 0)

    def w_map(s, blk, pair, se, ns, nu):
        return (se[s], 0, 0)

    return pl.pallas_call(
        _experts_body,
        grid_spec=pltpu.PrefetchScalarGridSpec(
            num_scalar_prefetch=5,
            grid=(step_blk.shape[0],),
            in_specs=[
                pl.BlockSpec((pl.Element(2 * ROW_BLOCK), pl.Element(D)), x_map),
                pl.BlockSpec((1, D, 2 * D_FF), w_map),
                pl.BlockSpec((1, 1, 2 * D_FF), w_map),
                pl.BlockSpec((1, D_FF, D), w_map),
                pl.BlockSpec((1, 1, D), w_map),
            ],
            out_specs=pl.BlockSpec(memory_space=pl.ANY),
            scratch_shapes=[pltpu.VMEM((D, 2 * D_FF), BF16), pltpu.VMEM((D_FF, D), BF16),
                            pltpu.VMEM((2, 2 * ROW_BLOCK, D), F32), pltpu.VMEM((ROW_BLOCK, D), F32),
                            pltpu.SemaphoreType.DMA((2,)), pltpu.SemaphoreType.DMA],
        ),
        out_shape=jax.ShapeDtypeStruct((n_rows, D), F32),
        compiler_params=pltpu.CompilerParams(
            dimension_semantics=("arbitrary",), vmem_limit_bytes=VMEM_LIMIT),
        name="experts",
    )(step_blk, step_pair, step_e, n_steps, n_used, xb, w1, b1, w2, b2)


def _combine_body(dest_ref, x_ref, gate_ref, g_ref, yb_ref, o_ref, rows_ref, sem):
    def start(r, c):
        for kk in range(TOP_K):
            _row_copy(yb_ref, dest_ref[0, 0, kk * TOK_TILE + r], rows_ref.at[kk], r, sem).start(priority=kk % 2)
        return c

    lax.fori_loop(0, TOK_TILE, start, 0, unroll=8)
    for kk in range(TOP_K):
        pltpu.make_async_copy(yb_ref.at[pl.ds(0, TOK_TILE)], rows_ref.at[kk], sem).wait()

    gates = gate_ref[...]
    y = gates[:, 0:1] * rows_ref[0]
    for kk in range(1, TOP_K):
        y = y + gates[:, kk:kk + 1] * rows_ref[kk]
    o_ref[...] = _rmsnorm(x_ref[...] + y, g_ref[...])


def _combine(dest_blocks, x2, gates_t, g, yb):
    T, D = x2.shape
    return pl.pallas_call(
        _combine_body,
        grid=(T // TOK_TILE,),
        in_specs=[
            pl.BlockSpec((1, 1, TOP_K * TOK_TILE), lambda i: (i, 0, 0), memory_space=pltpu.SMEM),
            pl.BlockSpec((TOK_TILE, D), lambda i: (i, 0)),
            pl.BlockSpec((TOK_TILE, TOP_K), lambda i: (i, 0)),
            _const_spec((1, D)),
            pl.BlockSpec(memory_space=pl.ANY),
        ],
        out_specs=pl.BlockSpec((TOK_TILE, D), lambda i: (i, 0)),
        out_shape=jax.ShapeDtypeStruct((T, D), F32),
        scratch_shapes=[pltpu.VMEM((TOP_K, TOK_TILE, D), F32), pltpu.SemaphoreType.DMA],
        compiler_params=pltpu.CompilerParams(
            dimension_semantics=("arbitrary",), vmem_limit_bytes=VMEM_LIMIT),
        name="combine",
    )(dest_blocks, x2, gates_t, g, yb)


def _row(v):
    return v.reshape(1, -1).astype(F32)


def _layer(x, kv, p):
    B, S, D = x.shape
    T = B * S
    w_in = p["w_in"]
    wa = jnp.zeros((D, GLA_RANK_PAD), F32).at[:, :GLA_GATE_RANK].set(w_in[:, OFF_A:OFF_U])
    wa2 = jnp.zeros((GLA_RANK_PAD, GLA_WIDTH_K), F32).at[:GLA_GATE_RANK].set(p["gla_w_a2"])
    x1 = _mixer(
        x, _row(p["norm_mix_g"]),
        w_in[:, OFF_Q:OFF_A].astype(BF16), wa.astype(BF16), wa2.astype(BF16),
        _row(p["gla_b_a"]), _row(p["gla_out_g"]),
        w_in[:, OFF_U:OFF_SV].astype(BF16), w_in[:, OFF_SV:OFF_G].astype(BF16),
        _row(p["sgu_norm_g"]), _row(p["sgu_norm_b"]), p["sgu_w"].astype(F32), p["sgu_b"].T.astype(F32),
        w_in[:, OFF_G:].astype(BF16), p["w_proj_a"].astype(BF16), p["w_proj_b"].astype(BF16),
        p["w_mix_out"].astype(BF16))

    kmem, vmem = kv
    wr_t = p["w_router"].T.astype(F32)
    wr_hi = wr_t.astype(BF16)
    wr_lo = (wr_t - wr_hi.astype(F32)).astype(BF16)
    x2, h3, topi, gates, rank, counts = _xattn(
        x1, kmem, vmem, _row(p["norm_x_g"]), p["w_xq"].astype(BF16), p["w_xo"].astype(BF16),
        _row(p["norm_ffn_g"]), wr_hi, wr_lo, p["b_router"].reshape(N_EXPERTS, 1).astype(F32))

    counts = counts.reshape(N_EXPERTS)
    blocks_e = (counts + ROW_BLOCK - 1) // ROW_BLOCK
    padded = blocks_e * ROW_BLOCK
    pad_end = jnp.cumsum(padded)
    pad_start = pad_end - padded
    n_blocks = (T * TOP_K) // ROW_BLOCK + N_EXPERTS
    n_slots = (n_blocks + 1) * ROW_BLOCK
    n_used = (pad_end[-1] // ROW_BLOCK).astype(jnp.int32)
    steps_e = (blocks_e + 1) // 2
    step_end = jnp.cumsum(steps_e)
    n_steps = step_end[-1].astype(jnp.int32)
    max_steps = (n_blocks + N_EXPERTS) // 2
    sidx = jnp.minimum(jnp.arange(max_steps, dtype=jnp.int32), n_steps - 1)
    step_e = jnp.minimum(jnp.sum(step_end[None, :] <= sidx[:, None], axis=1), N_EXPERTS - 1).astype(jnp.int32)
    of_step = step_e[:, None] == jnp.arange(N_EXPERTS, dtype=jnp.int32)[None, :]

    def per_step(v):
        return jnp.sum(jnp.where(of_step, v[None, :], 0), axis=1)

    local = sidx - per_step(step_end - steps_e)
    step_blk = (per_step(pad_start // ROW_BLOCK) + 2 * local).astype(jnp.int32)
    step_pair = (2 * local + 1 < per_step(blocks_e)).astype(jnp.int32)
    e_ids = jnp.arange(N_EXPERTS, dtype=jnp.int32)[:, None, None]
    dest = jnp.sum(jnp.where(topi[None] == e_ids, pad_start[:, None, None], 0), axis=0) + rank
    def blocked(tile):
        return (dest.reshape(TOP_K, T // tile, tile).transpose(1, 0, 2)
                .reshape(T // tile, 1, TOP_K * tile).astype(jnp.int32))

    dest_blocks = blocked(TOK_TILE)
    zero_start = jnp.maximum(pad_end - ROW_BLOCK, 0).astype(jnp.int32)

    n_used = n_used.reshape(1)
    xb = _dispatch(zero_start, n_used, blocked(DISPATCH_TILE), h3.reshape(T, D), n_slots)
    yb = _experts(step_blk, step_pair, step_e, n_steps.reshape(1), n_used, xb,
                  p["w_e1"], p["b_e1"].reshape(N_EXPERTS, 1, 2 * D_FF),
                  p["w_e2"], p["b_e2"].reshape(N_EXPERTS, 1, D))
    return dest_blocks, x2.reshape(T, D), gates.T, yb


def kernel(x, mem, norm_mix_g, w_in, gla_w_a2, gla_b_a, gla_out_g, sgu_norm_g, sgu_norm_b, sgu_w, sgu_b, w_proj_a, w_proj_b, w_mix_out, norm_x_g, norm_mem_g, w_xq, w_xk, w_xv, w_xo, norm_ffn_g, w_router, b_router, w_e1, b_e1, w_e2, b_e2, norm_final_g):
    B, S, D = x.shape
    depth = w_in.shape[0]
    assert depth == 1, "the final norm is fused into the last layer's combine step"
    stacked = dict(norm_mix_g=norm_mix_g, w_in=w_in, gla_w_a2=gla_w_a2, gla_b_a=gla_b_a,
                   gla_out_g=gla_out_g, sgu_norm_g=sgu_norm_g, sgu_norm_b=sgu_norm_b, sgu_w=sgu_w,
                   sgu_b=sgu_b, w_proj_a=w_proj_a, w_proj_b=w_proj_b, w_mix_out=w_mix_out,
                   norm_x_g=norm_x_g, w_xq=w_xq, w_xo=w_xo, norm_ffn_g=norm_ffn_g,
                   w_router=w_router, b_router=b_router, w_e1=w_e1, b_e1=b_e1, w_e2=w_e2, b_e2=b_e2)
    p = {name: v[0] for name, v in stacked.items()}
    kv = _xattn_kv(mem, _row(norm_mem_g[0]), w_xk[0].astype(BF16), w_xv[0].astype(BF16))
    dest_blocks, x2, gates_t, yb = _layer(x, kv, p)
    out = _combine(dest_blocks, x2, gates_t, _row(norm_final_g), yb)
    return out.reshape(B, S, D)
```

```python
import functools

import jax
import jax.numpy as jnp
from jax import lax
from jax.experimental import pallas as pl
from jax.experimental.pallas import tpu as pltpu

F32 = jnp.float32
BF16 = jnp.bfloat16

D_MODEL = 1024
MEM_LEN = 256
EPS = 1e-5

GLA_HEADS = 4
GLA_DV = 128
GLA_DK = 64
GLA_WIDTH_K = GLA_HEADS * GLA_DK
GLA_WIDTH_V = GLA_HEADS * GLA_DV
GLA_GATE_RANK = 16
GLA_RANK_PAD = 128
GLA_TAU = 16.0
GLA_CHUNK = 64

SGU_WIDTH = 512
SGU_GROUPS = 4
SGU_GROUP_DIM = SGU_WIDTH // SGU_GROUPS
SGU_CHUNK = 128

OFF_Q = 0
OFF_K = OFF_Q + GLA_WIDTH_K
OFF_V = OFF_K + GLA_WIDTH_K
OFF_R = OFF_V + GLA_WIDTH_V
OFF_A = OFF_R + GLA_WIDTH_V
OFF_U = OFF_A + GLA_GATE_RANK
OFF_SV = OFF_U + SGU_WIDTH
OFF_G = OFF_SV + SGU_WIDTH

XATTN_HEADS = 4
XATTN_DH = D_MODEL // XATTN_HEADS

N_EXPERTS = 32
TOP_K = 4
D_FF = D_MODEL
SWIGLU_ALPHA = 1.702
SWIGLU_LIMIT = 7.0

SEQ_TILE = 512
ROW_BLOCK = 256
TOK_TILE = 512
DISPATCH_TILE = 1024
VMEM_LIMIT = 56 * 1024 * 1024


def _rmsnorm(x, g):
    return x * lax.rsqrt(jnp.mean(x * x, axis=-1, keepdims=True) + EPS) * g


def _sigmoid(x):
    return 1.0 / (1.0 + jnp.exp(-x))


def _log_sigmoid(x):
    return jnp.minimum(x, 0.0) - jnp.log1p(jnp.exp(-jnp.abs(x)))


def _gelu(x):
    return 0.5 * x * (1.0 + lax.erf(x * (2.0 ** -0.5)))


def _dot(a, b):
    return jnp.dot(a, b, preferred_element_type=F32)


def _dot_nt(a, b):
    return lax.dot_general(a, b, (((1,), (1,)), ((), ())), preferred_element_type=F32)


def _dot_tn(a, b):
    return lax.dot_general(a, b, (((0,), (0,)), ((), ())), preferred_element_type=F32)


def _split_bf16(x):
    hi = x.astype(BF16)
    lo = (x - hi.astype(F32)).astype(BF16)
    return hi, lo


def _const_spec(shape):
    zeros = (0,) * len(shape)
    return pl.BlockSpec(shape, lambda *_: zeros, pipeline_mode=pl.Buffered(1))


def _xattn_kv_body(mem_ref, g_ref, wk_ref, wv_ref, k_ref, v_ref):
    m = _rmsnorm(mem_ref[0], g_ref[...]).astype(BF16)
    k_ref[0] = _dot(m, wk_ref[...]).astype(BF16)
    v_ref[0] = _dot(m, wv_ref[...]).astype(BF16)


def _xattn_kv(mem, g, wk, wv):
    B = mem.shape[0]
    return pl.pallas_call(
        _xattn_kv_body,
        grid=(B,),
        in_specs=[
            pl.BlockSpec((1, MEM_LEN, D_MODEL), lambda b: (b, 0, 0)),
            _const_spec((1, D_MODEL)),
            _const_spec((D_MODEL, D_MODEL)),
            _const_spec((D_MODEL, D_MODEL)),
        ],
        out_specs=[
            pl.BlockSpec((1, MEM_LEN, D_MODEL), lambda b: (b, 0, 0)),
            pl.BlockSpec((1, MEM_LEN, D_MODEL), lambda b: (b, 0, 0)),
        ],
        out_shape=[jax.ShapeDtypeStruct((B, MEM_LEN, D_MODEL), BF16)] * 2,
        compiler_params=pltpu.CompilerParams(
            dimension_semantics=("arbitrary",), vmem_limit_bytes=VMEM_LIMIT),
        name="xattn_kv",
    )(mem, g, wk, wv)


def _gla_branch(h, wqkvr_ref, wa_ref, wa2_ref, ba_ref, outg_ref, st_ref):
    ts = h.shape[0]
    n_chunks = ts // GLA_CHUNK
    p = _dot(h, wqkvr_ref[...])
    q = p[:, OFF_Q:OFF_K] * (GLA_DK ** -0.5)
    k = p[:, OFF_K:OFF_V]
    v = p[:, OFF_V:OFF_R].astype(BF16)
    r = p[:, OFF_R:OFF_A]

    a_low = _dot(h, wa_ref[...]).astype(BF16)
    log_a = _log_sigmoid(_dot(a_low, wa2_ref[...]) + ba_ref[...]) * (1.0 / GLA_TAU)

    row = lax.broadcasted_iota(jnp.int32, (ts, ts), 0)
    col = lax.broadcasted_iota(jnp.int32, (ts, ts), 1)
    tri = jnp.where((col <= row) & ((col >> 6) == (row >> 6)), 1.0, 0.0).astype(BF16)
    la_hi, la_lo = _split_bf16(log_a)
    cum = _dot(tri, la_hi) + _dot(tri, la_lo)

    q_in = (q * jnp.exp(cum)).astype(BF16)
    k_in = (k * jnp.exp(-cum)).astype(BF16)

    lane_k = lax.broadcasted_iota(jnp.int32, (1, GLA_WIDTH_K), 1)
    head_masks = [jnp.where((lane_k >> 6) == hh, 1.0, 0.0).astype(BF16) for hh in range(GLA_HEADS)]
    r4 = lax.broadcasted_iota(jnp.int32, (GLA_HEADS * GLA_CHUNK, GLA_CHUNK), 0)
    c4 = lax.broadcasted_iota(jnp.int32, (GLA_HEADS * GLA_CHUNK, GLA_CHUNK), 1)
    causal4 = c4 <= (r4 & (GLA_CHUNK - 1))
    sr = lax.broadcasted_iota(jnp.int32, (GLA_WIDTH_V, GLA_WIDTH_K), 0)
    sc = lax.broadcasted_iota(jnp.int32, (GLA_WIDTH_V, GLA_WIDTH_K), 1)
    state_mask = jnp.where((sr >> 7) == (sc >> 6), 1.0, 0.0)

    state = st_ref[...]
    outs = []
    for n in range(n_chunks):
        lo, hi = n * GLA_CHUNK, (n + 1) * GLA_CHUNK
        cum_c = cum[lo:hi]
        last = cum[hi - 1:hi]
        q_c = q_in[lo:hi]
        k_c = k_in[lo:hi]
        v_c = v[lo:hi]
        k_out = (k[lo:hi] * jnp.exp(last - cum_c)).astype(BF16)
        q4 = jnp.concatenate([q_c * head_masks[hh] for hh in range(GLA_HEADS)], axis=0)
        att4 = jnp.where(causal4, _dot_nt(q4, k_c), 0.0).astype(BF16)
        oi4 = _dot(att4, v_c)
        o_intra = jnp.concatenate(
            [oi4[hh * GLA_CHUNK:(hh + 1) * GLA_CHUNK, hh * GLA_DV:(hh + 1) * GLA_DV]
             for hh in range(GLA_HEADS)], axis=1)
        o_inter = _dot_nt(q_c, state.astype(BF16))
        outs.append(o_intra + o_inter)
        state = state * jnp.exp(last) + _dot_tn(v_c, k_out) * state_mask
    st_ref[...] = state

    o = jnp.concatenate(outs, axis=0)
    outg = outg_ref[...]
    normed = []
    for hh in range(GLA_HEADS):
        sl = slice(hh * GLA_DV, (hh + 1) * GLA_DV)
        normed.append(_rmsnorm(o[:, sl], outg[:, sl]))
    o = jnp.concatenate(normed, axis=1)
    return o * (r * _sigmoid(r))


def _sgu_branch(h, wu_ref, wsv_ref, ng_ref, nb_ref, sw_ref, sb_ref):
    ts = h.shape[0]
    n_chunks = ts // SGU_CHUNK
    u = _gelu(_dot(h, wu_ref[...]))
    v = _gelu(_dot(h, wsv_ref[...]))
    mu = jnp.mean(v, axis=-1, keepdims=True)
    vc = v - mu
    var = jnp.mean(vc * vc, axis=-1, keepdims=True)
    v = (vc * lax.rsqrt(var + EPS) * ng_ref[...] + nb_ref[...]).astype(BF16)

    row = lax.broadcasted_iota(jnp.int32, (SGU_CHUNK, SGU_CHUNK), 0)
    col = lax.broadcasted_iota(jnp.int32, (SGU_CHUNK, SGU_CHUNK), 1)
    causal = col <= row
    sb = sb_ref[...]
    per_group = []
    for g in range(SGU_GROUPS):
        w = jnp.where(causal, sw_ref[g], 0.0).astype(BF16)
        gs = slice(g * SGU_GROUP_DIM, (g + 1) * SGU_GROUP_DIM)
        vcat = jnp.concatenate(
            [v[c * SGU_CHUNK:(c + 1) * SGU_CHUNK, gs] for c in range(n_chunks)], axis=1)
        per_group.append(_dot(w, vcat) + sb[:, g:g + 1])
    rows = []
    for c in range(n_chunks):
        cs = slice(c * SGU_GROUP_DIM, (c + 1) * SGU_GROUP_DIM)
        rows.append(jnp.concatenate([per_group[g][:, cs] for g in range(SGU_GROUPS)], axis=1))
    mixed = jnp.concatenate(rows, axis=0)
    return u * mixed


def _mixer_body(x_ref, g_ref, wqkvr_ref, wa_ref, wa2_ref, ba_ref, outg_ref,
                wu_ref, wsv_ref, ng_ref, nb_ref, sw_ref, sb_ref,
                wg_ref, wpa_ref, wpb_ref, wmix_ref, o_ref, st_ref):
    @pl.when(pl.program_id(1) == 0)
    def _():
        st_ref[...] = jnp.zeros_like(st_ref)

    x = x_ref[0]
    h = _rmsnorm(x, g_ref[...]).astype(BF16)
    ya_in = _gla_branch(h, wqkvr_ref, wa_ref, wa2_ref, ba_ref, outg_ref, st_ref)
    y_a = _dot(ya_in.astype(BF16), wpa_ref[...])
    yb_in = _sgu_branch(h, wu_ref, wsv_ref, ng_ref, nb_ref, sw_ref, sb_ref)
    y_b = _dot(yb_in.astype(BF16), wpb_ref[...])
    gates = _sigmoid(_dot(h, wg_ref[...]))
    mix = gates[:, :D_MODEL] * y_a + gates[:, D_MODEL:] * y_b
    o_ref[0] = x + _dot(mix.astype(BF16), wmix_ref[...])


def _mixer(x, g, wqkvr, wa, wa2, ba, outg, wu, wsv, ng, nb, sw, sb, wg, wpa, wpb, wmix):
    B, S, D = x.shape
    consts = (g, wqkvr, wa, wa2, ba, outg, wu, wsv, ng, nb, sw, sb, wg, wpa, wpb, wmix)
    return pl.pallas_call(
        _mixer_body,
        grid=(B, S // SEQ_TILE),
        in_specs=[pl.BlockSpec((1, SEQ_TILE, D), lambda b, j: (b, j, 0))]
                 + [_const_spec(c.shape) for c in consts],
        out_specs=pl.BlockSpec((1, SEQ_TILE, D), lambda b, j: (b, j, 0)),
        out_shape=jax.ShapeDtypeStruct((B, S, D), F32),
        scratch_shapes=[pltpu.VMEM((GLA_WIDTH_V, GLA_WIDTH_K), F32)],
        compiler_params=pltpu.CompilerParams(
            dimension_semantics=("arbitrary", "arbitrary"), vmem_limit_bytes=VMEM_LIMIT),
        name="mixer",
    )(x, *consts)


def _xattn_body(x_ref, k_ref, v_ref, gx_ref, wq_ref, wo_ref, gf_ref, wr_hi_ref, wr_lo_ref, br_ref,
                x2_ref, h3_ref, topi_ref, gate_ref, rank_ref, cnt_ref, carry_ref):
    first = (pl.program_id(0) == 0) & (pl.program_id(1) == 0)

    @pl.when(first)
    def _():
        carry_ref[...] = jnp.zeros_like(carry_ref)

    x = x_ref[0]
    ts = x.shape[0]
    h = _rmsnorm(x, gx_ref[...]).astype(BF16)
    q = _dot(h, wq_ref[...]).astype(BF16)
    km = k_ref[0]
    vm = v_ref[0]
    heads = []
    for hh in range(XATTN_HEADS):
        sl = slice(hh * XATTN_DH, (hh + 1) * XATTN_DH)
        s = _dot_nt(q[:, sl], km[:, sl]) * (XATTN_DH ** -0.5)
        s = s - jnp.max(s, axis=-1, keepdims=True)
        e = jnp.exp(s)
        p = e / jnp.sum(e, axis=-1, keepdims=True)
        heads.append(_dot(p.astype(BF16), vm[:, sl]).astype(BF16))
    o = jnp.concatenate(heads, axis=1)
    x2 = x + _dot(o, wo_ref[...])
    x2_ref[0] = x2

    h3 = _rmsnorm(x2, gf_ref[...])
    h3_ref[0] = h3

    h_hi, h_lo = _split_bf16(h3)
    logits = (_dot_nt(wr_hi_ref[...], h_hi) + _dot_nt(wr_hi_ref[...], h_lo)
              + _dot_nt(wr_lo_ref[...], h_hi)) + br_ref[...]

    e_iota = lax.broadcasted_iota(jnp.int32, (N_EXPERTS, ts), 0)
    work = logits
    vals, idxs, hots = [], [], []
    for _ in range(TOP_K):
        m = jnp.max(work, axis=0, keepdims=True)
        idx = jnp.min(jnp.where(work == m, e_iota, N_EXPERTS), axis=0, keepdims=True)
        hot = e_iota == idx
        vals.append(m)
        idxs.append(idx)
        hots.append(hot)
        work = jnp.where(hot, -jnp.inf, work)
    exps = [jnp.exp(vv - vals[0]) for vv in vals]
    denom = exps[0] + exps[1] + exps[2] + exps[3]
    gate_ref[...] = jnp.concatenate([ee / denom for ee in exps], axis=0)
    topi_ref[...] = jnp.concatenate(idxs, axis=0)

    multi = jnp.where(hots[0] | hots[1] | hots[2] | hots[3], 1.0, 0.0)
    srow = lax.broadcasted_iota(jnp.int32, (ts, ts), 0)
    scol = lax.broadcasted_iota(jnp.int32, (ts, ts), 1)
    strict = jnp.where(srow < scol, 1.0, 0.0).astype(BF16)
    before = _dot(multi.astype(BF16), strict) + carry_ref[...]
    ranks = [jnp.sum(jnp.where(hot, before, 0.0), axis=0, keepdims=True) for hot in hots]
    rank_ref[...] = jnp.concatenate(ranks, axis=0).astype(jnp.int32)
    carry = carry_ref[...] + jnp.sum(multi, axis=1, keepdims=True)
    carry_ref[...] = carry
    cnt_ref[...] = carry.astype(jnp.int32)


def _xattn(x1, kmem, vmem, gx, wq, wo, gf, wr_hi, wr_lo, br):
    B, S, D = x1.shape
    T = B * S
    nj = S // SEQ_TILE
    tok_spec = pl.BlockSpec((1, SEQ_TILE, D), lambda b, j: (b, j, 0))
    mem_spec = pl.BlockSpec((1, MEM_LEN, D), lambda b, j: (b, 0, 0))
    lane_spec = pl.BlockSpec((TOP_K, SEQ_TILE), lambda b, j: (0, b * nj + j))
    consts = (gx, wq, wo, gf, wr_hi, wr_lo, br)
    return pl.pallas_call(
        _xattn_body,
        grid=(B, nj),
        in_specs=[tok_spec, mem_spec, mem_spec] + [_const_spec(c.shape) for c in consts],
        out_specs=[tok_spec, tok_spec, lane_spec, lane_spec, lane_spec,
                   pl.BlockSpec((N_EXPERTS, 1), lambda b, j: (0, 0))],
        out_shape=[
            jax.ShapeDtypeStruct((B, S, D), F32),
            jax.ShapeDtypeStruct((B, S, D), F32),
            jax.ShapeDtypeStruct((TOP_K, T), jnp.int32),
            jax.ShapeDtypeStruct((TOP_K, T), F32),
            jax.ShapeDtypeStruct((TOP_K, T), jnp.int32),
            jax.ShapeDtypeStruct((N_EXPERTS, 1), jnp.int32),
        ],
        scratch_shapes=[pltpu.VMEM((N_EXPERTS, 1), F32)],
        compiler_params=pltpu.CompilerParams(
            dimension_semantics=("arbitrary", "arbitrary"), vmem_limit_bytes=VMEM_LIMIT),
        name="xattn_router",
    )(x1, kmem, vmem, *consts)


def _row_copy(src_ref, src_row, dst_ref, dst_row, sem):
    return pltpu.make_async_copy(src_ref.at[pl.ds(src_row, 1)], dst_ref.at[pl.ds(dst_row, 1)], sem)


def _dispatch_body(zs_ref, nu_ref, dest_ref, h_ref, xb_ref, zbuf_ref, zsem, sem):
    @pl.when(pl.program_id(0) == 0)
    def _():
        zbuf_ref[...] = jnp.zeros_like(zbuf_ref)
        n_blocks = xb_ref.shape[0] // ROW_BLOCK

        def zero_copy(start):
            start = pl.multiple_of(start, ROW_BLOCK)
            return pltpu.make_async_copy(zbuf_ref, xb_ref.at[pl.ds(start, ROW_BLOCK)], zsem)

        for e in range(N_EXPERTS):
            zero_copy(zs_ref[e]).start()
        for e in range(N_EXPERTS):
            zero_copy(zs_ref[e]).wait()

        def tail(blk, c):
            cp = zero_copy(blk * ROW_BLOCK)
            cp.start()
            cp.wait()
            return c

        lax.fori_loop(nu_ref[0], n_blocks, tail, 0)

    def start(r, c):
        for kk in range(TOP_K):
            _row_copy(h_ref, r, xb_ref, dest_ref[0, 0, kk * DISPATCH_TILE + r], sem).start(
                priority=kk % 2)
        return c

    lax.fori_loop(0, DISPATCH_TILE, start, 0, unroll=8)
    for kk in range(TOP_K):
        pltpu.make_async_copy(h_ref, xb_ref.at[pl.ds(0, DISPATCH_TILE)], sem).wait()


def _dispatch(zero_start, n_used, dest_blocks, h3, n_slots):
    T, D = h3.shape
    return pl.pallas_call(
        _dispatch_body,
        grid_spec=pltpu.PrefetchScalarGridSpec(
            num_scalar_prefetch=2,
            grid=(T // DISPATCH_TILE,),
            in_specs=[
                pl.BlockSpec((1, 1, TOP_K * DISPATCH_TILE), lambda i, zs, nu: (i, 0, 0),
                             memory_space=pltpu.SMEM),
                pl.BlockSpec((DISPATCH_TILE, D), lambda i, zs, nu: (i, 0)),
            ],
            out_specs=pl.BlockSpec(memory_space=pl.ANY),
            scratch_shapes=[pltpu.VMEM((ROW_BLOCK, D), F32),
                            pltpu.SemaphoreType.DMA, pltpu.SemaphoreType.DMA],
        ),
        out_shape=jax.ShapeDtypeStruct((n_slots, D), F32),
        compiler_params=pltpu.CompilerParams(
            dimension_semantics=("arbitrary",), vmem_limit_bytes=VMEM_LIMIT),
        name="dispatch",
    )(zero_start, n_used, dest_blocks, h3)


def _expert_mlp(xb, w1b_ref, b1_ref, w2b_ref, b2_ref):
    hcat = _dot(xb.astype(BF16), w1b_ref[...]) + b1_ref[0]
    glu = jnp.minimum(hcat[:, :D_FF], SWIGLU_LIMIT)
    lin = jnp.clip(hcat[:, D_FF:], -SWIGLU_LIMIT, SWIGLU_LIMIT)
    act = (lin + 1.0) * glu * _sigmoid(SWIGLU_ALPHA * glu)
    return _dot(act.astype(BF16), w2b_ref[...]) + b2_ref[0]


def _experts_body(blk_ref, pair_ref, se_ref, run_ref, nxt_ref, ns_ref, nu_ref,
                  x_ref, w1_hbm, b1_ref, w2_hbm, b2_ref,
                  y_hbm, w1b_ref, w2b_ref, wf1_ref, wf2_ref, ybuf_ref, zbuf_ref, sems, wsems, zsem):
    s = pl.program_id(0)
    n_steps = ns_ref[0]
    last = pl.num_programs(0) - 1
    total_blocks = y_hbm.shape[0] // ROW_BLOCK

    def tail_copy(blk):
        row0 = pl.multiple_of(blk * ROW_BLOCK, ROW_BLOCK)
        return pltpu.make_async_copy(zbuf_ref, y_hbm.at[pl.ds(row0, ROW_BLOCK)], zsem)

    @pl.when(s == 0)
    def _():
        zbuf_ref[...] = jnp.zeros_like(zbuf_ref)

        def go(blk, c):
            tail_copy(blk).start()
            return c

        lax.fori_loop(nu_ref[0], total_blocks, go, 0)

    @pl.when(s == last)
    def _():
        def done(blk, c):
            tail_copy(blk).wait()
            return c

        lax.fori_loop(nu_ref[0], total_blocks, done, 0)

    def out_copy(step, half):
        slot = step % 2
        row0 = pl.multiple_of((blk_ref[step] + half) * ROW_BLOCK, ROW_BLOCK)
        return pltpu.make_async_copy(
            ybuf_ref.at[slot, pl.ds(half * ROW_BLOCK, ROW_BLOCK)],
            y_hbm.at[pl.ds(row0, ROW_BLOCK)], sems.at[slot])

    def wait_step(step):
        out_copy(step, 0).wait()

        @pl.when(pair_ref[step] == 1)
        def _():
            out_copy(step, 1).wait()

    def weight_copies(e, wslot):
        return (pltpu.make_async_copy(w1_hbm.at[e], wf1_ref.at[wslot], wsems.at[wslot]),
                pltpu.make_async_copy(w2_hbm.at[e], wf2_ref.at[wslot], wsems.at[wslot]))

    @pl.when(s < n_steps)
    def _():
        prev = se_ref[jnp.maximum(s - 1, 0)]
        wslot = run_ref[s] % 2

        @pl.when(s == 0)
        def _():
            for cp in weight_copies(se_ref[0], 0):
                cp.start()

        @pl.when((s == 0) | (se_ref[s] != prev))
        def _():
            for cp in weight_copies(se_ref[s], wslot):
                cp.wait()
            w1b_ref[...] = wf1_ref[wslot].astype(BF16)
            w2b_ref[...] = wf2_ref[wslot].astype(BF16)

            @pl.when(nxt_ref[s] < N_EXPERTS)
            def _():
                for cp in weight_copies(nxt_ref[s], 1 - wslot):
                    cp.start()

        slot = s % 2

        @pl.when(pair_ref[s] == 1)
        def _():
            ybuf_ref[slot] = _expert_mlp(x_ref[...], w1b_ref, b1_ref, w2b_ref, b2_ref)

        @pl.when(pair_ref[s] == 0)
        def _():
            ybuf_ref[slot, :ROW_BLOCK] = _expert_mlp(x_ref[:ROW_BLOCK], w1b_ref, b1_ref, w2b_ref, b2_ref)

    @pl.when((s >= 1) & (s - 1 < n_steps))
    def _():
        wait_step(s - 1)

    @pl.when(s < n_steps)
    def _():
        out_copy(s, 0).start()

        @pl.when(pair_ref[s] == 1)
        def _():
            out_copy(s, 1).start()

        @pl.when(s == last)
        def _():
            wait_step(s)


def _experts(step_blk, step_pair, step_e, step_run, step_next, n_steps, n_used, xb, w1, b1, w2, b2):
    n_rows, D = xb.shape

    def x_map(s, blk, pair, se, run, nxt, ns, nu):
        return (blk[s] * ROW_BLOCK, 0)

    def b_map(s, blk, pair, se, run, nxt, ns, nu):
        return (se[s], 0, 0)

    return pl.pallas_call(
        _experts_body,
        grid_spec=pltpu.PrefetchScalarGridSpec(
            num_scalar_prefetch=7,
            grid=(step_blk.shape[0],),
            in_specs=[
                pl.BlockSpec((pl.Element(2 * ROW_BLOCK), pl.Element(D)), x_map),
                pl.BlockSpec(memory_space=pl.ANY),
                pl.BlockSpec((1, 1, 2 * D_FF), b_map),
                pl.BlockSpec(memory_space=pl.ANY),
                pl.BlockSpec((1, 1, D), b_map),
            ],
            out_specs=pl.BlockSpec(memory_space=pl.ANY),
            scratch_shapes=[pltpu.VMEM((D, 2 * D_FF), BF16), pltpu.VMEM((D_FF, D), BF16),
                            pltpu.VMEM((2, D, 2 * D_FF), F32), pltpu.VMEM((2, D_FF, D), F32),
                            pltpu.VMEM((2, 2 * ROW_BLOCK, D), F32), pltpu.VMEM((ROW_BLOCK, D), F32),
                            pltpu.SemaphoreType.DMA((2,)), pltpu.SemaphoreType.DMA((2,)),
                            pltpu.SemaphoreType.DMA],
        ),
        out_shape=jax.ShapeDtypeStruct((n_rows, D), F32),
        compiler_params=pltpu.CompilerParams(
            dimension_semantics=("arbitrary",), vmem_limit_bytes=VMEM_LIMIT),
        name="experts",
    )(step_blk, step_pair, step_e, step_run, step_next, n_steps, n_used, xb, w1, b1, w2, b2)


def _combine_body(dest_ref, x_ref, gate_ref, g_ref, yb_ref, o_ref, rows_ref, sem):
    def start(r, c):
        for kk in range(TOP_K):
            _row_copy(yb_ref, dest_ref[0, 0, kk * TOK_TILE + r], rows_ref.at[kk], r, sem).start(priority=kk % 2)
        return c

    lax.fori_loop(0, TOK_TILE, start, 0, unroll=8)
    for kk in range(TOP_K):
        pltpu.make_async_copy(yb_ref.at[pl.ds(0, TOK_TILE)], rows_ref.at[kk], sem).wait()

    gates = gate_ref[...]
    y = gates[:, 0:1] * rows_ref[0]
    for kk in range(1, TOP_K):
        y = y + gates[:, kk:kk + 1] * rows_ref[kk]
    o_ref[...] = _rmsnorm(x_ref[...] + y, g_ref[...])


def _combine(dest_blocks, x2, gates_t, g, yb):
    T, D = x2.shape
    return pl.pallas_call(
        _combine_body,
        grid=(T // TOK_TILE,),
        in_specs=[
            pl.BlockSpec((1, 1, TOP_K * TOK_TILE), lambda i: (i, 0, 0), memory_space=pltpu.SMEM),
            pl.BlockSpec((TOK_TILE, D), lambda i: (i, 0)),
            pl.BlockSpec((TOK_TILE, TOP_K), lambda i: (i, 0)),
            _const_spec((1, D)),
            pl.BlockSpec(memory_space=pl.ANY),
        ],
        out_specs=pl.BlockSpec((TOK_TILE, D), lambda i: (i, 0)),
        out_shape=jax.ShapeDtypeStruct((T, D), F32),
        scratch_shapes=[pltpu.VMEM((TOP_K, TOK_TILE, D), F32), pltpu.SemaphoreType.DMA],
        compiler_params=pltpu.CompilerParams(
            dimension_semantics=("arbitrary",), vmem_limit_bytes=VMEM_LIMIT),
        name="combine",
    )(dest_blocks, x2, gates_t, g, yb)


def _row(v):
    return v.reshape(1, -1).astype(F32)


def _layer(x, kv, p):
    B, S, D = x.shape
    T = B * S
    w_in = p["w_in"]
    wa = jnp.zeros((D, GLA_RANK_PAD), F32).at[:, :GLA_GATE_RANK].set(w_in[:, OFF_A:OFF_U])
    wa2 = jnp.zeros((GLA_RANK_PAD, GLA_WIDTH_K), F32).at[:GLA_GATE_RANK].set(p["gla_w_a2"])
    x1 = _mixer(
        x, _row(p["norm_mix_g"]),
        w_in[:, OFF_Q:OFF_A].astype(BF16), wa.astype(BF16), wa2.astype(BF16),
        _row(p["gla_b_a"]), _row(p["gla_out_g"]),
        w_in[:, OFF_U:OFF_SV].astype(BF16), w_in[:, OFF_SV:OFF_G].astype(BF16),
        _row(p["sgu_norm_g"]), _row(p["sgu_norm_b"]), p["sgu_w"].astype(F32), p["sgu_b"].T.astype(F32),
        w_in[:, OFF_G:].astype(BF16), p["w_proj_a"].astype(BF16), p["w_proj_b"].astype(BF16),
        p["w_mix_out"].astype(BF16))

    kmem, vmem = kv
    wr_t = p["w_router"].T.astype(F32)
    wr_hi = wr_t.astype(BF16)
    wr_lo = (wr_t - wr_hi.astype(F32)).astype(BF16)
    x2, h3, topi, gates, rank, counts = _xattn(
        x1, kmem, vmem, _row(p["norm_x_g"]), p["w_xq"].astype(BF16), p["w_xo"].astype(BF16),
        _row(p["norm_ffn_g"]), wr_hi, wr_lo, p["b_router"].reshape(N_EXPERTS, 1).astype(F32))

    counts = counts.reshape(N_EXPERTS)
    blocks_e = (counts + ROW_BLOCK - 1) // ROW_BLOCK
    padded = blocks_e * ROW_BLOCK
    pad_end = jnp.cumsum(padded)
    pad_start = pad_end - padded
    n_blocks = (T * TOP_K) // ROW_BLOCK + N_EXPERTS
    n_slots = (n_blocks + 1) * ROW_BLOCK
    n_used = (pad_end[-1] // ROW_BLOCK).astype(jnp.int32)
    steps_e = (blocks_e + 1) // 2
    step_end = jnp.cumsum(steps_e)
    n_steps = step_end[-1].astype(jnp.int32)
    max_steps = (n_blocks + N_EXPERTS) // 2
    sidx = jnp.minimum(jnp.arange(max_steps, dtype=jnp.int32), n_steps - 1)
    step_e = jnp.minimum(jnp.sum(step_end[None, :] <= sidx[:, None], axis=1), N_EXPERTS - 1).astype(jnp.int32)
    of_step = step_e[:, None] == jnp.arange(N_EXPERTS, dtype=jnp.int32)[None, :]

    def per_step(v):
        return jnp.sum(jnp.where(of_step, v[None, :], 0), axis=1)

    local = sidx - per_step(step_end - steps_e)
    step_blk = (per_step(pad_start // ROW_BLOCK) + 2 * local).astype(jnp.int32)
    step_pair = (2 * local + 1 < per_step(blocks_e)).astype(jnp.int32)
    ids = jnp.arange(N_EXPERTS, dtype=jnp.int32)
    has_rows = steps_e > 0
    run_e = jnp.cumsum(has_rows.astype(jnp.int32)) - 1
    later = (ids[None, :] > ids[:, None]) & has_rows[None, :]
    next_e = jnp.min(jnp.where(later, ids[None, :], N_EXPERTS), axis=1)
    step_run = per_step(run_e).astype(jnp.int32)
    step_next = per_step(next_e).astype(jnp.int32)
    e_ids = jnp.arange(N_EXPERTS, dtype=jnp.int32)[:, None, None]
    dest = jnp.sum(jnp.where(topi[None] == e_ids, pad_start[:, None, None], 0), axis=0) + rank
    def blocked(tile):
        return (dest.reshape(TOP_K, T // tile, tile).transpose(1, 0, 2)
                .reshape(T // tile, 1, TOP_K * tile).astype(jnp.int32))

    dest_blocks = blocked(TOK_TILE)
    zero_start = jnp.maximum(pad_end - ROW_BLOCK, 0).astype(jnp.int32)

    n_used = n_used.reshape(1)
    xb = _dispatch(zero_start, n_used, blocked(DISPATCH_TILE), h3.reshape(T, D), n_slots)
    yb = _experts(step_blk, step_pair, step_e, step_run, step_next, n_steps.reshape(1), n_used, xb,
                  p["w_e1"], p["b_e1"].reshape(N_EXPERTS, 1, 2 * D_FF),
                  p["w_e2"], p["b_e2"].reshape(N_EXPERTS, 1, D))
    return dest_blocks, x2.reshape(T, D), gates.T, yb


def kernel(x, mem, norm_mix_g, w_in, gla_w_a2, gla_b_a, gla_out_g, sgu_norm_g, sgu_norm_b, sgu_w, sgu_b, w_proj_a, w_proj_b, w_mix_out, norm_x_g, norm_mem_g, w_xq, w_xk, w_xv, w_xo, norm_ffn_g, w_router, b_router, w_e1, b_e1, w_e2, b_e2, norm_final_g):
    B, S, D = x.shape
    depth = w_in.shape[0]
    assert depth == 1, "the final norm is fused into the last layer's combine step"
    stacked = dict(norm_mix_g=norm_mix_g, w_in=w_in, gla_w_a2=gla_w_a2, gla_b_a=gla_b_a,
                   gla_out_g=gla_out_g, sgu_norm_g=sgu_norm_g, sgu_norm_b=sgu_norm_b, sgu_w=sgu_w,
                   sgu_b=sgu_b, w_proj_a=w_proj_a, w_proj_b=w_proj_b, w_mix_out=w_mix_out,
                   norm_x_g=norm_x_g, w_xq=w_xq, w_xo=w_xo, norm_ffn_g=norm_ffn_g,
                   w_router=w_router, b_router=b_router, w_e1=w_e1, b_e1=b_e1, w_e2=w_e2, b_e2=b_e2)
    p = {name: v[0] for name, v in stacked.items()}
    kv = _xattn_kv(mem, _row(norm_mem_g[0]), w_xk[0].astype(BF16), w_xv[0].astype(BF16))
    dest_blocks, x2, gates_t, yb = _layer(x, kv, p)
    out = _combine(dest_blocks, x2, gates_t, _row(norm_final_g), yb)
    return out.reshape(B, S, D)
```

```python
import functools

import jax
import jax.numpy as jnp
from jax import lax
from jax.experimental import pallas as pl
from jax.experimental.pallas import tpu as pltpu

F32 = jnp.float32
BF16 = jnp.bfloat16

D_MODEL = 1024
MEM_LEN = 256
EPS = 1e-5

GLA_HEADS = 4
GLA_DV = 128
GLA_DK = 64
GLA_WIDTH_K = GLA_HEADS * GLA_DK
GLA_WIDTH_V = GLA_HEADS * GLA_DV
GLA_GATE_RANK = 16
GLA_RANK_PAD = 128
GLA_TAU = 16.0
GLA_CHUNK = 64

SGU_WIDTH = 512
SGU_GROUPS = 4
SGU_GROUP_DIM = SGU_WIDTH // SGU_GROUPS
SGU_CHUNK = 128

OFF_Q = 0
OFF_K = OFF_Q + GLA_WIDTH_K
OFF_V = OFF_K + GLA_WIDTH_K
OFF_R = OFF_V + GLA_WIDTH_V
OFF_A = OFF_R + GLA_WIDTH_V
OFF_U = OFF_A + GLA_GATE_RANK
OFF_SV = OFF_U + SGU_WIDTH
OFF_G = OFF_SV + SGU_WIDTH

XATTN_HEADS = 4
XATTN_DH = D_MODEL // XATTN_HEADS

N_EXPERTS = 32
TOP_K = 4
D_FF = D_MODEL
SWIGLU_ALPHA = 1.702
SWIGLU_LIMIT = 7.0

SEQ_TILE = 512
ROW_BLOCK = 256
TOK_TILE = 512
DISPATCH_TILE = 1024
VMEM_LIMIT = 56 * 1024 * 1024


def _rmsnorm(x, g):
    return x * lax.rsqrt(jnp.mean(x * x, axis=-1, keepdims=True) + EPS) * g


def _sigmoid(x):
    return 1.0 / (1.0 + jnp.exp(-x))


def _log_sigmoid(x):
    return jnp.minimum(x, 0.0) - jnp.log1p(jnp.exp(-jnp.abs(x)))


def _gelu(x):
    return 0.5 * x * (1.0 + lax.erf(x * (2.0 ** -0.5)))


def _dot(a, b):
    return jnp.dot(a, b, preferred_element_type=F32)


def _dot_nt(a, b):
    return lax.dot_general(a, b, (((1,), (1,)), ((), ())), preferred_element_type=F32)


def _dot_tn(a, b):
    return lax.dot_general(a, b, (((0,), (0,)), ((), ())), preferred_element_type=F32)


def _split_bf16(x):
    hi = x.astype(BF16)
    lo = (x - hi.astype(F32)).astype(BF16)
    return hi, lo


def _const_spec(shape):
    zeros = (0,) * len(shape)
    return pl.BlockSpec(shape, lambda *_: zeros, pipeline_mode=pl.Buffered(1))


def _xattn_kv_body(mem_ref, g_ref, wk_ref, wv_ref, k_ref, v_ref):
    m = _rmsnorm(mem_ref[0], g_ref[...]).astype(BF16)
    k_ref[0] = _dot(m, wk_ref[...]).astype(BF16)
    v_ref[0] = _dot(m, wv_ref[...]).astype(BF16)


def _xattn_kv(mem, g, wk, wv):
    B = mem.shape[0]
    return pl.pallas_call(
        _xattn_kv_body,
        grid=(B,),
        in_specs=[
            pl.BlockSpec((1, MEM_LEN, D_MODEL), lambda b: (b, 0, 0)),
            _const_spec((1, D_MODEL)),
            _const_spec((D_MODEL, D_MODEL)),
            _const_spec((D_MODEL, D_MODEL)),
        ],
        out_specs=[
            pl.BlockSpec((1, MEM_LEN, D_MODEL), lambda b: (b, 0, 0)),
            pl.BlockSpec((1, MEM_LEN, D_MODEL), lambda b: (b, 0, 0)),
        ],
        out_shape=[jax.ShapeDtypeStruct((B, MEM_LEN, D_MODEL), BF16)] * 2,
        compiler_params=pltpu.CompilerParams(
            dimension_semantics=("arbitrary",), vmem_limit_bytes=VMEM_LIMIT),
        name="xattn_kv",
    )(mem, g, wk, wv)


def _gla_branch(h, wqkvr_ref, wa_ref, wa2_ref, ba_ref, outg_ref, st_ref):
    ts = h.shape[0]
    n_chunks = ts // GLA_CHUNK
    p = _dot(h, wqkvr_ref[...])
    q = p[:, OFF_Q:OFF_K] * (GLA_DK ** -0.5)
    k = p[:, OFF_K:OFF_V]
    v = p[:, OFF_V:OFF_R].astype(BF16)
    r = p[:, OFF_R:OFF_A]

    a_low = _dot(h, wa_ref[...]).astype(BF16)
    log_a = _log_sigmoid(_dot(a_low, wa2_ref[...]) + ba_ref[...]) * (1.0 / GLA_TAU)

    row = lax.broadcasted_iota(jnp.int32, (ts, ts), 0)
    col = lax.broadcasted_iota(jnp.int32, (ts, ts), 1)
    tri = jnp.where((col <= row) & ((col >> 6) == (row >> 6)), 1.0, 0.0).astype(BF16)
    la_hi, la_lo = _split_bf16(log_a)
    cum = _dot(tri, la_hi) + _dot(tri, la_lo)

    q_in = (q * jnp.exp(cum)).astype(BF16)
    k_in = (k * jnp.exp(-cum)).astype(BF16)

    lane_k = lax.broadcasted_iota(jnp.int32, (1, GLA_WIDTH_K), 1)
    head_masks = [jnp.where((lane_k >> 6) == hh, 1.0, 0.0).astype(BF16) for hh in range(GLA_HEADS)]
    r4 = lax.broadcasted_iota(jnp.int32, (GLA_HEADS * GLA_CHUNK, GLA_CHUNK), 0)
    c4 = lax.broadcasted_iota(jnp.int32, (GLA_HEADS * GLA_CHUNK, GLA_CHUNK), 1)
    causal4 = c4 <= (r4 & (GLA_CHUNK - 1))
    sr = lax.broadcasted_iota(jnp.int32, (GLA_WIDTH_V, GLA_WIDTH_K), 0)
    sc = lax.broadcasted_iota(jnp.int32, (GLA_WIDTH_V, GLA_WIDTH_K), 1)
    state_mask = jnp.where((sr >> 7) == (sc >> 6), 1.0, 0.0)

    state = st_ref[...]
    outs = []
    for n in range(n_chunks):
        lo, hi = n * GLA_CHUNK, (n + 1) * GLA_CHUNK
        cum_c = cum[lo:hi]
        last = cum[hi - 1:hi]
        q_c = q_in[lo:hi]
        k_c = k_in[lo:hi]
        v_c = v[lo:hi]
        k_out = (k[lo:hi] * jnp.exp(last - cum_c)).astype(BF16)
        q4 = jnp.concatenate([q_c * head_masks[hh] for hh in range(GLA_HEADS)], axis=0)
        att4 = jnp.where(causal4, _dot_nt(q4, k_c), 0.0).astype(BF16)
        oi4 = _dot(att4, v_c)
        o_intra = jnp.concatenate(
            [oi4[hh * GLA_CHUNK:(hh + 1) * GLA_CHUNK, hh * GLA_DV:(hh + 1) * GLA_DV]
             for hh in range(GLA_HEADS)], axis=1)
        o_inter = _dot_nt(q_c, state.astype(BF16))
        outs.append(o_intra + o_inter)
        state = state * jnp.exp(last) + _dot_tn(v_c, k_out) * state_mask
    st_ref[...] = state

    o = jnp.concatenate(outs, axis=0)
    outg = outg_ref[...]
    normed = []
    for hh in range(GLA_HEADS):
        sl = slice(hh * GLA_DV, (hh + 1) * GLA_DV)
        normed.append(_rmsnorm(o[:, sl], outg[:, sl]))
    o = jnp.concatenate(normed, axis=1)
    return o * (r * _sigmoid(r))


def _sgu_branch(h, wu_ref, wsv_ref, ng_ref, nb_ref, sw_ref, sb_ref):
    ts = h.shape[0]
    n_chunks = ts // SGU_CHUNK
    u = _gelu(_dot(h, wu_ref[...]))
    v = _gelu(_dot(h, wsv_ref[...]))
    mu = jnp.mean(v, axis=-1, keepdims=True)
    vc = v - mu
    var = jnp.mean(vc * vc, axis=-1, keepdims=True)
    v = (vc * lax.rsqrt(var + EPS) * ng_ref[...] + nb_ref[...]).astype(BF16)

    row = lax.broadcasted_iota(jnp.int32, (SGU_CHUNK, SGU_CHUNK), 0)
    col = lax.broadcasted_iota(jnp.int32, (SGU_CHUNK, SGU_CHUNK), 1)
    causal = col <= row
    sb = sb_ref[...]
    per_group = []
    for g in range(SGU_GROUPS):
        w = jnp.where(causal, sw_ref[g], 0.0).astype(BF16)
        gs = slice(g * SGU_GROUP_DIM, (g + 1) * SGU_GROUP_DIM)
        vcat = jnp.concatenate(
            [v[c * SGU_CHUNK:(c + 1) * SGU_CHUNK, gs] for c in range(n_chunks)], axis=1)
        per_group.append(_dot(w, vcat) + sb[:, g:g + 1])
    rows = []
    for c in range(n_chunks):
        cs = slice(c * SGU_GROUP_DIM, (c + 1) * SGU_GROUP_DIM)
        rows.append(jnp.concatenate([per_group[g][:, cs] for g in range(SGU_GROUPS)], axis=1))
    mixed = jnp.concatenate(rows, axis=0)
    return u * mixed


def _mixer_body(x_ref, g_ref, wqkvr_ref, wa_ref, wa2_ref, ba_ref, outg_ref,
                wu_ref, wsv_ref, ng_ref, nb_ref, sw_ref, sb_ref,
                wg_ref, wpa_ref, wpb_ref, wmix_ref, o_ref, st_ref):
    @pl.when(pl.program_id(1) == 0)
    def _():
        st_ref[...] = jnp.zeros_like(st_ref)

    x = x_ref[0]
    h = _rmsnorm(x, g_ref[...]).astype(BF16)
    ya_in = _gla_branch(h, wqkvr_ref, wa_ref, wa2_ref, ba_ref, outg_ref, st_ref)
    y_a = _dot(ya_in.astype(BF16), wpa_ref[...])
    yb_in = _sgu_branch(h, wu_ref, wsv_ref, ng_ref, nb_ref, sw_ref, sb_ref)
    y_b = _dot(yb_in.astype(BF16), wpb_ref[...])
    gates = _sigmoid(_dot(h, wg_ref[...]))
    mix = gates[:, :D_MODEL] * y_a + gates[:, D_MODEL:] * y_b
    o_ref[0] = x + _dot(mix.astype(BF16), wmix_ref[...])


def _mixer(x, g, wqkvr, wa, wa2, ba, outg, wu, wsv, ng, nb, sw, sb, wg, wpa, wpb, wmix):
    B, S, D = x.shape
    consts = (g, wqkvr, wa, wa2, ba, outg, wu, wsv, ng, nb, sw, sb, wg, wpa, wpb, wmix)
    return pl.pallas_call(
        _mixer_body,
        grid=(B, S // SEQ_TILE),
        in_specs=[pl.BlockSpec((1, SEQ_TILE, D), lambda b, j: (b, j, 0))]
                 + [_const_spec(c.shape) for c in consts],
        out_specs=pl.BlockSpec((1, SEQ_TILE, D), lambda b, j: (b, j, 0)),
        out_shape=jax.ShapeDtypeStruct((B, S, D), F32),
        scratch_shapes=[pltpu.VMEM((GLA_WIDTH_V, GLA_WIDTH_K), F32)],
        compiler_params=pltpu.CompilerParams(
            dimension_semantics=("arbitrary", "arbitrary"), vmem_limit_bytes=VMEM_LIMIT),
        name="mixer",
    )(x, *consts)


def _xattn_body(x_ref, k_ref, v_ref, gx_ref, wq_ref, wo_ref, gf_ref, wr_hi_ref, wr_lo_ref, br_ref,
                x2_ref, h3_ref, topi_ref, gate_ref, rank_ref, cnt_ref, carry_ref):
    first = (pl.program_id(0) == 0) & (pl.program_id(1) == 0)

    @pl.when(first)
    def _():
        carry_ref[...] = jnp.zeros_like(carry_ref)

    x = x_ref[0]
    ts = x.shape[0]
    h = _rmsnorm(x, gx_ref[...]).astype(BF16)
    q = _dot(h, wq_ref[...]).astype(BF16)
    km = k_ref[0]
    vm = v_ref[0]
    heads = []
    for hh in range(XATTN_HEADS):
        sl = slice(hh * XATTN_DH, (hh + 1) * XATTN_DH)
        s = _dot_nt(q[:, sl], km[:, sl]) * (XATTN_DH ** -0.5)
        s = s - jnp.max(s, axis=-1, keepdims=True)
        e = jnp.exp(s)
        p = e / jnp.sum(e, axis=-1, keepdims=True)
        heads.append(_dot(p.astype(BF16), vm[:, sl]).astype(BF16))
    o = jnp.concatenate(heads, axis=1)
    x2 = x + _dot(o, wo_ref[...])
    x2_ref[0] = x2

    h3 = _rmsnorm(x2, gf_ref[...])
    h3_ref[0] = h3

    h_hi, h_lo = _split_bf16(h3)
    logits = (_dot_nt(wr_hi_ref[...], h_hi) + _dot_nt(wr_hi_ref[...], h_lo)
              + _dot_nt(wr_lo_ref[...], h_hi)) + br_ref[...]

    e_iota = lax.broadcasted_iota(jnp.int32, (N_EXPERTS, ts), 0)
    work = logits
    vals, idxs, hots = [], [], []
    for _ in range(TOP_K):
        m = jnp.max(work, axis=0, keepdims=True)
        idx = jnp.min(jnp.where(work == m, e_iota, N_EXPERTS), axis=0, keepdims=True)
        hot = e_iota == idx
        vals.append(m)
        idxs.append(idx)
        hots.append(hot)
        work = jnp.where(hot, -jnp.inf, work)
    exps = [jnp.exp(vv - vals[0]) for vv in vals]
    denom = exps[0] + exps[1] + exps[2] + exps[3]
    gate_ref[...] = jnp.concatenate([ee / denom for ee in exps], axis=0)
    topi_ref[...] = jnp.concatenate(idxs, axis=0)

    multi = jnp.where(hots[0] | hots[1] | hots[2] | hots[3], 1.0, 0.0)
    srow = lax.broadcasted_iota(jnp.int32, (ts, ts), 0)
    scol = lax.broadcasted_iota(jnp.int32, (ts, ts), 1)
    strict = jnp.where(srow < scol, 1.0, 0.0).astype(BF16)
    before = _dot(multi.astype(BF16), strict) + carry_ref[...]
    ranks = [jnp.sum(jnp.where(hot, before, 0.0), axis=0, keepdims=True) for hot in hots]
    rank_ref[...] = jnp.concatenate(ranks, axis=0).astype(jnp.int32)
    carry = carry_ref[...] + jnp.sum(multi, axis=1, keepdims=True)
    carry_ref[...] = carry
    cnt_ref[...] = carry.astype(jnp.int32)


def _xattn(x1, kmem, vmem, gx, wq, wo, gf, wr_hi, wr_lo, br):
    B, S, D = x1.shape
    T = B * S
    nj = S // SEQ_TILE
    tok_spec = pl.BlockSpec((1, SEQ_TILE, D), lambda b, j: (b, j, 0))
    mem_spec = pl.BlockSpec((1, MEM_LEN, D), lambda b, j: (b, 0, 0))
    lane_spec = pl.BlockSpec((TOP_K, SEQ_TILE), lambda b, j: (0, b * nj + j))
    consts = (gx, wq, wo, gf, wr_hi, wr_lo, br)
    return pl.pallas_call(
        _xattn_body,
        grid=(B, nj),
        in_specs=[tok_spec, mem_spec, mem_spec] + [_const_spec(c.shape) for c in consts],
        out_specs=[tok_spec, tok_spec, lane_spec, lane_spec, lane_spec,
                   pl.BlockSpec((N_EXPERTS, 1), lambda b, j: (0, 0))],
        out_shape=[
            jax.ShapeDtypeStruct((B, S, D), F32),
            jax.ShapeDtypeStruct((B, S, D), F32),
            jax.ShapeDtypeStruct((TOP_K, T), jnp.int32),
            jax.ShapeDtypeStruct((TOP_K, T), F32),
            jax.ShapeDtypeStruct((TOP_K, T), jnp.int32),
            jax.ShapeDtypeStruct((N_EXPERTS, 1), jnp.int32),
        ],
        scratch_shapes=[pltpu.VMEM((N_EXPERTS, 1), F32)],
        compiler_params=pltpu.CompilerParams(
            dimension_semantics=("arbitrary", "arbitrary"), vmem_limit_bytes=VMEM_LIMIT),
        name="xattn_router",
    )(x1, kmem, vmem, *consts)


ROW_GROUP = 8


def _dispatch_body(zs_ref, nu_ref, dest_ref, h_ref, xb_ref, zbuf_ref, zsem, sem):
    @pl.when(pl.program_id(0) == 0)
    def _():
        zbuf_ref[...] = jnp.zeros_like(zbuf_ref)
        n_blocks = xb_ref.shape[0] // ROW_BLOCK

        def zero_copy(start):
            start = pl.multiple_of(start, ROW_BLOCK)
            return pltpu.make_async_copy(zbuf_ref, xb_ref.at[pl.ds(start, ROW_BLOCK)], zsem)

        for e in range(N_EXPERTS):
            zero_copy(zs_ref[e]).start()
        for e in range(N_EXPERTS):
            zero_copy(zs_ref[e]).wait()

        def tail(blk, c):
            cp = zero_copy(blk * ROW_BLOCK)
            cp.start()
            cp.wait()
            return c

        lax.fori_loop(nu_ref[0], n_blocks, tail, 0)

    def start(g, c):
        base = pl.multiple_of(g * ROW_GROUP, ROW_GROUP)
        tile = h_ref.at[pl.ds(base, ROW_GROUP)]
        for j in range(ROW_GROUP):
            for kk in range(TOP_K):
                d = dest_ref[0, 0, kk * DISPATCH_TILE + base + j]
                pltpu.make_async_copy(tile.at[pl.ds(j, 1)], xb_ref.at[pl.ds(d, 1)], sem).start(
                    priority=kk % 2)
        return c

    lax.fori_loop(0, DISPATCH_TILE // ROW_GROUP, start, 0)
    for kk in range(TOP_K):
        pltpu.make_async_copy(h_ref, xb_ref.at[pl.ds(0, DISPATCH_TILE)], sem).wait()


def _dispatch(zero_start, n_used, dest_blocks, h3, n_slots):
    T, D = h3.shape
    return pl.pallas_call(
        _dispatch_body,
        grid_spec=pltpu.PrefetchScalarGridSpec(
            num_scalar_prefetch=2,
            grid=(T // DISPATCH_TILE,),
            in_specs=[
                pl.BlockSpec((1, 1, TOP_K * DISPATCH_TILE), lambda i, zs, nu: (i, 0, 0),
                             memory_space=pltpu.SMEM),
                pl.BlockSpec((DISPATCH_TILE, D), lambda i, zs, nu: (i, 0)),
            ],
            out_specs=pl.BlockSpec(memory_space=pl.ANY),
            scratch_shapes=[pltpu.VMEM((ROW_BLOCK, D), F32),
                            pltpu.SemaphoreType.DMA, pltpu.SemaphoreType.DMA],
        ),
        out_shape=jax.ShapeDtypeStruct((n_slots, D), F32),
        compiler_params=pltpu.CompilerParams(
            dimension_semantics=("arbitrary",), vmem_limit_bytes=VMEM_LIMIT),
        name="dispatch",
    )(zero_start, n_used, dest_blocks, h3)


def _expert_mlp(xb, w1b_ref, b1_ref, w2b_ref, b2_ref):
    hcat = _dot(xb.astype(BF16), w1b_ref[...]) + b1_ref[0]
    glu = jnp.minimum(hcat[:, :D_FF], SWIGLU_LIMIT)
    lin = jnp.clip(hcat[:, D_FF:], -SWIGLU_LIMIT, SWIGLU_LIMIT)
    act = (lin + 1.0) * glu * _sigmoid(SWIGLU_ALPHA * glu)
    return _dot(act.astype(BF16), w2b_ref[...]) + b2_ref[0]


def _experts_body(blk_ref, pair_ref, se_ref, run_ref, nxt_ref, ns_ref, nu_ref,
                  x_ref, w1_hbm, b1_ref, w2_hbm, b2_ref,
                  y_hbm, w1b_ref, w2b_ref, wf1_ref, wf2_ref, ybuf_ref, zbuf_ref, sems, wsems, zsem):
    s = pl.program_id(0)
    n_steps = ns_ref[0]
    last = pl.num_programs(0) - 1
    total_blocks = y_hbm.shape[0] // ROW_BLOCK

    def tail_copy(blk):
        row0 = pl.multiple_of(blk * ROW_BLOCK, ROW_BLOCK)
        return pltpu.make_async_copy(zbuf_ref, y_hbm.at[pl.ds(row0, ROW_BLOCK)], zsem)

    @pl.when(s == 0)
    def _():
        zbuf_ref[...] = jnp.zeros_like(zbuf_ref)

        def go(blk, c):
            tail_copy(blk).start()
            return c

        lax.fori_loop(nu_ref[0], total_blocks, go, 0)

    @pl.when(s == last)
    def _():
        def done(blk, c):
            tail_copy(blk).wait()
            return c

        lax.fori_loop(nu_ref[0], total_blocks, done, 0)

    def out_copy(step, half):
        slot = step % 2
        row0 = pl.multiple_of((blk_ref[step] + half) * ROW_BLOCK, ROW_BLOCK)
        return pltpu.make_async_copy(
            ybuf_ref.at[slot, pl.ds(half * ROW_BLOCK, ROW_BLOCK)],
            y_hbm.at[pl.ds(row0, ROW_BLOCK)], sems.at[slot])

    def wait_step(step):
        out_copy(step, 0).wait()

        @pl.when(pair_ref[step] == 1)
        def _():
            out_copy(step, 1).wait()

    def weight_copies(e, wslot):
        return (pltpu.make_async_copy(w1_hbm.at[e], wf1_ref.at[wslot], wsems.at[wslot]),
                pltpu.make_async_copy(w2_hbm.at[e], wf2_ref.at[wslot], wsems.at[wslot]))

    @pl.when(s < n_steps)
    def _():
        prev = se_ref[jnp.maximum(s - 1, 0)]
        wslot = run_ref[s] % 2

        @pl.when(s == 0)
        def _():
            for cp in weight_copies(se_ref[0], 0):
                cp.start()

        @pl.when((s == 0) | (se_ref[s] != prev))
        def _():
            for cp in weight_copies(se_ref[s], wslot):
                cp.wait()
            w1b_ref[...] = wf1_ref[wslot].astype(BF16)
            w2b_ref[...] = wf2_ref[wslot].astype(BF16)

            @pl.when(nxt_ref[s] < N_EXPERTS)
            def _():
                for cp in weight_copies(nxt_ref[s], 1 - wslot):
                    cp.start()

        slot = s % 2

        @pl.when(pair_ref[s] == 1)
        def _():
            ybuf_ref[slot] = _expert_mlp(x_ref[...], w1b_ref, b1_ref, w2b_ref, b2_ref)

        @pl.when(pair_ref[s] == 0)
        def _():
            ybuf_ref[slot, :ROW_BLOCK] = _expert_mlp(x_ref[:ROW_BLOCK], w1b_ref, b1_ref, w2b_ref, b2_ref)

    @pl.when((s >= 1) & (s - 1 < n_steps))
    def _():
        wait_step(s - 1)

    @pl.when(s < n_steps)
    def _():
        out_copy(s, 0).start()

        @pl.when(pair_ref[s] == 1)
        def _():
            out_copy(s, 1).start()

        @pl.when(s == last)
        def _():
            wait_step(s)


def _experts(step_blk, step_pair, step_e, step_run, step_next, n_steps, n_used, xb, w1, b1, w2, b2):
    n_rows, D = xb.shape

    def x_map(s, blk, pair, se, run, nxt, ns, nu):
        return (blk[s] * ROW_BLOCK, 0)

    def b_map(s, blk, pair, se, run, nxt, ns, nu):
        return (se[s], 0, 0)

    return pl.pallas_call(
        _experts_body,
        grid_spec=pltpu.PrefetchScalarGridSpec(
            num_scalar_prefetch=7,
            grid=(step_blk.shape[0],),
            in_specs=[
                pl.BlockSpec((pl.Element(2 * ROW_BLOCK), pl.Element(D)), x_map),
                pl.BlockSpec(memory_space=pl.ANY),
                pl.BlockSpec((1, 1, 2 * D_FF), b_map),
                pl.BlockSpec(memory_space=pl.ANY),
                pl.BlockSpec((1, 1, D), b_map),
            ],
            out_specs=pl.BlockSpec(memory_space=pl.ANY),
            scratch_shapes=[pltpu.VMEM((D, 2 * D_FF), BF16), pltpu.VMEM((D_FF, D), BF16),
                            pltpu.VMEM((2, D, 2 * D_FF), F32), pltpu.VMEM((2, D_FF, D), F32),
                            pltpu.VMEM((2, 2 * ROW_BLOCK, D), F32), pltpu.VMEM((ROW_BLOCK, D), F32),
                            pltpu.SemaphoreType.DMA((2,)), pltpu.SemaphoreType.DMA((2,)),
                            pltpu.SemaphoreType.DMA],
        ),
        out_shape=jax.ShapeDtypeStruct((n_rows, D), F32),
        compiler_params=pltpu.CompilerParams(
            dimension_semantics=("arbitrary",), vmem_limit_bytes=VMEM_LIMIT),
        name="experts",
    )(step_blk, step_pair, step_e, step_run, step_next, n_steps, n_used, xb, w1, b1, w2, b2)


def _combine_body(dest_ref, x_ref, gate_ref, g_ref, yb_ref, o_ref, rows_ref, sem):
    def start(g, c):
        base = pl.multiple_of(g * ROW_GROUP, ROW_GROUP)
        for kk in range(TOP_K):
            tile = rows_ref.at[kk, pl.ds(base, ROW_GROUP)]
            for j in range(ROW_GROUP):
                d = dest_ref[0, 0, kk * TOK_TILE + base + j]
                pltpu.make_async_copy(yb_ref.at[pl.ds(d, 1)], tile.at[pl.ds(j, 1)], sem).start(
                    priority=kk % 2)
        return c

    lax.fori_loop(0, TOK_TILE // ROW_GROUP, start, 0)
    for kk in range(TOP_K):
        pltpu.make_async_copy(yb_ref.at[pl.ds(0, TOK_TILE)], rows_ref.at[kk], sem).wait()

    gates = gate_ref[...]
    y = gates[:, 0:1] * rows_ref[0]
    for kk in range(1, TOP_K):
        y = y + gates[:, kk:kk + 1] * rows_ref[kk]
    o_ref[...] = _rmsnorm(x_ref[...] + y, g_ref[...])


def _combine(dest_blocks, x2, gates_t, g, yb):
    T, D = x2.shape
    return pl.pallas_call(
        _combine_body,
        grid=(T // TOK_TILE,),
        in_specs=[
            pl.BlockSpec((1, 1, TOP_K * TOK_TILE), lambda i: (i, 0, 0), memory_space=pltpu.SMEM),
            pl.BlockSpec((TOK_TILE, D), lambda i: (i, 0)),
            pl.BlockSpec((TOK_TILE, TOP_K), lambda i: (i, 0)),
            _const_spec((1, D)),
            pl.BlockSpec(memory_space=pl.ANY),
        ],
        out_specs=pl.BlockSpec((TOK_TILE, D), lambda i: (i, 0)),
        out_shape=jax.ShapeDtypeStruct((T, D), F32),
        scratch_shapes=[pltpu.VMEM((TOP_K, TOK_TILE, D), F32), pltpu.SemaphoreType.DMA],
        compiler_params=pltpu.CompilerParams(
            dimension_semantics=("arbitrary",), vmem_limit_bytes=VMEM_LIMIT),
        name="combine",
    )(dest_blocks, x2, gates_t, g, yb)


def _row(v):
    return v.reshape(1, -1).astype(F32)


def _layer(x, kv, p):
    B, S, D = x.shape
    T = B * S
    w_in = p["w_in"]
    wa = jnp.zeros((D, GLA_RANK_PAD), F32).at[:, :GLA_GATE_RANK].set(w_in[:, OFF_A:OFF_U])
    wa2 = jnp.zeros((GLA_RANK_PAD, GLA_WIDTH_K), F32).at[:GLA_GATE_RANK].set(p["gla_w_a2"])
    x1 = _mixer(
        x, _row(p["norm_mix_g"]),
        w_in[:, OFF_Q:OFF_A].astype(BF16), wa.astype(BF16), wa2.astype(BF16),
        _row(p["gla_b_a"]), _row(p["gla_out_g"]),
        w_in[:, OFF_U:OFF_SV].astype(BF16), w_in[:, OFF_SV:OFF_G].astype(BF16),
        _row(p["sgu_norm_g"]), _row(p["sgu_norm_b"]), p["sgu_w"].astype(F32), p["sgu_b"].T.astype(F32),
        w_in[:, OFF_G:].astype(BF16), p["w_proj_a"].astype(BF16), p["w_proj_b"].astype(BF16),
        p["w_mix_out"].astype(BF16))

    kmem, vmem = kv
    wr_t = p["w_router"].T.astype(F32)
    wr_hi = wr_t.astype(BF16)
    wr_lo = (wr_t - wr_hi.astype(F32)).astype(BF16)
    x2, h3, topi, gates, rank, counts = _xattn(
        x1, kmem, vmem, _row(p["norm_x_g"]), p["w_xq"].astype(BF16), p["w_xo"].astype(BF16),
        _row(p["norm_ffn_g"]), wr_hi, wr_lo, p["b_router"].reshape(N_EXPERTS, 1).astype(F32))

    counts = counts.reshape(N_EXPERTS)
    blocks_e = (counts + ROW_BLOCK - 1) // ROW_BLOCK
    padded = blocks_e * ROW_BLOCK
    pad_end = jnp.cumsum(padded)
    pad_start = pad_end - padded
    n_blocks = (T * TOP_K) // ROW_BLOCK + N_EXPERTS
    n_slots = (n_blocks + 1) * ROW_BLOCK
    n_used = (pad_end[-1] // ROW_BLOCK).astype(jnp.int32)
    steps_e = (blocks_e + 1) // 2
    step_end = jnp.cumsum(steps_e)
    n_steps = step_end[-1].astype(jnp.int32)
    max_steps = (n_blocks + N_EXPERTS) // 2
    sidx = jnp.minimum(jnp.arange(max_steps, dtype=jnp.int32), n_steps - 1)
    step_e = jnp.minimum(jnp.sum(step_end[None, :] <= sidx[:, None], axis=1), N_EXPERTS - 1).astype(jnp.int32)
    of_step = step_e[:, None] == jnp.arange(N_EXPERTS, dtype=jnp.int32)[None, :]

    def per_step(v):
        return jnp.sum(jnp.where(of_step, v[None, :], 0), axis=1)

    local = sidx - per_step(step_end - steps_e)
    step_blk = (per_step(pad_start // ROW_BLOCK) + 2 * local).astype(jnp.int32)
    step_pair = (2 * local + 1 < per_step(blocks_e)).astype(jnp.int32)
    ids = jnp.arange(N_EXPERTS, dtype=jnp.int32)
    has_rows = steps_e > 0
    run_e = jnp.cumsum(has_rows.astype(jnp.int32)) - 1
    later = (ids[None, :] > ids[:, None]) & has_rows[None, :]
    next_e = jnp.min(jnp.where(later, ids[None, :], N_EXPERTS), axis=1)
    step_run = per_step(run_e).astype(jnp.int32)
    step_next = per_step(next_e).astype(jnp.int32)
    e_ids = jnp.arange(N_EXPERTS, dtype=jnp.int32)[:, None, None]
    dest = jnp.sum(jnp.where(topi[None] == e_ids, pad_start[:, None, None], 0), axis=0) + rank
    def blocked(tile):
        return (dest.reshape(TOP_K, T // tile, tile).transpose(1, 0, 2)
                .reshape(T // tile, 1, TOP_K * tile).astype(jnp.int32))

    dest_blocks = blocked(TOK_TILE)
    zero_start = jnp.maximum(pad_end - ROW_BLOCK, 0).astype(jnp.int32)

    n_used = n_used.reshape(1)
    xb = _dispatch(zero_start, n_used, blocked(DISPATCH_TILE), h3.reshape(T, D), n_slots)
    yb = _experts(step_blk, step_pair, step_e, step_run, step_next, n_steps.reshape(1), n_used, xb,
                  p["w_e1"], p["b_e1"].reshape(N_EXPERTS, 1, 2 * D_FF),
                  p["w_e2"], p["b_e2"].reshape(N_EXPERTS, 1, D))
    return dest_blocks, x2.reshape(T, D), gates.T, yb


def kernel(x, mem, norm_mix_g, w_in, gla_w_a2, gla_b_a, gla_out_g, sgu_norm_g, sgu_norm_b, sgu_w, sgu_b, w_proj_a, w_proj_b, w_mix_out, norm_x_g, norm_mem_g, w_xq, w_xk, w_xv, w_xo, norm_ffn_g, w_router, b_router, w_e1, b_e1, w_e2, b_e2, norm_final_g):
    B, S, D = x.shape
    depth = w_in.shape[0]
    assert depth == 1, "the final norm is fused into the last layer's combine step"
    stacked = dict(norm_mix_g=norm_mix_g, w_in=w_in, gla_w_a2=gla_w_a2, gla_b_a=gla_b_a,
                   gla_out_g=gla_out_g, sgu_norm_g=sgu_norm_g, sgu_norm_b=sgu_norm_b, sgu_w=sgu_w,
                   sgu_b=sgu_b, w_proj_a=w_proj_a, w_proj_b=w_proj_b, w_mix_out=w_mix_out,
                   norm_x_g=norm_x_g, w_xq=w_xq, w_xo=w_xo, norm_ffn_g=norm_ffn_g,
                   w_router=w_router, b_router=b_router, w_e1=w_e1, b_e1=b_e1, w_e2=w_e2, b_e2=b_e2)
    p = {name: v[0] for name, v in stacked.items()}
    kv = _xattn_kv(mem, _row(norm_mem_g[0]), w_xk[0].astype(BF16), w_xv[0].astype(BF16))
    dest_blocks, x2, gates_t, yb = _layer(x, kv, p)
    out = _combine(dest_blocks, x2, gates_t, _row(norm_final_g), yb)
    return out.reshape(B, S, D)
```

```python
import functools

import jax
import jax.numpy as jnp
from jax import lax
from jax.experimental import pallas as pl
from jax.experimental.pallas import tpu as pltpu

F32 = jnp.float32
BF16 = jnp.bfloat16

D_MODEL = 1024
MEM_LEN = 256
EPS = 1e-5

GLA_HEADS = 4
GLA_DV = 128
GLA_DK = 64
GLA_WIDTH_K = GLA_HEADS * GLA_DK
GLA_WIDTH_V = GLA_HEADS * GLA_DV
GLA_GATE_RANK = 16
GLA_RANK_PAD = 128
GLA_TAU = 16.0
GLA_CHUNK = 64

SGU_WIDTH = 512
SGU_GROUPS = 4
SGU_GROUP_DIM = SGU_WIDTH // SGU_GROUPS
SGU_CHUNK = 128

OFF_Q = 0
OFF_K = OFF_Q + GLA_WIDTH_K
OFF_V = OFF_K + GLA_WIDTH_K
OFF_R = OFF_V + GLA_WIDTH_V
OFF_A = OFF_R + GLA_WIDTH_V
OFF_U = OFF_A + GLA_GATE_RANK
OFF_SV = OFF_U + SGU_WIDTH
OFF_G = OFF_SV + SGU_WIDTH

XATTN_HEADS = 4
XATTN_DH = D_MODEL // XATTN_HEADS

N_EXPERTS = 32
TOP_K = 4
D_FF = D_MODEL
SWIGLU_ALPHA = 1.702
SWIGLU_LIMIT = 7.0

SEQ_TILE = 512
ROW_BLOCK = 256
TOK_TILE = 1024
DISPATCH_TILE = 2048
VMEM_LIMIT = 56 * 1024 * 1024


def _rmsnorm(x, g):
    return x * lax.rsqrt(jnp.mean(x * x, axis=-1, keepdims=True) + EPS) * g


def _sigmoid(x):
    return 1.0 / (1.0 + jnp.exp(-x))


def _log_sigmoid(x):
    return jnp.minimum(x, 0.0) - jnp.log1p(jnp.exp(-jnp.abs(x)))


def _gelu(x):
    return 0.5 * x * (1.0 + lax.erf(x * (2.0 ** -0.5)))


def _dot(a, b):
    return jnp.dot(a, b, preferred_element_type=F32)


def _dot_nt(a, b):
    return lax.dot_general(a, b, (((1,), (1,)), ((), ())), preferred_element_type=F32)


def _dot_tn(a, b):
    return lax.dot_general(a, b, (((0,), (0,)), ((), ())), preferred_element_type=F32)


def _split_bf16(x):
    hi = x.astype(BF16)
    lo = (x - hi.astype(F32)).astype(BF16)
    return hi, lo


def _const_spec(shape):
    zeros = (0,) * len(shape)
    return pl.BlockSpec(shape, lambda *_: zeros, pipeline_mode=pl.Buffered(1))


def _xattn_kv_body(mem_ref, g_ref, wk_ref, wv_ref, k_ref, v_ref):
    m = _rmsnorm(mem_ref[0], g_ref[...]).astype(BF16)
    k_ref[0] = _dot(m, wk_ref[...]).astype(BF16)
    v_ref[0] = _dot(m, wv_ref[...]).astype(BF16)


def _xattn_kv(mem, g, wk, wv):
    B = mem.shape[0]
    return pl.pallas_call(
        _xattn_kv_body,
        grid=(B,),
        in_specs=[
            pl.BlockSpec((1, MEM_LEN, D_MODEL), lambda b: (b, 0, 0)),
            _const_spec((1, D_MODEL)),
            _const_spec((D_MODEL, D_MODEL)),
            _const_spec((D_MODEL, D_MODEL)),
        ],
        out_specs=[
            pl.BlockSpec((1, MEM_LEN, D_MODEL), lambda b: (b, 0, 0)),
            pl.BlockSpec((1, MEM_LEN, D_MODEL), lambda b: (b, 0, 0)),
        ],
        out_shape=[jax.ShapeDtypeStruct((B, MEM_LEN, D_MODEL), BF16)] * 2,
        compiler_params=pltpu.CompilerParams(
            dimension_semantics=("arbitrary",), vmem_limit_bytes=VMEM_LIMIT),
        name="xattn_kv",
    )(mem, g, wk, wv)


def _gla_branch(h, wqkvr_ref, wa_ref, wa2_ref, ba_ref, outg_ref, st_ref):
    ts = h.shape[0]
    n_chunks = ts // GLA_CHUNK
    p = _dot(h, wqkvr_ref[...])
    q = p[:, OFF_Q:OFF_K] * (GLA_DK ** -0.5)
    k = p[:, OFF_K:OFF_V]
    v = p[:, OFF_V:OFF_R].astype(BF16)
    r = p[:, OFF_R:OFF_A]

    a_low = _dot(h, wa_ref[...]).astype(BF16)
    log_a = _log_sigmoid(_dot(a_low, wa2_ref[...]) + ba_ref[...]) * (1.0 / GLA_TAU)

    row = lax.broadcasted_iota(jnp.int32, (ts, ts), 0)
    col = lax.broadcasted_iota(jnp.int32, (ts, ts), 1)
    tri = jnp.where((col <= row) & ((col >> 6) == (row >> 6)), 1.0, 0.0).astype(BF16)
    la_hi, la_lo = _split_bf16(log_a)
    cum = _dot(tri, la_hi) + _dot(tri, la_lo)

    q_in = (q * jnp.exp(cum)).astype(BF16)
    k_in = (k * jnp.exp(-cum)).astype(BF16)

    lane_k = lax.broadcasted_iota(jnp.int32, (1, GLA_WIDTH_K), 1)
    head_masks = [jnp.where((lane_k >> 6) == hh, 1.0, 0.0).astype(BF16) for hh in range(GLA_HEADS)]
    r4 = lax.broadcasted_iota(jnp.int32, (GLA_HEADS * GLA_CHUNK, GLA_CHUNK), 0)
    c4 = lax.broadcasted_iota(jnp.int32, (GLA_HEADS * GLA_CHUNK, GLA_CHUNK), 1)
    causal4 = c4 <= (r4 & (GLA_CHUNK - 1))
    sr = lax.broadcasted_iota(jnp.int32, (GLA_WIDTH_V, GLA_WIDTH_K), 0)
    sc = lax.broadcasted_iota(jnp.int32, (GLA_WIDTH_V, GLA_WIDTH_K), 1)
    state_mask = jnp.where((sr >> 7) == (sc >> 6), 1.0, 0.0)

    state = st_ref[...]
    outs = []
    for n in range(n_chunks):
        lo, hi = n * GLA_CHUNK, (n + 1) * GLA_CHUNK
        cum_c = cum[lo:hi]
        last = cum[hi - 1:hi]
        q_c = q_in[lo:hi]
        k_c = k_in[lo:hi]
        v_c = v[lo:hi]
        k_out = (k[lo:hi] * jnp.exp(last - cum_c)).astype(BF16)
        q4 = jnp.concatenate([q_c * head_masks[hh] for hh in range(GLA_HEADS)], axis=0)
        att4 = jnp.where(causal4, _dot_nt(q4, k_c), 0.0).astype(BF16)
        oi4 = _dot(att4, v_c)
        o_intra = jnp.concatenate(
            [oi4[hh * GLA_CHUNK:(hh + 1) * GLA_CHUNK, hh * GLA_DV:(hh + 1) * GLA_DV]
             for hh in range(GLA_HEADS)], axis=1)
        o_inter = _dot_nt(q_c, state.astype(BF16))
        outs.append(o_intra + o_inter)
        state = state * jnp.exp(last) + _dot_tn(v_c, k_out) * state_mask
    st_ref[...] = state

    o = jnp.concatenate(outs, axis=0)
    outg = outg_ref[...]
    normed = []
    for hh in range(GLA_HEADS):
        sl = slice(hh * GLA_DV, (hh + 1) * GLA_DV)
        normed.append(_rmsnorm(o[:, sl], outg[:, sl]))
    o = jnp.concatenate(normed, axis=1)
    return o * (r * _sigmoid(r))


def _sgu_branch(h, wu_ref, wsv_ref, ng_ref, nb_ref, sw_ref, sb_ref):
    ts = h.shape[0]
    n_chunks = ts // SGU_CHUNK
    u = _gelu(_dot(h, wu_ref[...]))
    v = _gelu(_dot(h, wsv_ref[...]))
    mu = jnp.mean(v, axis=-1, keepdims=True)
    vc = v - mu
    var = jnp.mean(vc * vc, axis=-1, keepdims=True)
    v = (vc * lax.rsqrt(var + EPS) * ng_ref[...] + nb_ref[...]).astype(BF16)

    row = lax.broadcasted_iota(jnp.int32, (SGU_CHUNK, SGU_CHUNK), 0)
    col = lax.broadcasted_iota(jnp.int32, (SGU_CHUNK, SGU_CHUNK), 1)
    causal = col <= row
    sb = sb_ref[...]
    per_group = []
    for g in range(SGU_GROUPS):
        w = jnp.where(causal, sw_ref[g], 0.0).astype(BF16)
        gs = slice(g * SGU_GROUP_DIM, (g + 1) * SGU_GROUP_DIM)
        vcat = jnp.concatenate(
            [v[c * SGU_CHUNK:(c + 1) * SGU_CHUNK, gs] for c in range(n_chunks)], axis=1)
        per_group.append(_dot(w, vcat) + sb[:, g:g + 1])
    rows = []
    for c in range(n_chunks):
        cs = slice(c * SGU_GROUP_DIM, (c + 1) * SGU_GROUP_DIM)
        rows.append(jnp.concatenate([per_group[g][:, cs] for g in range(SGU_GROUPS)], axis=1))
    mixed = jnp.concatenate(rows, axis=0)
    return u * mixed


def _mixer_body(x_ref, g_ref, wqkvr_ref, wa_ref, wa2_ref, ba_ref, outg_ref,
                wu_ref, wsv_ref, ng_ref, nb_ref, sw_ref, sb_ref,
                wg_ref, wpa_ref, wpb_ref, wmix_ref, o_ref, st_ref):
    @pl.when(pl.program_id(1) == 0)
    def _():
        st_ref[...] = jnp.zeros_like(st_ref)

    x = x_ref[0]
    h = _rmsnorm(x, g_ref[...]).astype(BF16)
    ya_in = _gla_branch(h, wqkvr_ref, wa_ref, wa2_ref, ba_ref, outg_ref, st_ref)
    y_a = _dot(ya_in.astype(BF16), wpa_ref[...])
    yb_in = _sgu_branch(h, wu_ref, wsv_ref, ng_ref, nb_ref, sw_ref, sb_ref)
    y_b = _dot(yb_in.astype(BF16), wpb_ref[...])
    gates = _sigmoid(_dot(h, wg_ref[...]))
    mix = gates[:, :D_MODEL] * y_a + gates[:, D_MODEL:] * y_b
    o_ref[0] = x + _dot(mix.astype(BF16), wmix_ref[...])


def _mixer(x, g, wqkvr, wa, wa2, ba, outg, wu, wsv, ng, nb, sw, sb, wg, wpa, wpb, wmix):
    B, S, D = x.shape
    consts = (g, wqkvr, wa, wa2, ba, outg, wu, wsv, ng, nb, sw, sb, wg, wpa, wpb, wmix)
    return pl.pallas_call(
        _mixer_body,
        grid=(B, S // SEQ_TILE),
        in_specs=[pl.BlockSpec((1, SEQ_TILE, D), lambda b, j: (b, j, 0))]
                 + [_const_spec(c.shape) for c in consts],
        out_specs=pl.BlockSpec((1, SEQ_TILE, D), lambda b, j: (b, j, 0)),
        out_shape=jax.ShapeDtypeStruct((B, S, D), F32),
        scratch_shapes=[pltpu.VMEM((GLA_WIDTH_V, GLA_WIDTH_K), F32)],
        compiler_params=pltpu.CompilerParams(
            dimension_semantics=("arbitrary", "arbitrary"), vmem_limit_bytes=VMEM_LIMIT),
        name="mixer",
    )(x, *consts)


def _xattn_body(x_ref, k_ref, v_ref, gx_ref, wq_ref, wo_ref, gf_ref, wr_hi_ref, wr_lo_ref, br_ref,
                x2_ref, h3_ref, topi_ref, gate_ref, rank_ref, cnt_ref, carry_ref):
    first = (pl.program_id(0) == 0) & (pl.program_id(1) == 0)

    @pl.when(first)
    def _():
        carry_ref[...] = jnp.zeros_like(carry_ref)

    x = x_ref[0]
    ts = x.shape[0]
    h = _rmsnorm(x, gx_ref[...]).astype(BF16)
    q = _dot(h, wq_ref[...]).astype(BF16)
    km = k_ref[0]
    vm = v_ref[0]
    heads = []
    for hh in range(XATTN_HEADS):
        sl = slice(hh * XATTN_DH, (hh + 1) * XATTN_DH)
        s = _dot_nt(q[:, sl], km[:, sl]) * (XATTN_DH ** -0.5)
        s = s - jnp.max(s, axis=-1, keepdims=True)
        e = jnp.exp(s)
        p = e / jnp.sum(e, axis=-1, keepdims=True)
        heads.append(_dot(p.astype(BF16), vm[:, sl]).astype(BF16))
    o = jnp.concatenate(heads, axis=1)
    x2 = x + _dot(o, wo_ref[...])
    x2_ref[0] = x2

    h3 = _rmsnorm(x2, gf_ref[...])
    h3_ref[0] = h3

    h_hi, h_lo = _split_bf16(h3)
    logits = (_dot_nt(wr_hi_ref[...], h_hi) + _dot_nt(wr_hi_ref[...], h_lo)
              + _dot_nt(wr_lo_ref[...], h_hi)) + br_ref[...]

    e_iota = lax.broadcasted_iota(jnp.int32, (N_EXPERTS, ts), 0)
    work = logits
    vals, idxs, hots = [], [], []
    for _ in range(TOP_K):
        m = jnp.max(work, axis=0, keepdims=True)
        idx = jnp.min(jnp.where(work == m, e_iota, N_EXPERTS), axis=0, keepdims=True)
        hot = e_iota == idx
        vals.append(m)
        idxs.append(idx)
        hots.append(hot)
        work = jnp.where(hot, -jnp.inf, work)
    exps = [jnp.exp(vv - vals[0]) for vv in vals]
    denom = exps[0] + exps[1] + exps[2] + exps[3]
    gate_ref[...] = jnp.concatenate([ee / denom for ee in exps], axis=0)
    topi_ref[...] = jnp.concatenate(idxs, axis=0)

    multi = jnp.where(hots[0] | hots[1] | hots[2] | hots[3], 1.0, 0.0)
    srow = lax.broadcasted_iota(jnp.int32, (ts, ts), 0)
    scol = lax.broadcasted_iota(jnp.int32, (ts, ts), 1)
    strict = jnp.where(srow < scol, 1.0, 0.0).astype(BF16)
    before = _dot(multi.astype(BF16), strict) + carry_ref[...]
    ranks = [jnp.sum(jnp.where(hot, before, 0.0), axis=0, keepdims=True) for hot in hots]
    rank_ref[...] = jnp.concatenate(ranks, axis=0).astype(jnp.int32)
    carry = carry_ref[...] + jnp.sum(multi, axis=1, keepdims=True)
    carry_ref[...] = carry
    cnt_ref[...] = carry.astype(jnp.int32)


def _xattn(x1, kmem, vmem, gx, wq, wo, gf, wr_hi, wr_lo, br):
    B, S, D = x1.shape
    T = B * S
    nj = S // SEQ_TILE
    tok_spec = pl.BlockSpec((1, SEQ_TILE, D), lambda b, j: (b, j, 0))
    mem_spec = pl.BlockSpec((1, MEM_LEN, D), lambda b, j: (b, 0, 0))
    lane_spec = pl.BlockSpec((TOP_K, SEQ_TILE), lambda b, j: (0, b * nj + j))
    consts = (gx, wq, wo, gf, wr_hi, wr_lo, br)
    return pl.pallas_call(
        _xattn_body,
        grid=(B, nj),
        in_specs=[tok_spec, mem_spec, mem_spec] + [_const_spec(c.shape) for c in consts],
        out_specs=[tok_spec, tok_spec, lane_spec, lane_spec, lane_spec,
                   pl.BlockSpec((N_EXPERTS, 1), lambda b, j: (0, 0))],
        out_shape=[
            jax.ShapeDtypeStruct((B, S, D), F32),
            jax.ShapeDtypeStruct((B, S, D), F32),
            jax.ShapeDtypeStruct((TOP_K, T), jnp.int32),
            jax.ShapeDtypeStruct((TOP_K, T), F32),
            jax.ShapeDtypeStruct((TOP_K, T), jnp.int32),
            jax.ShapeDtypeStruct((N_EXPERTS, 1), jnp.int32),
        ],
        scratch_shapes=[pltpu.VMEM((N_EXPERTS, 1), F32)],
        compiler_params=pltpu.CompilerParams(
            dimension_semantics=("arbitrary", "arbitrary"), vmem_limit_bytes=VMEM_LIMIT),
        name="xattn_router",
    )(x1, kmem, vmem, *consts)


ROW_GROUP = 8


def _dispatch_body(zs_ref, nu_ref, dest_ref, h_ref, xb_ref, zbuf_ref, zsem, sem):
    @pl.when(pl.program_id(0) == 0)
    def _():
        zbuf_ref[...] = jnp.zeros_like(zbuf_ref)
        n_blocks = xb_ref.shape[0] // ROW_BLOCK

        def zero_copy(start):
            start = pl.multiple_of(start, ROW_BLOCK)
            return pltpu.make_async_copy(zbuf_ref, xb_ref.at[pl.ds(start, ROW_BLOCK)], zsem)

        for e in range(N_EXPERTS):
            zero_copy(zs_ref[e]).start()
        for e in range(N_EXPERTS):
            zero_copy(zs_ref[e]).wait()

        def tail(blk, c):
            cp = zero_copy(blk * ROW_BLOCK)
            cp.start()
            cp.wait()
            return c

        lax.fori_loop(nu_ref[0], n_blocks, tail, 0)

    def start(g, c):
        base = pl.multiple_of(g * ROW_GROUP, ROW_GROUP)
        tile = h_ref.at[pl.ds(base, ROW_GROUP)]
        for j in range(ROW_GROUP):
            for kk in range(TOP_K):
                d = dest_ref[0, 0, kk * DISPATCH_TILE + base + j]
                pltpu.make_async_copy(tile.at[pl.ds(j, 1)], xb_ref.at[pl.ds(d, 1)], sem).start(
                    priority=kk % 2)
        return c

    lax.fori_loop(0, DISPATCH_TILE // ROW_GROUP, start, 0)
    for kk in range(TOP_K):
        pltpu.make_async_copy(h_ref, xb_ref.at[pl.ds(0, DISPATCH_TILE)], sem).wait()


def _dispatch(zero_start, n_used, dest_blocks, h3, n_slots):
    T, D = h3.shape
    return pl.pallas_call(
        _dispatch_body,
        grid_spec=pltpu.PrefetchScalarGridSpec(
            num_scalar_prefetch=2,
            grid=(T // DISPATCH_TILE,),
            in_specs=[
                pl.BlockSpec((1, 1, TOP_K * DISPATCH_TILE), lambda i, zs, nu: (i, 0, 0),
                             memory_space=pltpu.SMEM),
                pl.BlockSpec((DISPATCH_TILE, D), lambda i, zs, nu: (i, 0)),
            ],
            out_specs=pl.BlockSpec(memory_space=pl.ANY),
            scratch_shapes=[pltpu.VMEM((ROW_BLOCK, D), F32),
                            pltpu.SemaphoreType.DMA, pltpu.SemaphoreType.DMA],
        ),
        out_shape=jax.ShapeDtypeStruct((n_slots, D), F32),
        compiler_params=pltpu.CompilerParams(
            dimension_semantics=("arbitrary",), vmem_limit_bytes=VMEM_LIMIT),
        name="dispatch",
    )(zero_start, n_used, dest_blocks, h3)


def _expert_mlp(xb, w1b_ref, b1_ref, w2b_ref, b2_ref):
    hcat = _dot(xb.astype(BF16), w1b_ref[...]) + b1_ref[0]
    glu = jnp.minimum(hcat[:, :D_FF], SWIGLU_LIMIT)
    lin = jnp.clip(hcat[:, D_FF:], -SWIGLU_LIMIT, SWIGLU_LIMIT)
    act = (lin + 1.0) * glu * _sigmoid(SWIGLU_ALPHA * glu)
    return _dot(act.astype(BF16), w2b_ref[...]) + b2_ref[0]


def _experts_body(blk_ref, pair_ref, se_ref, run_ref, nxt_ref, ns_ref, nu_ref,
                  x_ref, w1_hbm, b1_ref, w2_hbm, b2_ref,
                  y_hbm, w1b_ref, w2b_ref, wf1_ref, wf2_ref, ybuf_ref, zbuf_ref, sems, wsems, zsem):
    s = pl.program_id(0)
    n_steps = ns_ref[0]
    last = pl.num_programs(0) - 1
    total_blocks = y_hbm.shape[0] // ROW_BLOCK

    def tail_copy(blk):
        row0 = pl.multiple_of(blk * ROW_BLOCK, ROW_BLOCK)
        return pltpu.make_async_copy(zbuf_ref, y_hbm.at[pl.ds(row0, ROW_BLOCK)], zsem)

    @pl.when(s == 0)
    def _():
        zbuf_ref[...] = jnp.zeros_like(zbuf_ref)

        def go(blk, c):
            tail_copy(blk).start()
            return c

        lax.fori_loop(nu_ref[0], total_blocks, go, 0)

    @pl.when(s == last)
    def _():
        def done(blk, c):
            tail_copy(blk).wait()
            return c

        lax.fori_loop(nu_ref[0], total_blocks, done, 0)

    def out_copy(step, half):
        slot = step % 2
        row0 = pl.multiple_of((blk_ref[step] + half) * ROW_BLOCK, ROW_BLOCK)
        return pltpu.make_async_copy(
            ybuf_ref.at[slot, pl.ds(half * ROW_BLOCK, ROW_BLOCK)],
            y_hbm.at[pl.ds(row0, ROW_BLOCK)], sems.at[slot])

    def wait_step(step):
        out_copy(step, 0).wait()

        @pl.when(pair_ref[step] == 1)
        def _():
            out_copy(step, 1).wait()

    def weight_copies(e, wslot):
        return (pltpu.make_async_copy(w1_hbm.at[e], wf1_ref.at[wslot], wsems.at[wslot]),
                pltpu.make_async_copy(w2_hbm.at[e], wf2_ref.at[wslot], wsems.at[wslot]))

    @pl.when(s < n_steps)
    def _():
        prev = se_ref[jnp.maximum(s - 1, 0)]
        wslot = run_ref[s] % 2

        @pl.when(s == 0)
        def _():
            for cp in weight_copies(se_ref[0], 0):
                cp.start()

        @pl.when((s == 0) | (se_ref[s] != prev))
        def _():
            for cp in weight_copies(se_ref[s], wslot):
                cp.wait()
            w1b_ref[...] = wf1_ref[wslot].astype(BF16)
            w2b_ref[...] = wf2_ref[wslot].astype(BF16)

            @pl.when(nxt_ref[s] < N_EXPERTS)
            def _():
                for cp in weight_copies(nxt_ref[s], 1 - wslot):
                    cp.start()

        slot = s % 2

        @pl.when(pair_ref[s] == 1)
        def _():
            ybuf_ref[slot] = _expert_mlp(x_ref[...], w1b_ref, b1_ref, w2b_ref, b2_ref)

        @pl.when(pair_ref[s] == 0)
        def _():
            ybuf_ref[slot, :ROW_BLOCK] = _expert_mlp(x_ref[:ROW_BLOCK], w1b_ref, b1_ref, w2b_ref, b2_ref)

    @pl.when((s >= 1) & (s - 1 < n_steps))
    def _():
        wait_step(s - 1)

    @pl.when(s < n_steps)
    def _():
        out_copy(s, 0).start()

        @pl.when(pair_ref[s] == 1)
        def _():
            out_copy(s, 1).start()

        @pl.when(s == last)
        def _():
            wait_step(s)


def _experts(step_blk, step_pair, step_e, step_run, step_next, n_steps, n_used, xb, w1, b1, w2, b2):
    n_rows, D = xb.shape

    def x_map(s, blk, pair, se, run, nxt, ns, nu):
        return (blk[s] * ROW_BLOCK, 0)

    def b_map(s, blk, pair, se, run, nxt, ns, nu):
        return (se[s], 0, 0)

    return pl.pallas_call(
        _experts_body,
        grid_spec=pltpu.PrefetchScalarGridSpec(
            num_scalar_prefetch=7,
            grid=(step_blk.shape[0],),
            in_specs=[
                pl.BlockSpec((pl.Element(2 * ROW_BLOCK), pl.Element(D)), x_map),
                pl.BlockSpec(memory_space=pl.ANY),
                pl.BlockSpec((1, 1, 2 * D_FF), b_map),
                pl.BlockSpec(memory_space=pl.ANY),
                pl.BlockSpec((1, 1, D), b_map),
            ],
            out_specs=pl.BlockSpec(memory_space=pl.ANY),
            scratch_shapes=[pltpu.VMEM((D, 2 * D_FF), BF16), pltpu.VMEM((D_FF, D), BF16),
                            pltpu.VMEM((2, D, 2 * D_FF), F32), pltpu.VMEM((2, D_FF, D), F32),
                            pltpu.VMEM((2, 2 * ROW_BLOCK, D), F32), pltpu.VMEM((ROW_BLOCK, D), F32),
                            pltpu.SemaphoreType.DMA((2,)), pltpu.SemaphoreType.DMA((2,)),
                            pltpu.SemaphoreType.DMA],
        ),
        out_shape=jax.ShapeDtypeStruct((n_rows, D), F32),
        compiler_params=pltpu.CompilerParams(
            dimension_semantics=("arbitrary",), vmem_limit_bytes=VMEM_LIMIT),
        name="experts",
    )(step_blk, step_pair, step_e, step_run, step_next, n_steps, n_used, xb, w1, b1, w2, b2)


def _combine_body(dest_ref, x_ref, gate_ref, g_ref, yb_ref, o_ref, rows_ref, sem):
    def start(g, c):
        base = pl.multiple_of(g * ROW_GROUP, ROW_GROUP)
        for kk in range(TOP_K):
            tile = rows_ref.at[kk, pl.ds(base, ROW_GROUP)]
            for j in range(ROW_GROUP):
                d = dest_ref[0, 0, kk * TOK_TILE + base + j]
                pltpu.make_async_copy(yb_ref.at[pl.ds(d, 1)], tile.at[pl.ds(j, 1)], sem).start(
                    priority=kk % 2)
        return c

    lax.fori_loop(0, TOK_TILE // ROW_GROUP, start, 0)
    for kk in range(TOP_K):
        pltpu.make_async_copy(yb_ref.at[pl.ds(0, TOK_TILE)], rows_ref.at[kk], sem).wait()

    gates = jnp.transpose(gate_ref[...])
    y = gates[:, 0:1] * rows_ref[0]
    for kk in range(1, TOP_K):
        y = y + gates[:, kk:kk + 1] * rows_ref[kk]
    o_ref[...] = _rmsnorm(x_ref[...] + y, g_ref[...])


def _combine(dest_blocks, x2, gates_t, g, yb):
    T, D = x2.shape
    return pl.pallas_call(
        _combine_body,
        grid=(T // TOK_TILE,),
        in_specs=[
            pl.BlockSpec((1, 1, TOP_K * TOK_TILE), lambda i: (i, 0, 0), memory_space=pltpu.SMEM),
            pl.BlockSpec((TOK_TILE, D), lambda i: (i, 0)),
            pl.BlockSpec((TOP_K, TOK_TILE), lambda i: (0, i)),
            _const_spec((1, D)),
            pl.BlockSpec(memory_space=pl.ANY),
        ],
        out_specs=pl.BlockSpec((TOK_TILE, D), lambda i: (i, 0)),
        out_shape=jax.ShapeDtypeStruct((T, D), F32),
        scratch_shapes=[pltpu.VMEM((TOP_K, TOK_TILE, D), F32), pltpu.SemaphoreType.DMA],
        compiler_params=pltpu.CompilerParams(
            dimension_semantics=("arbitrary",), vmem_limit_bytes=VMEM_LIMIT),
        name="combine",
    )(dest_blocks, x2, gates_t, g, yb)


def _row(v):
    return v.reshape(1, -1).astype(F32)


def _layer(x, kv, p):
    B, S, D = x.shape
    T = B * S
    w_in = p["w_in"]
    wa = jnp.zeros((D, GLA_RANK_PAD), F32).at[:, :GLA_GATE_RANK].set(w_in[:, OFF_A:OFF_U])
    wa2 = jnp.zeros((GLA_RANK_PAD, GLA_WIDTH_K), F32).at[:GLA_GATE_RANK].set(p["gla_w_a2"])
    x1 = _mixer(
        x, _row(p["norm_mix_g"]),
        w_in[:, OFF_Q:OFF_A].astype(BF16), wa.astype(BF16), wa2.astype(BF16),
        _row(p["gla_b_a"]), _row(p["gla_out_g"]),
        w_in[:, OFF_U:OFF_SV].astype(BF16), w_in[:, OFF_SV:OFF_G].astype(BF16),
        _row(p["sgu_norm_g"]), _row(p["sgu_norm_b"]), p["sgu_w"].astype(F32), p["sgu_b"].T.astype(F32),
        w_in[:, OFF_G:].astype(BF16), p["w_proj_a"].astype(BF16), p["w_proj_b"].astype(BF16),
        p["w_mix_out"].astype(BF16))

    kmem, vmem = kv
    wr_t = p["w_router"].T.astype(F32)
    wr_hi = wr_t.astype(BF16)
    wr_lo = (wr_t - wr_hi.astype(F32)).astype(BF16)
    x2, h3, topi, gates, rank, counts = _xattn(
        x1, kmem, vmem, _row(p["norm_x_g"]), p["w_xq"].astype(BF16), p["w_xo"].astype(BF16),
        _row(p["norm_ffn_g"]), wr_hi, wr_lo, p["b_router"].reshape(N_EXPERTS, 1).astype(F32))

    counts = counts.reshape(N_EXPERTS)
    blocks_e = (counts + ROW_BLOCK - 1) // ROW_BLOCK
    padded = blocks_e * ROW_BLOCK
    pad_end = jnp.cumsum(padded)
    pad_start = pad_end - padded
    n_blocks = (T * TOP_K) // ROW_BLOCK + N_EXPERTS
    n_slots = (n_blocks + 1) * ROW_BLOCK
    n_used = (pad_end[-1] // ROW_BLOCK).astype(jnp.int32)
    steps_e = (blocks_e + 1) // 2
    step_end = jnp.cumsum(steps_e)
    n_steps = step_end[-1].astype(jnp.int32)
    max_steps = (n_blocks + N_EXPERTS) // 2
    sidx = jnp.minimum(jnp.arange(max_steps, dtype=jnp.int32), n_steps - 1)
    step_e = jnp.minimum(jnp.sum(step_end[None, :] <= sidx[:, None], axis=1), N_EXPERTS - 1).astype(jnp.int32)
    of_step = step_e[:, None] == jnp.arange(N_EXPERTS, dtype=jnp.int32)[None, :]

    def per_step(v):
        return jnp.sum(jnp.where(of_step, v[None, :], 0), axis=1)

    local = sidx - per_step(step_end - steps_e)
    step_blk = (per_step(pad_start // ROW_BLOCK) + 2 * local).astype(jnp.int32)
    step_pair = (2 * local + 1 < per_step(blocks_e)).astype(jnp.int32)
    ids = jnp.arange(N_EXPERTS, dtype=jnp.int32)
    has_rows = steps_e > 0
    run_e = jnp.cumsum(has_rows.astype(jnp.int32)) - 1
    later = (ids[None, :] > ids[:, None]) & has_rows[None, :]
    next_e = jnp.min(jnp.where(later, ids[None, :], N_EXPERTS), axis=1)
    step_run = per_step(run_e).astype(jnp.int32)
    step_next = per_step(next_e).astype(jnp.int32)
    e_ids = jnp.arange(N_EXPERTS, dtype=jnp.int32)[:, None, None]
    dest = jnp.sum(jnp.where(topi[None] == e_ids, pad_start[:, None, None], 0), axis=0) + rank
    def blocked(tile):
        return (dest.reshape(TOP_K, T // tile, tile).transpose(1, 0, 2)
                .reshape(T // tile, 1, TOP_K * tile).astype(jnp.int32))

    dest_blocks = blocked(TOK_TILE)
    zero_start = jnp.maximum(pad_end - ROW_BLOCK, 0).astype(jnp.int32)

    n_used = n_used.reshape(1)
    xb = _dispatch(zero_start, n_used, blocked(DISPATCH_TILE), h3.reshape(T, D), n_slots)
    yb = _experts(step_blk, step_pair, step_e, step_run, step_next, n_steps.reshape(1), n_used, xb,
                  p["w_e1"], p["b_e1"].reshape(N_EXPERTS, 1, 2 * D_FF),
                  p["w_e2"], p["b_e2"].reshape(N_EXPERTS, 1, D))
    return dest_blocks, x2.reshape(T, D), gates, yb


def kernel(x, mem, norm_mix_g, w_in, gla_w_a2, gla_b_a, gla_out_g, sgu_norm_g, sgu_norm_b, sgu_w, sgu_b, w_proj_a, w_proj_b, w_mix_out, norm_x_g, norm_mem_g, w_xq, w_xk, w_xv, w_xo, norm_ffn_g, w_router, b_router, w_e1, b_e1, w_e2, b_e2, norm_final_g):
    B, S, D = x.shape
    depth = w_in.shape[0]
    assert depth == 1, "the final norm is fused into the last layer's combine step"
    stacked = dict(norm_mix_g=norm_mix_g, w_in=w_in, gla_w_a2=gla_w_a2, gla_b_a=gla_b_a,
                   gla_out_g=gla_out_g, sgu_norm_g=sgu_norm_g, sgu_norm_b=sgu_norm_b, sgu_w=sgu_w,
                   sgu_b=sgu_b, w_proj_a=w_proj_a, w_proj_b=w_proj_b, w_mix_out=w_mix_out,
                   norm_x_g=norm_x_g, w_xq=w_xq, w_xo=w_xo, norm_ffn_g=norm_ffn_g,
                   w_router=w_router, b_router=b_router, w_e1=w_e1, b_e1=b_e1, w_e2=w_e2, b_e2=b_e2)
    p = {name: v[0] for name, v in stacked.items()}
    kv = _xattn_kv(mem, _row(norm_mem_g[0]), w_xk[0].astype(BF16), w_xv[0].astype(BF16))
    dest_blocks, x2, gates_t, yb = _layer(x, kv, p)
    out = _combine(dest_blocks, x2, gates_t, _row(norm_final_g), yb)
    return out.reshape(B, S, D)
```

```python
import functools

import jax
import jax.numpy as jnp
from jax import lax
from jax.experimental import pallas as pl
from jax.experimental.pallas import tpu as pltpu

F32 = jnp.float32
BF16 = jnp.bfloat16

D_MODEL = 1024
MEM_LEN = 256
EPS = 1e-5

GLA_HEADS = 4
GLA_DV = 128
GLA_DK = 64
GLA_WIDTH_K = GLA_HEADS * GLA_DK
GLA_WIDTH_V = GLA_HEADS * GLA_DV
GLA_GATE_RANK = 16
GLA_RANK_PAD = 128
GLA_TAU = 16.0
GLA_CHUNK = 64

SGU_WIDTH = 512
SGU_GROUPS = 4
SGU_GROUP_DIM = SGU_WIDTH // SGU_GROUPS
SGU_CHUNK = 128

OFF_Q = 0
OFF_K = OFF_Q + GLA_WIDTH_K
OFF_V = OFF_K + GLA_WIDTH_K
OFF_R = OFF_V + GLA_WIDTH_V
OFF_A = OFF_R + GLA_WIDTH_V
OFF_U = OFF_A + GLA_GATE_RANK
OFF_SV = OFF_U + SGU_WIDTH
OFF_G = OFF_SV + SGU_WIDTH

XATTN_HEADS = 4
XATTN_DH = D_MODEL // XATTN_HEADS

N_EXPERTS = 32
TOP_K = 4
D_FF = D_MODEL
SWIGLU_ALPHA = 1.702
SWIGLU_LIMIT = 7.0

SEQ_TILE = 512
ROW_BLOCK = 256
TOK_TILE = 512
DISPATCH_TILE = 1024
VMEM_LIMIT = 56 * 1024 * 1024


def _rmsnorm(x, g):
    return x * lax.rsqrt(jnp.mean(x * x, axis=-1, keepdims=True) + EPS) * g


def _sigmoid(x):
    return 1.0 / (1.0 + jnp.exp(-x))


def _log_sigmoid(x):
    return jnp.minimum(x, 0.0) - jnp.log1p(jnp.exp(-jnp.abs(x)))


def _gelu(x):
    return 0.5 * x * (1.0 + lax.erf(x * (2.0 ** -0.5)))


def _dot(a, b):
    return jnp.dot(a, b, preferred_element_type=F32)


def _dot_nt(a, b):
    return lax.dot_general(a, b, (((1,), (1,)), ((), ())), preferred_element_type=F32)


def _dot_tn(a, b):
    return lax.dot_general(a, b, (((0,), (0,)), ((), ())), preferred_element_type=F32)


def _split_bf16(x):
    hi = x.astype(BF16)
    lo = (x - hi.astype(F32)).astype(BF16)
    return hi, lo


def _const_spec(shape):
    zeros = (0,) * len(shape)
    return pl.BlockSpec(shape, lambda *_: zeros, pipeline_mode=pl.Buffered(1))


def _xattn_kv_body(mem_ref, g_ref, wk_ref, wv_ref, k_ref, v_ref):
    m = _rmsnorm(mem_ref[0], g_ref[...]).astype(BF16)
    k_ref[0] = _dot(m, wk_ref[...]).astype(BF16)
    v_ref[0] = _dot(m, wv_ref[...]).astype(BF16)


def _xattn_kv(mem, g, wk, wv):
    B = mem.shape[0]
    return pl.pallas_call(
        _xattn_kv_body,
        grid=(B,),
        in_specs=[
            pl.BlockSpec((1, MEM_LEN, D_MODEL), lambda b: (b, 0, 0)),
            _const_spec((1, D_MODEL)),
            _const_spec((D_MODEL, D_MODEL)),
            _const_spec((D_MODEL, D_MODEL)),
        ],
        out_specs=[
            pl.BlockSpec((1, MEM_LEN, D_MODEL), lambda b: (b, 0, 0)),
            pl.BlockSpec((1, MEM_LEN, D_MODEL), lambda b: (b, 0, 0)),
        ],
        out_shape=[jax.ShapeDtypeStruct((B, MEM_LEN, D_MODEL), BF16)] * 2,
        compiler_params=pltpu.CompilerParams(
            dimension_semantics=("arbitrary",), vmem_limit_bytes=VMEM_LIMIT),
        name="xattn_kv",
    )(mem, g, wk, wv)


def _gla_branch(h, wqkvr_ref, wa_ref, wa2_ref, ba_ref, outg_ref, st_ref):
    ts = h.shape[0]
    n_chunks = ts // GLA_CHUNK
    p = _dot(h, wqkvr_ref[...])
    q = p[:, OFF_Q:OFF_K] * (GLA_DK ** -0.5)
    k = p[:, OFF_K:OFF_V]
    v = p[:, OFF_V:OFF_R].astype(BF16)
    r = p[:, OFF_R:OFF_A]

    a_low = _dot(h, wa_ref[...]).astype(BF16)
    log_a = _log_sigmoid(_dot(a_low, wa2_ref[...]) + ba_ref[...]) * (1.0 / GLA_TAU)

    row = lax.broadcasted_iota(jnp.int32, (ts, ts), 0)
    col = lax.broadcasted_iota(jnp.int32, (ts, ts), 1)
    tri = jnp.where((col <= row) & ((col >> 6) == (row >> 6)), 1.0, 0.0).astype(BF16)
    la_hi, la_lo = _split_bf16(log_a)
    cum = _dot(tri, la_hi) + _dot(tri, la_lo)

    q_in = (q * jnp.exp(cum)).astype(BF16)
    k_in = (k * jnp.exp(-cum)).astype(BF16)

    lane_k = lax.broadcasted_iota(jnp.int32, (1, GLA_WIDTH_K), 1)
    head_masks = [jnp.where((lane_k >> 6) == hh, 1.0, 0.0).astype(BF16) for hh in range(GLA_HEADS)]
    r4 = lax.broadcasted_iota(jnp.int32, (GLA_HEADS * GLA_CHUNK, GLA_CHUNK), 0)
    c4 = lax.broadcasted_iota(jnp.int32, (GLA_HEADS * GLA_CHUNK, GLA_CHUNK), 1)
    causal4 = c4 <= (r4 & (GLA_CHUNK - 1))
    sr = lax.broadcasted_iota(jnp.int32, (GLA_WIDTH_V, GLA_WIDTH_K), 0)
    sc = lax.broadcasted_iota(jnp.int32, (GLA_WIDTH_V, GLA_WIDTH_K), 1)
    state_mask = jnp.where((sr >> 7) == (sc >> 6), 1.0, 0.0)

    state = st_ref[...]
    outs = []
    for n in range(n_chunks):
        lo, hi = n * GLA_CHUNK, (n + 1) * GLA_CHUNK
        cum_c = cum[lo:hi]
        last = cum[hi - 1:hi]
        q_c = q_in[lo:hi]
        k_c = k_in[lo:hi]
        v_c = v[lo:hi]
        k_out = (k[lo:hi] * jnp.exp(last - cum_c)).astype(BF16)
        q4 = jnp.concatenate([q_c * head_masks[hh] for hh in range(GLA_HEADS)], axis=0)
        att4 = jnp.where(causal4, _dot_nt(q4, k_c), 0.0).astype(BF16)
        oi4 = _dot(att4, v_c)
        o_intra = jnp.concatenate(
            [oi4[hh * GLA_CHUNK:(hh + 1) * GLA_CHUNK, hh * GLA_DV:(hh + 1) * GLA_DV]
             for hh in range(GLA_HEADS)], axis=1)
        o_inter = _dot_nt(q_c, state.astype(BF16))
        outs.append(o_intra + o_inter)
        state = state * jnp.exp(last) + _dot_tn(v_c, k_out) * state_mask
    st_ref[...] = state

    o = jnp.concatenate(outs, axis=0)
    outg = outg_ref[...]
    normed = []
    for hh in range(GLA_HEADS):
        sl = slice(hh * GLA_DV, (hh + 1) * GLA_DV)
        normed.append(_rmsnorm(o[:, sl], outg[:, sl]))
    o = jnp.concatenate(normed, axis=1)
    return o * (r * _sigmoid(r))


def _sgu_branch(h, wu_ref, wsv_ref, ng_ref, nb_ref, sw_ref, sb_ref):
    ts = h.shape[0]
    n_chunks = ts // SGU_CHUNK
    u = _gelu(_dot(h, wu_ref[...]))
    v = _gelu(_dot(h, wsv_ref[...]))
    mu = jnp.mean(v, axis=-1, keepdims=True)
    vc = v - mu
    var = jnp.mean(vc * vc, axis=-1, keepdims=True)
    v = (vc * lax.rsqrt(var + EPS) * ng_ref[...] + nb_ref[...]).astype(BF16)

    row = lax.broadcasted_iota(jnp.int32, (SGU_CHUNK, SGU_CHUNK), 0)
    col = lax.broadcasted_iota(jnp.int32, (SGU_CHUNK, SGU_CHUNK), 1)
    causal = col <= row
    sb = sb_ref[...]
    per_group = []
    for g in range(SGU_GROUPS):
        w = jnp.where(causal, sw_ref[g], 0.0).astype(BF16)
        gs = slice(g * SGU_GROUP_DIM, (g + 1) * SGU_GROUP_DIM)
        vcat = jnp.concatenate(
            [v[c * SGU_CHUNK:(c + 1) * SGU_CHUNK, gs] for c in range(n_chunks)], axis=1)
        per_group.append(_dot(w, vcat) + sb[:, g:g + 1])
    rows = []
    for c in range(n_chunks):
        cs = slice(c * SGU_GROUP_DIM, (c + 1) * SGU_GROUP_DIM)
        rows.append(jnp.concatenate([per_group[g][:, cs] for g in range(SGU_GROUPS)], axis=1))
    mixed = jnp.concatenate(rows, axis=0)
    return u * mixed


def _mixer_body(x_ref, g_ref, wqkvr_ref, wa_ref, wa2_ref, ba_ref, outg_ref,
                wu_ref, wsv_ref, ng_ref, nb_ref, sw_ref, sb_ref,
                wg_ref, wpa_ref, wpb_ref, wmix_ref, o_ref, st_ref):
    @pl.when(pl.program_id(1) == 0)
    def _():
        st_ref[...] = jnp.zeros_like(st_ref)

    x = x_ref[0]
    h = _rmsnorm(x, g_ref[...]).astype(BF16)
    ya_in = _gla_branch(h, wqkvr_ref, wa_ref, wa2_ref, ba_ref, outg_ref, st_ref)
    y_a = _dot(ya_in.astype(BF16), wpa_ref[...])
    yb_in = _sgu_branch(h, wu_ref, wsv_ref, ng_ref, nb_ref, sw_ref, sb_ref)
    y_b = _dot(yb_in.astype(BF16), wpb_ref[...])
    gates = _sigmoid(_dot(h, wg_ref[...]))
    mix = gates[:, :D_MODEL] * y_a + gates[:, D_MODEL:] * y_b
    o_ref[0] = x + _dot(mix.astype(BF16), wmix_ref[...])


def _mixer(x, g, wqkvr, wa, wa2, ba, outg, wu, wsv, ng, nb, sw, sb, wg, wpa, wpb, wmix):
    B, S, D = x.shape
    consts = (g, wqkvr, wa, wa2, ba, outg, wu, wsv, ng, nb, sw, sb, wg, wpa, wpb, wmix)
    return pl.pallas_call(
        _mixer_body,
        grid=(B, S // SEQ_TILE),
        in_specs=[pl.BlockSpec((1, SEQ_TILE, D), lambda b, j: (b, j, 0))]
                 + [_const_spec(c.shape) for c in consts],
        out_specs=pl.BlockSpec((1, SEQ_TILE, D), lambda b, j: (b, j, 0)),
        out_shape=jax.ShapeDtypeStruct((B, S, D), F32),
        scratch_shapes=[pltpu.VMEM((GLA_WIDTH_V, GLA_WIDTH_K), F32)],
        compiler_params=pltpu.CompilerParams(
            dimension_semantics=("arbitrary", "arbitrary"), vmem_limit_bytes=VMEM_LIMIT),
        name="mixer",
    )(x, *consts)


def _xattn_body(x_ref, k_ref, v_ref, gx_ref, wq_ref, wo_ref, gf_ref, wr_hi_ref, wr_lo_ref, br_ref,
                x2_ref, h3_ref, topi_ref, gate_ref, rank_ref, cnt_ref, carry_ref):
    first = (pl.program_id(0) == 0) & (pl.program_id(1) == 0)

    @pl.when(first)
    def _():
        carry_ref[...] = jnp.zeros_like(carry_ref)

    x = x_ref[0]
    ts = x.shape[0]
    h = _rmsnorm(x, gx_ref[...]).astype(BF16)
    q = _dot(h, wq_ref[...]).astype(BF16)
    km = k_ref[0]
    vm = v_ref[0]
    heads = []
    for hh in range(XATTN_HEADS):
        sl = slice(hh * XATTN_DH, (hh + 1) * XATTN_DH)
        s = _dot_nt(q[:, sl], km[:, sl]) * (XATTN_DH ** -0.5)
        s = s - jnp.max(s, axis=-1, keepdims=True)
        e = jnp.exp(s)
        p = e / jnp.sum(e, axis=-1, keepdims=True)
        heads.append(_dot(p.astype(BF16), vm[:, sl]).astype(BF16))
    o = jnp.concatenate(heads, axis=1)
    x2 = x + _dot(o, wo_ref[...])
    x2_ref[0] = x2

    h3 = _rmsnorm(x2, gf_ref[...])
    h3_ref[0] = h3

    h_hi, h_lo = _split_bf16(h3)
    logits = (_dot_nt(wr_hi_ref[...], h_hi) + _dot_nt(wr_hi_ref[...], h_lo)
              + _dot_nt(wr_lo_ref[...], h_hi)) + br_ref[...]

    e_iota = lax.broadcasted_iota(jnp.int32, (N_EXPERTS, ts), 0)
    work = logits
    vals, idxs, hots = [], [], []
    for _ in range(TOP_K):
        m = jnp.max(work, axis=0, keepdims=True)
        idx = jnp.min(jnp.where(work == m, e_iota, N_EXPERTS), axis=0, keepdims=True)
        hot = e_iota == idx
        vals.append(m)
        idxs.append(idx)
        hots.append(hot)
        work = jnp.where(hot, -jnp.inf, work)
    exps = [jnp.exp(vv - vals[0]) for vv in vals]
    denom = exps[0] + exps[1] + exps[2] + exps[3]
    gate_ref[...] = jnp.concatenate([ee / denom for ee in exps], axis=0)
    topi_ref[...] = jnp.concatenate(idxs, axis=0)

    multi = jnp.where(hots[0] | hots[1] | hots[2] | hots[3], 1.0, 0.0)
    srow = lax.broadcasted_iota(jnp.int32, (ts, ts), 0)
    scol = lax.broadcasted_iota(jnp.int32, (ts, ts), 1)
    strict = jnp.where(srow < scol, 1.0, 0.0).astype(BF16)
    before = _dot(multi.astype(BF16), strict) + carry_ref[...]
    ranks = [jnp.sum(jnp.where(hot, before, 0.0), axis=0, keepdims=True) for hot in hots]
    rank_ref[...] = jnp.concatenate(ranks, axis=0).astype(jnp.int32)
    carry = carry_ref[...] + jnp.sum(multi, axis=1, keepdims=True)
    carry_ref[...] = carry
    cnt_ref[...] = carry.astype(jnp.int32)


def _xattn(x1, kmem, vmem, gx, wq, wo, gf, wr_hi, wr_lo, br):
    B, S, D = x1.shape
    T = B * S
    nj = S // SEQ_TILE
    tok_spec = pl.BlockSpec((1, SEQ_TILE, D), lambda b, j: (b, j, 0))
    mem_spec = pl.BlockSpec((1, MEM_LEN, D), lambda b, j: (b, 0, 0))
    lane_spec = pl.BlockSpec((TOP_K, SEQ_TILE), lambda b, j: (0, b * nj + j))
    consts = (gx, wq, wo, gf, wr_hi, wr_lo, br)
    return pl.pallas_call(
        _xattn_body,
        grid=(B, nj),
        in_specs=[tok_spec, mem_spec, mem_spec] + [_const_spec(c.shape) for c in consts],
        out_specs=[tok_spec, tok_spec, lane_spec, lane_spec, lane_spec,
                   pl.BlockSpec((N_EXPERTS, 1), lambda b, j: (0, 0))],
        out_shape=[
            jax.ShapeDtypeStruct((B, S, D), F32),
            jax.ShapeDtypeStruct((B, S, D), F32),
            jax.ShapeDtypeStruct((TOP_K, T), jnp.int32),
            jax.ShapeDtypeStruct((TOP_K, T), F32),
            jax.ShapeDtypeStruct((TOP_K, T), jnp.int32),
            jax.ShapeDtypeStruct((N_EXPERTS, 1), jnp.int32),
        ],
        scratch_shapes=[pltpu.VMEM((N_EXPERTS, 1), F32)],
        compiler_params=pltpu.CompilerParams(
            dimension_semantics=("arbitrary", "arbitrary"), vmem_limit_bytes=VMEM_LIMIT),
        name="xattn_router",
    )(x1, kmem, vmem, *consts)


ROW_GROUP = 8


def _dispatch_body(zs_ref, nu_ref, dest_ref, h_ref, xb_ref, zbuf_ref, zsem, tsem, sem):
    n_blocks = xb_ref.shape[0] // ROW_BLOCK

    def zero_copy(start, zs):
        start = pl.multiple_of(start, ROW_BLOCK)
        return pltpu.make_async_copy(zbuf_ref, xb_ref.at[pl.ds(start, ROW_BLOCK)], zs)

    @pl.when(pl.program_id(0) == 0)
    def _():
        zbuf_ref[...] = jnp.zeros_like(zbuf_ref)
        for e in range(N_EXPERTS):
            zero_copy(zs_ref[e], zsem).start()

        def tail_start(blk, c):
            zero_copy(blk * ROW_BLOCK, tsem).start()
            return c

        lax.fori_loop(nu_ref[0], n_blocks, tail_start, 0)
        for e in range(N_EXPERTS):
            zero_copy(zs_ref[e], zsem).wait()

    def start(g, c):
        base = pl.multiple_of(g * ROW_GROUP, ROW_GROUP)
        tile = h_ref.at[pl.ds(base, ROW_GROUP)]
        for j in range(ROW_GROUP):
            for kk in range(TOP_K):
                d = dest_ref[0, 0, kk * DISPATCH_TILE + base + j]
                pltpu.make_async_copy(tile.at[pl.ds(j, 1)], xb_ref.at[pl.ds(d, 1)], sem).start(
                    priority=kk % 2)
        return c

    lax.fori_loop(0, DISPATCH_TILE // ROW_GROUP, start, 0)
    for kk in range(TOP_K):
        pltpu.make_async_copy(h_ref, xb_ref.at[pl.ds(0, DISPATCH_TILE)], sem).wait()

    @pl.when(pl.program_id(0) == 0)
    def _():
        def tail_wait(blk, c):
            zero_copy(blk * ROW_BLOCK, tsem).wait()
            return c

        lax.fori_loop(nu_ref[0], n_blocks, tail_wait, 0)


def _dispatch(zero_start, n_used, dest_blocks, h3, n_slots):
    T, D = h3.shape
    return pl.pallas_call(
        _dispatch_body,
        grid_spec=pltpu.PrefetchScalarGridSpec(
            num_scalar_prefetch=2,
            grid=(T // DISPATCH_TILE,),
            in_specs=[
                pl.BlockSpec((1, 1, TOP_K * DISPATCH_TILE), lambda i, zs, nu: (i, 0, 0),
                             memory_space=pltpu.SMEM),
                pl.BlockSpec((DISPATCH_TILE, D), lambda i, zs, nu: (i, 0)),
            ],
            out_specs=pl.BlockSpec(memory_space=pl.ANY),
            scratch_shapes=[pltpu.VMEM((ROW_BLOCK, D), F32), pltpu.SemaphoreType.DMA,
                            pltpu.SemaphoreType.DMA, pltpu.SemaphoreType.DMA],
        ),
        out_shape=jax.ShapeDtypeStruct((n_slots, D), F32),
        compiler_params=pltpu.CompilerParams(
            dimension_semantics=("arbitrary",), vmem_limit_bytes=VMEM_LIMIT),
        name="dispatch",
    )(zero_start, n_used, dest_blocks, h3)


def _expert_mlp(xb, w1b_ref, b1_ref, w2b_ref, b2_ref):
    hcat = _dot(xb.astype(BF16), w1b_ref[...]) + b1_ref[0]
    glu = jnp.minimum(hcat[:, :D_FF], SWIGLU_LIMIT)
    lin = jnp.clip(hcat[:, D_FF:], -SWIGLU_LIMIT, SWIGLU_LIMIT)
    act = (lin + 1.0) * glu * _sigmoid(SWIGLU_ALPHA * glu)
    return _dot(act.astype(BF16), w2b_ref[...]) + b2_ref[0]


def _experts_body(blk_ref, pair_ref, se_ref, run_ref, nxt_ref, ns_ref, nu_ref,
                  x_ref, w1_hbm, b1_ref, w2_hbm, b2_ref,
                  y_hbm, w1b_ref, w2b_ref, wf1_ref, wf2_ref, ybuf_ref, zbuf_ref, sems, wsems, zsem):
    s = pl.program_id(0)
    n_steps = ns_ref[0]
    last = pl.num_programs(0) - 1
    total_blocks = y_hbm.shape[0] // ROW_BLOCK

    def tail_copy(blk):
        row0 = pl.multiple_of(blk * ROW_BLOCK, ROW_BLOCK)
        return pltpu.make_async_copy(zbuf_ref, y_hbm.at[pl.ds(row0, ROW_BLOCK)], zsem)

    @pl.when(s == 0)
    def _():
        zbuf_ref[...] = jnp.zeros_like(zbuf_ref)

        def go(blk, c):
            tail_copy(blk).start()
            return c

        lax.fori_loop(nu_ref[0], total_blocks, go, 0)

    @pl.when(s == last)
    def _():
        def done(blk, c):
            tail_copy(blk).wait()
            return c

        lax.fori_loop(nu_ref[0], total_blocks, done, 0)

    def out_copy(step, half):
        slot = step % 2
        row0 = pl.multiple_of((blk_ref[step] + half) * ROW_BLOCK, ROW_BLOCK)
        return pltpu.make_async_copy(
            ybuf_ref.at[slot, pl.ds(half * ROW_BLOCK, ROW_BLOCK)],
            y_hbm.at[pl.ds(row0, ROW_BLOCK)], sems.at[slot])

    def wait_step(step):
        out_copy(step, 0).wait()

        @pl.when(pair_ref[step] == 1)
        def _():
            out_copy(step, 1).wait()

    def weight_copies(e, wslot):
        return (pltpu.make_async_copy(w1_hbm.at[e], wf1_ref.at[wslot], wsems.at[wslot]),
                pltpu.make_async_copy(w2_hbm.at[e], wf2_ref.at[wslot], wsems.at[wslot]))

    @pl.when(s < n_steps)
    def _():
        prev = se_ref[jnp.maximum(s - 1, 0)]
        wslot = run_ref[s] % 2

        @pl.when(s == 0)
        def _():
            for cp in weight_copies(se_ref[0], 0):
                cp.start()

        @pl.when((s == 0) | (se_ref[s] != prev))
        def _():
            for cp in weight_copies(se_ref[s], wslot):
                cp.wait()
            w1b_ref[...] = wf1_ref[wslot].astype(BF16)
            w2b_ref[...] = wf2_ref[wslot].astype(BF16)

            @pl.when(nxt_ref[s] < N_EXPERTS)
            def _():
                for cp in weight_copies(nxt_ref[s], 1 - wslot):
                    cp.start()

        slot = s % 2

        @pl.when(pair_ref[s] == 1)
        def _():
            ybuf_ref[slot] = _expert_mlp(x_ref[...], w1b_ref, b1_ref, w2b_ref, b2_ref)

        @pl.when(pair_ref[s] == 0)
        def _():
            ybuf_ref[slot, :ROW_BLOCK] = _expert_mlp(x_ref[:ROW_BLOCK], w1b_ref, b1_ref, w2b_ref, b2_ref)

    @pl.when((s >= 1) & (s - 1 < n_steps))
    def _():
        wait_step(s - 1)

    @pl.when(s < n_steps)
    def _():
        out_copy(s, 0).start()

        @pl.when(pair_ref[s] == 1)
        def _():
            out_copy(s, 1).start()

        @pl.when(s == last)
        def _():
            wait_step(s)


def _experts(step_blk, step_pair, step_e, step_run, step_next, n_steps, n_used, xb, w1, b1, w2, b2):
    n_rows, D = xb.shape

    def x_map(s, blk, pair, se, run, nxt, ns, nu):
        return (blk[s] * ROW_BLOCK, 0)

    def b_map(s, blk, pair, se, run, nxt, ns, nu):
        return (se[s], 0, 0)

    return pl.pallas_call(
        _experts_body,
        grid_spec=pltpu.PrefetchScalarGridSpec(
            num_scalar_prefetch=7,
            grid=(step_blk.shape[0],),
            in_specs=[
                pl.BlockSpec((pl.Element(2 * ROW_BLOCK), pl.Element(D)), x_map),
                pl.BlockSpec(memory_space=pl.ANY),
                pl.BlockSpec((1, 1, 2 * D_FF), b_map),
                pl.BlockSpec(memory_space=pl.ANY),
                pl.BlockSpec((1, 1, D), b_map),
            ],
            out_specs=pl.BlockSpec(memory_space=pl.ANY),
            scratch_shapes=[pltpu.VMEM((D, 2 * D_FF), BF16), pltpu.VMEM((D_FF, D), BF16),
                            pltpu.VMEM((2, D, 2 * D_FF), F32), pltpu.VMEM((2, D_FF, D), F32),
                            pltpu.VMEM((2, 2 * ROW_BLOCK, D), F32), pltpu.VMEM((ROW_BLOCK, D), F32),
                            pltpu.SemaphoreType.DMA((2,)), pltpu.SemaphoreType.DMA((2,)),
                            pltpu.SemaphoreType.DMA],
        ),
        out_shape=jax.ShapeDtypeStruct((n_rows, D), F32),
        compiler_params=pltpu.CompilerParams(
            dimension_semantics=("arbitrary",), vmem_limit_bytes=VMEM_LIMIT),
        name="experts",
    )(step_blk, step_pair, step_e, step_run, step_next, n_steps, n_used, xb, w1, b1, w2, b2)


def _combine_body(dest_ref, next_ref, x_ref, gate_ref, g_ref, yb_ref, o_ref, rows_ref, sems):
    i = pl.program_id(0)
    last = pl.num_programs(0) - 1
    slot = i % 2
    n_groups = TOK_TILE // ROW_GROUP

    def start_group(idx_ref, sl, base):
        for kk in range(TOP_K):
            tile = rows_ref.at[sl, kk, pl.ds(base, ROW_GROUP)]
            for j in range(ROW_GROUP):
                d = idx_ref[0, 0, kk * TOK_TILE + base + j]
                pltpu.make_async_copy(yb_ref.at[pl.ds(d, 1)], tile.at[pl.ds(j, 1)],
                                      sems.at[sl]).start(priority=kk % 2)

    def reduce_group(base):
        gates = gate_ref[pl.ds(base, ROW_GROUP), :]
        y = gates[:, 0:1] * rows_ref[slot, 0, pl.ds(base, ROW_GROUP), :]
        for kk in range(1, TOP_K):
            y = y + gates[:, kk:kk + 1] * rows_ref[slot, kk, pl.ds(base, ROW_GROUP), :]
        o_ref[pl.ds(base, ROW_GROUP), :] = x_ref[pl.ds(base, ROW_GROUP), :] + y

    @pl.when(i == 0)
    def _():
        def first(g, c):
            start_group(dest_ref, 0, pl.multiple_of(g * ROW_GROUP, ROW_GROUP))
            return c

        lax.fori_loop(0, n_groups, first, 0)

    for kk in range(TOP_K):
        pltpu.make_async_copy(yb_ref.at[pl.ds(0, TOK_TILE)], rows_ref.at[slot, kk], sems.at[slot]).wait()

    @pl.when(i < last)
    def _():
        def both(g, c):
            base = pl.multiple_of(g * ROW_GROUP, ROW_GROUP)
            start_group(next_ref, 1 - slot, base)
            reduce_group(base)
            return c

        lax.fori_loop(0, n_groups, both, 0)

    @pl.when(i == last)
    def _():
        def only(g, c):
            reduce_group(pl.multiple_of(g * ROW_GROUP, ROW_GROUP))
            return c

        lax.fori_loop(0, n_groups, only, 0)

    o_ref[...] = _rmsnorm(o_ref[...], g_ref[...])


def _combine(dest_blocks, x2, gates_t, g, yb):
    T, D = x2.shape
    n_tiles = T // TOK_TILE
    return pl.pallas_call(
        _combine_body,
        grid=(n_tiles,),
        in_specs=[
            pl.BlockSpec((1, 1, TOP_K * TOK_TILE), lambda i: (i, 0, 0), memory_space=pltpu.SMEM),
            pl.BlockSpec((1, 1, TOP_K * TOK_TILE), lambda i: (jnp.minimum(i + 1, n_tiles - 1), 0, 0),
                         memory_space=pltpu.SMEM),
            pl.BlockSpec((TOK_TILE, D), lambda i: (i, 0)),
            pl.BlockSpec((TOK_TILE, TOP_K), lambda i: (i, 0)),
            _const_spec((1, D)),
            pl.BlockSpec(memory_space=pl.ANY),
        ],
        out_specs=pl.BlockSpec((TOK_TILE, D), lambda i: (i, 0)),
        out_shape=jax.ShapeDtypeStruct((T, D), F32),
        scratch_shapes=[pltpu.VMEM((2, TOP_K, TOK_TILE, D), F32), pltpu.SemaphoreType.DMA((2,))],
        compiler_params=pltpu.CompilerParams(
            dimension_semantics=("arbitrary",), vmem_limit_bytes=VMEM_LIMIT),
        name="combine",
    )(dest_blocks, dest_blocks, x2, gates_t, g, yb)


def _row(v):
    return v.reshape(1, -1).astype(F32)


def _layer(x, kv, p):
    B, S, D = x.shape
    T = B * S
    w_in = p["w_in"]
    wa = jnp.zeros((D, GLA_RANK_PAD), F32).at[:, :GLA_GATE_RANK].set(w_in[:, OFF_A:OFF_U])
    wa2 = jnp.zeros((GLA_RANK_PAD, GLA_WIDTH_K), F32).at[:GLA_GATE_RANK].set(p["gla_w_a2"])
    x1 = _mixer(
        x, _row(p["norm_mix_g"]),
        w_in[:, OFF_Q:OFF_A].astype(BF16), wa.astype(BF16), wa2.astype(BF16),
        _row(p["gla_b_a"]), _row(p["gla_out_g"]),
        w_in[:, OFF_U:OFF_SV].astype(BF16), w_in[:, OFF_SV:OFF_G].astype(BF16),
        _row(p["sgu_norm_g"]), _row(p["sgu_norm_b"]), p["sgu_w"].astype(F32), p["sgu_b"].T.astype(F32),
        w_in[:, OFF_G:].astype(BF16), p["w_proj_a"].astype(BF16), p["w_proj_b"].astype(BF16),
        p["w_mix_out"].astype(BF16))

    kmem, vmem = kv
    wr_t = p["w_router"].T.astype(F32)
    wr_hi = wr_t.astype(BF16)
    wr_lo = (wr_t - wr_hi.astype(F32)).astype(BF16)
    x2, h3, topi, gates, rank, counts = _xattn(
        x1, kmem, vmem, _row(p["norm_x_g"]), p["w_xq"].astype(BF16), p["w_xo"].astype(BF16),
        _row(p["norm_ffn_g"]), wr_hi, wr_lo, p["b_router"].reshape(N_EXPERTS, 1).astype(F32))

    counts = counts.reshape(N_EXPERTS)
    blocks_e = (counts + ROW_BLOCK - 1) // ROW_BLOCK
    padded = blocks_e * ROW_BLOCK
    pad_end = jnp.cumsum(padded)
    pad_start = pad_end - padded
    n_blocks = (T * TOP_K) // ROW_BLOCK + N_EXPERTS
    n_slots = (n_blocks + 1) * ROW_BLOCK
    n_used = (pad_end[-1] // ROW_BLOCK).astype(jnp.int32)
    steps_e = (blocks_e + 1) // 2
    step_end = jnp.cumsum(steps_e)
    n_steps = step_end[-1].astype(jnp.int32)
    max_steps = (n_blocks + N_EXPERTS) // 2
    sidx = jnp.minimum(jnp.arange(max_steps, dtype=jnp.int32), n_steps - 1)
    step_e = jnp.minimum(jnp.sum(step_end[None, :] <= sidx[:, None], axis=1), N_EXPERTS - 1).astype(jnp.int32)
    of_step = step_e[:, None] == jnp.arange(N_EXPERTS, dtype=jnp.int32)[None, :]

    def per_step(v):
        return jnp.sum(jnp.where(of_step, v[None, :], 0), axis=1)

    local = sidx - per_step(step_end - steps_e)
    step_blk = (per_step(pad_start // ROW_BLOCK) + 2 * local).astype(jnp.int32)
    step_pair = (2 * local + 1 < per_step(blocks_e)).astype(jnp.int32)
    ids = jnp.arange(N_EXPERTS, dtype=jnp.int32)
    has_rows = steps_e > 0
    run_e = jnp.cumsum(has_rows.astype(jnp.int32)) - 1
    later = (ids[None, :] > ids[:, None]) & has_rows[None, :]
    next_e = jnp.min(jnp.where(later, ids[None, :], N_EXPERTS), axis=1)
    step_run = per_step(run_e).astype(jnp.int32)
    step_next = per_step(next_e).astype(jnp.int32)
    e_ids = jnp.arange(N_EXPERTS, dtype=jnp.int32)[:, None, None]
    dest = jnp.sum(jnp.where(topi[None] == e_ids, pad_start[:, None, None], 0), axis=0) + rank
    def blocked(tile):
        return (dest.reshape(TOP_K, T // tile, tile).transpose(1, 0, 2)
                .reshape(T // tile, 1, TOP_K * tile).astype(jnp.int32))

    dest_blocks = blocked(TOK_TILE)
    zero_start = jnp.maximum(pad_end - ROW_BLOCK, 0).astype(jnp.int32)

    n_used = n_used.reshape(1)
    xb = _dispatch(zero_start, n_used, blocked(DISPATCH_TILE), h3.reshape(T, D), n_slots)
    yb = _experts(step_blk, step_pair, step_e, step_run, step_next, n_steps.reshape(1), n_used, xb,
                  p["w_e1"], p["b_e1"].reshape(N_EXPERTS, 1, 2 * D_FF),
                  p["w_e2"], p["b_e2"].reshape(N_EXPERTS, 1, D))
    return dest_blocks, x2.reshape(T, D), gates.T, yb


def kernel(x, mem, norm_mix_g, w_in, gla_w_a2, gla_b_a, gla_out_g, sgu_norm_g, sgu_norm_b, sgu_w, sgu_b, w_proj_a, w_proj_b, w_mix_out, norm_x_g, norm_mem_g, w_xq, w_xk, w_xv, w_xo, norm_ffn_g, w_router, b_router, w_e1, b_e1, w_e2, b_e2, norm_final_g):
    B, S, D = x.shape
    depth = w_in.shape[0]
    assert depth == 1, "the final norm is fused into the last layer's combine step"
    stacked = dict(norm_mix_g=norm_mix_g, w_in=w_in, gla_w_a2=gla_w_a2, gla_b_a=gla_b_a,
                   gla_out_g=gla_out_g, sgu_norm_g=sgu_norm_g, sgu_norm_b=sgu_norm_b, sgu_w=sgu_w,
                   sgu_b=sgu_b, w_proj_a=w_proj_a, w_proj_b=w_proj_b, w_mix_out=w_mix_out,
                   norm_x_g=norm_x_g, w_xq=w_xq, w_xo=w_xo, norm_ffn_g=norm_ffn_g,
                   w_router=w_router, b_router=b_router, w_e1=w_e1, b_e1=b_e1, w_e2=w_e2, b_e2=b_e2)
    p = {name: v[0] for name, v in stacked.items()}
    kv = _xattn_kv(mem, _row(norm_mem_g[0]), w_xk[0].astype(BF16), w_xv[0].astype(BF16))
    dest_blocks, x2, gates_t, yb = _layer(x, kv, p)
    out = _combine(dest_blocks, x2, gates_t, _row(norm_final_g), yb)
    return out.reshape(B, S, D)
```

```python
import functools

import jax
import jax.numpy as jnp
from jax import lax
from jax.experimental import pallas as pl
from jax.experimental.pallas import tpu as pltpu

F32 = jnp.float32
BF16 = jnp.bfloat16

D_MODEL = 1024
MEM_LEN = 256
EPS = 1e-5

GLA_HEADS = 4
GLA_DV = 128
GLA_DK = 64
GLA_WIDTH_K = GLA_HEADS * GLA_DK
GLA_WIDTH_V = GLA_HEADS * GLA_DV
GLA_GATE_RANK = 16
GLA_RANK_PAD = 128
GLA_TAU = 16.0
GLA_CHUNK = 64

SGU_WIDTH = 512
SGU_GROUPS = 4
SGU_GROUP_DIM = SGU_WIDTH // SGU_GROUPS
SGU_CHUNK = 128

OFF_Q = 0
OFF_K = OFF_Q + GLA_WIDTH_K
OFF_V = OFF_K + GLA_WIDTH_K
OFF_R = OFF_V + GLA_WIDTH_V
OFF_A = OFF_R + GLA_WIDTH_V
OFF_U = OFF_A + GLA_GATE_RANK
OFF_SV = OFF_U + SGU_WIDTH
OFF_G = OFF_SV + SGU_WIDTH

XATTN_HEADS = 4
XATTN_DH = D_MODEL // XATTN_HEADS

N_EXPERTS = 32
TOP_K = 4
D_FF = D_MODEL
SWIGLU_ALPHA = 1.702
SWIGLU_LIMIT = 7.0

SEQ_TILE = 512
XATTN_TILE = 1024
ROW_BLOCK = 256
TOK_TILE = 512
DISPATCH_TILE = 1024
VMEM_LIMIT = 56 * 1024 * 1024


def _rmsnorm(x, g):
    return x * lax.rsqrt(jnp.mean(x * x, axis=-1, keepdims=True) + EPS) * g


def _sigmoid(x):
    return 1.0 / (1.0 + jnp.exp(-x))


def _log_sigmoid(x):
    return jnp.minimum(x, 0.0) - jnp.log1p(jnp.exp(-jnp.abs(x)))


def _gelu(x):
    return 0.5 * x * (1.0 + lax.erf(x * (2.0 ** -0.5)))


def _dot(a, b):
    return jnp.dot(a, b, preferred_element_type=F32)


def _dot_nt(a, b):
    return lax.dot_general(a, b, (((1,), (1,)), ((), ())), preferred_element_type=F32)


def _dot_tn(a, b):
    return lax.dot_general(a, b, (((0,), (0,)), ((), ())), preferred_element_type=F32)


def _split_bf16(x):
    hi = x.astype(BF16)
    lo = (x - hi.astype(F32)).astype(BF16)
    return hi, lo


def _const_spec(shape):
    zeros = (0,) * len(shape)
    return pl.BlockSpec(shape, lambda *_: zeros, pipeline_mode=pl.Buffered(1))


def _xattn_kv_body(mem_ref, g_ref, wk_ref, wv_ref, k_ref, v_ref):
    m = _rmsnorm(mem_ref[0], g_ref[...]).astype(BF16)
    k_ref[0] = _dot(m, wk_ref[...]).astype(BF16)
    v_ref[0] = _dot(m, wv_ref[...]).astype(BF16)


def _xattn_kv(mem, g, wk, wv):
    B = mem.shape[0]
    return pl.pallas_call(
        _xattn_kv_body,
        grid=(B,),
        in_specs=[
            pl.BlockSpec((1, MEM_LEN, D_MODEL), lambda b: (b, 0, 0)),
            _const_spec((1, D_MODEL)),
            _const_spec((D_MODEL, D_MODEL)),
            _const_spec((D_MODEL, D_MODEL)),
        ],
        out_specs=[
            pl.BlockSpec((1, MEM_LEN, D_MODEL), lambda b: (b, 0, 0)),
            pl.BlockSpec((1, MEM_LEN, D_MODEL), lambda b: (b, 0, 0)),
        ],
        out_shape=[jax.ShapeDtypeStruct((B, MEM_LEN, D_MODEL), BF16)] * 2,
        compiler_params=pltpu.CompilerParams(
            dimension_semantics=("arbitrary",), vmem_limit_bytes=VMEM_LIMIT),
        name="xattn_kv",
    )(mem, g, wk, wv)


def _gla_branch(h, wqkvr_ref, wa_ref, wa2_ref, ba_ref, outg_ref, st_ref):
    ts = h.shape[0]
    n_chunks = ts // GLA_CHUNK
    p = _dot(h, wqkvr_ref[...])
    q = p[:, OFF_Q:OFF_K] * (GLA_DK ** -0.5)
    k = p[:, OFF_K:OFF_V]
    v = p[:, OFF_V:OFF_R].astype(BF16)
    r = p[:, OFF_R:OFF_A]

    a_low = _dot(h, wa_ref[...]).astype(BF16)
    log_a = _log_sigmoid(_dot(a_low, wa2_ref[...]) + ba_ref[...]) * (1.0 / GLA_TAU)

    row = lax.broadcasted_iota(jnp.int32, (ts, ts), 0)
    col = lax.broadcasted_iota(jnp.int32, (ts, ts), 1)
    tri = jnp.where((col <= row) & ((col >> 6) == (row >> 6)), 1.0, 0.0).astype(BF16)
    la_hi, la_lo = _split_bf16(log_a)
    cum = _dot(tri, la_hi) + _dot(tri, la_lo)

    q_in = (q * jnp.exp(cum)).astype(BF16)
    k_in = (k * jnp.exp(-cum)).astype(BF16)

    lane_k = lax.broadcasted_iota(jnp.int32, (1, GLA_WIDTH_K), 1)
    head_masks = [jnp.where((lane_k >> 6) == hh, 1.0, 0.0).astype(BF16) for hh in range(GLA_HEADS)]
    lane_v = lax.broadcasted_iota(jnp.int32, (1, GLA_WIDTH_V), 1)
    value_masks = [jnp.where((lane_v >> 7) == hh, 1.0, 0.0).astype(BF16) for hh in range(GLA_HEADS)]
    r4 = lax.broadcasted_iota(jnp.int32, (GLA_CHUNK, GLA_HEADS * GLA_CHUNK), 0)
    c4 = lax.broadcasted_iota(jnp.int32, (GLA_CHUNK, GLA_HEADS * GLA_CHUNK), 1)
    causal4 = (c4 & (GLA_CHUNK - 1)) <= r4
    sr = lax.broadcasted_iota(jnp.int32, (GLA_WIDTH_V, GLA_WIDTH_K), 0)
    sc = lax.broadcasted_iota(jnp.int32, (GLA_WIDTH_V, GLA_WIDTH_K), 1)
    state_mask = jnp.where((sr >> 7) == (sc >> 6), 1.0, 0.0)

    state = st_ref[...]
    outs = []
    for n in range(n_chunks):
        lo, hi = n * GLA_CHUNK, (n + 1) * GLA_CHUNK
        cum_c = cum[lo:hi]
        last = cum[hi - 1:hi]
        q_c = q_in[lo:hi]
        k_c = k_in[lo:hi]
        v_c = v[lo:hi]
        k_out = (k[lo:hi] * jnp.exp(last - cum_c)).astype(BF16)
        k4 = jnp.concatenate([k_c * head_masks[hh] for hh in range(GLA_HEADS)], axis=0)
        v4 = jnp.concatenate([v_c * value_masks[hh] for hh in range(GLA_HEADS)], axis=0)
        att4 = jnp.where(causal4, _dot_nt(q_c, k4), 0.0).astype(BF16)
        o_intra = _dot(att4, v4)
        o_inter = _dot_nt(q_c, state.astype(BF16))
        outs.append(o_intra + o_inter)
        state = state * jnp.exp(last) + _dot_tn(v_c, k_out) * state_mask
    st_ref[...] = state

    o = jnp.concatenate(outs, axis=0)
    outg = outg_ref[...]
    normed = []
    for hh in range(GLA_HEADS):
        sl = slice(hh * GLA_DV, (hh + 1) * GLA_DV)
        normed.append(_rmsnorm(o[:, sl], outg[:, sl]))
    o = jnp.concatenate(normed, axis=1)
    return o * (r * _sigmoid(r))


def _sgu_branch(h, wu_ref, wsv_ref, ng_ref, nb_ref, sw_ref, sb_ref):
    ts = h.shape[0]
    n_chunks = ts // SGU_CHUNK
    u = _gelu(_dot(h, wu_ref[...]))
    v = _gelu(_dot(h, wsv_ref[...]))
    mu = jnp.mean(v, axis=-1, keepdims=True)
    vc = v - mu
    var = jnp.mean(vc * vc, axis=-1, keepdims=True)
    v = (vc * lax.rsqrt(var + EPS) * ng_ref[...] + nb_ref[...]).astype(BF16)

    row = lax.broadcasted_iota(jnp.int32, (SGU_CHUNK, SGU_CHUNK), 0)
    col = lax.broadcasted_iota(jnp.int32, (SGU_CHUNK, SGU_CHUNK), 1)
    causal = col <= row
    sb = sb_ref[...]
    per_group = []
    for g in range(SGU_GROUPS):
        w = jnp.where(causal, sw_ref[g], 0.0).astype(BF16)
        gs = slice(g * SGU_GROUP_DIM, (g + 1) * SGU_GROUP_DIM)
        vcat = jnp.concatenate(
            [v[c * SGU_CHUNK:(c + 1) * SGU_CHUNK, gs] for c in range(n_chunks)], axis=1)
        per_group.append(_dot(w, vcat) + sb[:, g:g + 1])
    rows = []
    for c in range(n_chunks):
        cs = slice(c * SGU_GROUP_DIM, (c + 1) * SGU_GROUP_DIM)
        rows.append(jnp.concatenate([per_group[g][:, cs] for g in range(SGU_GROUPS)], axis=1))
    mixed = jnp.concatenate(rows, axis=0)
    return u * mixed


def _mixer_body(x_ref, g_ref, wqkvr_ref, wa_ref, wa2_ref, ba_ref, outg_ref,
                wu_ref, wsv_ref, ng_ref, nb_ref, sw_ref, sb_ref,
                wg_ref, wpa_ref, wpb_ref, wmix_ref, o_ref, st_ref):
    @pl.when(pl.program_id(1) == 0)
    def _():
        st_ref[...] = jnp.zeros_like(st_ref)

    x = x_ref[0]
    h = _rmsnorm(x, g_ref[...]).astype(BF16)
    ya_in = _gla_branch(h, wqkvr_ref, wa_ref, wa2_ref, ba_ref, outg_ref, st_ref)
    y_a = _dot(ya_in.astype(BF16), wpa_ref[...])
    yb_in = _sgu_branch(h, wu_ref, wsv_ref, ng_ref, nb_ref, sw_ref, sb_ref)
    y_b = _dot(yb_in.astype(BF16), wpb_ref[...])
    gates = _sigmoid(_dot(h, wg_ref[...]))
    mix = gates[:, :D_MODEL] * y_a + gates[:, D_MODEL:] * y_b
    o_ref[0] = x + _dot(mix.astype(BF16), wmix_ref[...])


def _mixer(x, g, wqkvr, wa, wa2, ba, outg, wu, wsv, ng, nb, sw, sb, wg, wpa, wpb, wmix):
    B, S, D = x.shape
    consts = (g, wqkvr, wa, wa2, ba, outg, wu, wsv, ng, nb, sw, sb, wg, wpa, wpb, wmix)
    return pl.pallas_call(
        _mixer_body,
        grid=(B, S // SEQ_TILE),
        in_specs=[pl.BlockSpec((1, SEQ_TILE, D), lambda b, j: (b, j, 0))]
                 + [_const_spec(c.shape) for c in consts],
        out_specs=pl.BlockSpec((1, SEQ_TILE, D), lambda b, j: (b, j, 0)),
        out_shape=jax.ShapeDtypeStruct((B, S, D), F32),
        scratch_shapes=[pltpu.VMEM((GLA_WIDTH_V, GLA_WIDTH_K), F32)],
        compiler_params=pltpu.CompilerParams(
            dimension_semantics=("arbitrary", "arbitrary"), vmem_limit_bytes=VMEM_LIMIT),
        name="mixer",
    )(x, *consts)


def _xattn_body(x_ref, k_ref, v_ref, gx_ref, wq_ref, wo_ref, gf_ref, wr_hi_ref, wr_lo_ref, br_ref,
                x2_ref, h3_ref, topi_ref, gate_ref, rank_ref, cnt_ref, carry_ref):
    first = (pl.program_id(0) == 0) & (pl.program_id(1) == 0)

    @pl.when(first)
    def _():
        carry_ref[...] = jnp.zeros_like(carry_ref)

    x = x_ref[0]
    ts = x.shape[0]
    h = _rmsnorm(x, gx_ref[...]).astype(BF16)
    q = _dot(h, wq_ref[...]).astype(BF16)
    km = k_ref[0]
    vm = v_ref[0]
    heads = []
    for hh in range(XATTN_HEADS):
        sl = slice(hh * XATTN_DH, (hh + 1) * XATTN_DH)
        s = _dot_nt(q[:, sl], km[:, sl]) * (XATTN_DH ** -0.5)
        s = s - jnp.max(s, axis=-1, keepdims=True)
        e = jnp.exp(s)
        p = e / jnp.sum(e, axis=-1, keepdims=True)
        heads.append(_dot(p.astype(BF16), vm[:, sl]).astype(BF16))
    o = jnp.concatenate(heads, axis=1)
    x2 = x + _dot(o, wo_ref[...])
    x2_ref[0] = x2

    h3 = _rmsnorm(x2, gf_ref[...])
    h3_ref[0] = h3

    h_hi, h_lo = _split_bf16(h3)
    logits = (_dot_nt(wr_hi_ref[...], h_hi) + _dot_nt(wr_hi_ref[...], h_lo)
              + _dot_nt(wr_lo_ref[...], h_hi)) + br_ref[...]

    e_iota = lax.broadcasted_iota(jnp.int32, (N_EXPERTS, ts), 0)
    work = logits
    vals, idxs, hots = [], [], []
    for _ in range(TOP_K):
        m = jnp.max(work, axis=0, keepdims=True)
        idx = jnp.min(jnp.where(work == m, e_iota, N_EXPERTS), axis=0, keepdims=True)
        hot = e_iota == idx
        vals.append(m)
        idxs.append(idx)
        hots.append(hot)
        work = jnp.where(hot, -jnp.inf, work)
    exps = [jnp.exp(vv - vals[0]) for vv in vals]
    denom = exps[0] + exps[1] + exps[2] + exps[3]
    gate_ref[...] = jnp.concatenate([ee / denom for ee in exps], axis=0)
    topi_ref[...] = jnp.concatenate(idxs, axis=0)

    multi = jnp.where(hots[0] | hots[1] | hots[2] | hots[3], 1.0, 0.0)
    srow = lax.broadcasted_iota(jnp.int32, (ts, ts), 0)
    scol = lax.broadcasted_iota(jnp.int32, (ts, ts), 1)
    strict = jnp.where(srow < scol, 1.0, 0.0).astype(BF16)
    before = _dot(multi.astype(BF16), strict) + carry_ref[...]
    ranks = [jnp.sum(jnp.where(hot, before, 0.0), axis=0, keepdims=True) for hot in hots]
    rank_ref[...] = jnp.concatenate(ranks, axis=0).astype(jnp.int32)
    carry = carry_ref[...] + jnp.sum(multi, axis=1, keepdims=True)
    carry_ref[...] = carry
    cnt_ref[...] = carry.astype(jnp.int32)


def _xattn(x1, kmem, vmem, gx, wq, wo, gf, wr_hi, wr_lo, br):
    B, S, D = x1.shape
    T = B * S
    nj = S // XATTN_TILE
    tok_spec = pl.BlockSpec((1, XATTN_TILE, D), lambda b, j: (b, j, 0))
    mem_spec = pl.BlockSpec((1, MEM_LEN, D), lambda b, j: (b, 0, 0))
    lane_spec = pl.BlockSpec((TOP_K, XATTN_TILE), lambda b, j: (0, b * nj + j))
    consts = (gx, wq, wo, gf, wr_hi, wr_lo, br)
    return pl.pallas_call(
        _xattn_body,
        grid=(B, nj),
        in_specs=[tok_spec, mem_spec, mem_spec] + [_const_spec(c.shape) for c in consts],
        out_specs=[tok_spec, tok_spec, lane_spec, lane_spec, lane_spec,
                   pl.BlockSpec((N_EXPERTS, 1), lambda b, j: (0, 0))],
        out_shape=[
            jax.ShapeDtypeStruct((B, S, D), F32),
            jax.ShapeDtypeStruct((B, S, D), F32),
            jax.ShapeDtypeStruct((TOP_K, T), jnp.int32),
            jax.ShapeDtypeStruct((TOP_K, T), F32),
            jax.ShapeDtypeStruct((TOP_K, T), jnp.int32),
            jax.ShapeDtypeStruct((N_EXPERTS, 1), jnp.int32),
        ],
        scratch_shapes=[pltpu.VMEM((N_EXPERTS, 1), F32)],
        compiler_params=pltpu.CompilerParams(
            dimension_semantics=("arbitrary", "arbitrary"), vmem_limit_bytes=VMEM_LIMIT),
        name="xattn_router",
    )(x1, kmem, vmem, *consts)


ROW_GROUP = 8


def _dispatch_body(zs_ref, nu_ref, dest_ref, h_ref, xb_ref, zbuf_ref, zsem, tsem, sem):
    n_blocks = xb_ref.shape[0] // ROW_BLOCK

    def zero_copy(start, zs):
        start = pl.multiple_of(start, ROW_BLOCK)
        return pltpu.make_async_copy(zbuf_ref, xb_ref.at[pl.ds(start, ROW_BLOCK)], zs)

    @pl.when(pl.program_id(0) == 0)
    def _():
        zbuf_ref[...] = jnp.zeros_like(zbuf_ref)
        for e in range(N_EXPERTS):
            zero_copy(zs_ref[e], zsem).start()

        def tail_start(blk, c):
            zero_copy(blk * ROW_BLOCK, tsem).start()
            return c

        lax.fori_loop(nu_ref[0], n_blocks, tail_start, 0)
        for e in range(N_EXPERTS):
            zero_copy(zs_ref[e], zsem).wait()

    def start(g, c):
        base = pl.multiple_of(g * ROW_GROUP, ROW_GROUP)
        tile = h_ref.at[pl.ds(base, ROW_GROUP)]
        for j in range(ROW_GROUP):
            for kk in range(TOP_K):
                d = dest_ref[0, 0, kk * DISPATCH_TILE + base + j]
                pltpu.make_async_copy(tile.at[pl.ds(j, 1)], xb_ref.at[pl.ds(d, 1)], sem).start(
                    priority=kk % 2)
        return c

    lax.fori_loop(0, DISPATCH_TILE // ROW_GROUP, start, 0)
    for kk in range(TOP_K):
        pltpu.make_async_copy(h_ref, xb_ref.at[pl.ds(0, DISPATCH_TILE)], sem).wait()

    @pl.when(pl.program_id(0) == 0)
    def _():
        def tail_wait(blk, c):
            zero_copy(blk * ROW_BLOCK, tsem).wait()
            return c

        lax.fori_loop(nu_ref[0], n_blocks, tail_wait, 0)


def _dispatch(zero_start, n_used, dest_blocks, h3, n_slots):
    T, D = h3.shape
    return pl.pallas_call(
        _dispatch_body,
        grid_spec=pltpu.PrefetchScalarGridSpec(
            num_scalar_prefetch=2,
            grid=(T // DISPATCH_TILE,),
            in_specs=[
                pl.BlockSpec((1, 1, TOP_K * DISPATCH_TILE), lambda i, zs, nu: (i, 0, 0),
                             memory_space=pltpu.SMEM),
                pl.BlockSpec((DISPATCH_TILE, D), lambda i, zs, nu: (i, 0)),
            ],
            out_specs=pl.BlockSpec(memory_space=pl.ANY),
            scratch_shapes=[pltpu.VMEM((ROW_BLOCK, D), F32), pltpu.SemaphoreType.DMA,
                            pltpu.SemaphoreType.DMA, pltpu.SemaphoreType.DMA],
        ),
        out_shape=jax.ShapeDtypeStruct((n_slots, D), F32),
        compiler_params=pltpu.CompilerParams(
            dimension_semantics=("arbitrary",), vmem_limit_bytes=VMEM_LIMIT),
        name="dispatch",
    )(zero_start, n_used, dest_blocks, h3)


def _expert_mlp(xb, w1b_ref, b1_ref, w2b_ref, b2_ref):
    hcat = _dot(xb.astype(BF16), w1b_ref[...]) + b1_ref[0]
    glu = jnp.minimum(hcat[:, :D_FF], SWIGLU_LIMIT)
    lin = jnp.clip(hcat[:, D_FF:], -SWIGLU_LIMIT, SWIGLU_LIMIT)
    act = (lin + 1.0) * glu * _sigmoid(SWIGLU_ALPHA * glu)
    return _dot(act.astype(BF16), w2b_ref[...]) + b2_ref[0]


def _experts_body(blk_ref, pair_ref, se_ref, run_ref, nxt_ref, ns_ref, nu_ref,
                  x_ref, w1_hbm, b1_ref, w2_hbm, b2_ref,
                  y_hbm, w1b_ref, w2b_ref, wf1_ref, wf2_ref, ybuf_ref, zbuf_ref, sems, wsems, zsem):
    s = pl.program_id(0)
    n_steps = ns_ref[0]
    last = pl.num_programs(0) - 1
    total_blocks = y_hbm.shape[0] // ROW_BLOCK

    def tail_copy(blk):
        row0 = pl.multiple_of(blk * ROW_BLOCK, ROW_BLOCK)
        return pltpu.make_async_copy(zbuf_ref, y_hbm.at[pl.ds(row0, ROW_BLOCK)], zsem)

    @pl.when(s == 0)
    def _():
        zbuf_ref[...] = jnp.zeros_like(zbuf_ref)

        def go(blk, c):
            tail_copy(blk).start()
            return c

        lax.fori_loop(nu_ref[0], total_blocks, go, 0)

    @pl.when(s == last)
    def _():
        def done(blk, c):
            tail_copy(blk).wait()
            return c

        lax.fori_loop(nu_ref[0], total_blocks, done, 0)

    def out_copy(step, half):
        slot = step % 2
        row0 = pl.multiple_of((blk_ref[step] + half) * ROW_BLOCK, ROW_BLOCK)
        return pltpu.make_async_copy(
            ybuf_ref.at[slot, pl.ds(half * ROW_BLOCK, ROW_BLOCK)],
            y_hbm.at[pl.ds(row0, ROW_BLOCK)], sems.at[slot])

    def wait_step(step):
        out_copy(step, 0).wait()

        @pl.when(pair_ref[step] == 1)
        def _():
            out_copy(step, 1).wait()

    def weight_copies(e, wslot):
        return (pltpu.make_async_copy(w1_hbm.at[e], wf1_ref.at[wslot], wsems.at[wslot]),
                pltpu.make_async_copy(w2_hbm.at[e], wf2_ref.at[wslot], wsems.at[wslot]))

    @pl.when(s < n_steps)
    def _():
        prev = se_ref[jnp.maximum(s - 1, 0)]
        wslot = run_ref[s] % 2

        @pl.when(s == 0)
        def _():
            for cp in weight_copies(se_ref[0], 0):
                cp.start()

        @pl.when((s == 0) | (se_ref[s] != prev))
        def _():
            for cp in weight_copies(se_ref[s], wslot):
                cp.wait()
            w1b_ref[...] = wf1_ref[wslot].astype(BF16)
            w2b_ref[...] = wf2_ref[wslot].astype(BF16)

            @pl.when(nxt_ref[s] < N_EXPERTS)
            def _():
                for cp in weight_copies(nxt_ref[s], 1 - wslot):
                    cp.start()

        slot = s % 2

        @pl.when(pair_ref[s] == 1)
        def _():
            ybuf_ref[slot] = _expert_mlp(x_ref[...], w1b_ref, b1_ref, w2b_ref, b2_ref)

        @pl.when(pair_ref[s] == 0)
        def _():
            ybuf_ref[slot, :ROW_BLOCK] = _expert_mlp(x_ref[:ROW_BLOCK], w1b_ref, b1_ref, w2b_ref, b2_ref)

    @pl.when((s >= 1) & (s - 1 < n_steps))
    def _():
        wait_step(s - 1)

    @pl.when(s < n_steps)
    def _():
        out_copy(s, 0).start()

        @pl.when(pair_ref[s] == 1)
        def _():
            out_copy(s, 1).start()

        @pl.when(s == last)
        def _():
            wait_step(s)


def _experts(step_blk, step_pair, step_e, step_run, step_next, n_steps, n_used, xb, w1, b1, w2, b2):
    n_rows, D = xb.shape

    def x_map(s, blk, pair, se, run, nxt, ns, nu):
        return (blk[s] * ROW_BLOCK, 0)

    def b_map(s, blk, pair, se, run, nxt, ns, nu):
        return (se[s], 0, 0)

    return pl.pallas_call(
        _experts_body,
        grid_spec=pltpu.PrefetchScalarGridSpec(
            num_scalar_prefetch=7,
            grid=(step_blk.shape[0],),
            in_specs=[
                pl.BlockSpec((pl.Element(2 * ROW_BLOCK), pl.Element(D)), x_map),
                pl.BlockSpec(memory_space=pl.ANY),
                pl.BlockSpec((1, 1, 2 * D_FF), b_map),
                pl.BlockSpec(memory_space=pl.ANY),
                pl.BlockSpec((1, 1, D), b_map),
            ],
            out_specs=pl.BlockSpec(memory_space=pl.ANY),
            scratch_shapes=[pltpu.VMEM((D, 2 * D_FF), BF16), pltpu.VMEM((D_FF, D), BF16),
                            pltpu.VMEM((2, D, 2 * D_FF), F32), pltpu.VMEM((2, D_FF, D), F32),
                            pltpu.VMEM((2, 2 * ROW_BLOCK, D), F32), pltpu.VMEM((ROW_BLOCK, D), F32),
                            pltpu.SemaphoreType.DMA((2,)), pltpu.SemaphoreType.DMA((2,)),
                            pltpu.SemaphoreType.DMA],
        ),
        out_shape=jax.ShapeDtypeStruct((n_rows, D), F32),
        compiler_params=pltpu.CompilerParams(
            dimension_semantics=("arbitrary",), vmem_limit_bytes=VMEM_LIMIT),
        name="experts",
    )(step_blk, step_pair, step_e, step_run, step_next, n_steps, n_used, xb, w1, b1, w2, b2)


def _combine_body(dest_ref, next_ref, x_ref, gate_ref, g_ref, yb_ref, o_ref, rows_ref, sems):
    i = pl.program_id(0)
    last = pl.num_programs(0) - 1
    slot = i % 2
    n_groups = TOK_TILE // ROW_GROUP

    def start_group(idx_ref, sl, base):
        for kk in range(TOP_K):
            tile = rows_ref.at[sl, kk, pl.ds(base, ROW_GROUP)]
            for j in range(ROW_GROUP):
                d = idx_ref[0, 0, kk * TOK_TILE + base + j]
                pltpu.make_async_copy(yb_ref.at[pl.ds(d, 1)], tile.at[pl.ds(j, 1)],
                                      sems.at[sl]).start(priority=kk % 2)

    def reduce_group(base):
        gates = gate_ref[pl.ds(base, ROW_GROUP), :]
        y = gates[:, 0:1] * rows_ref[slot, 0, pl.ds(base, ROW_GROUP), :]
        for kk in range(1, TOP_K):
            y = y + gates[:, kk:kk + 1] * rows_ref[slot, kk, pl.ds(base, ROW_GROUP), :]
        o_ref[pl.ds(base, ROW_GROUP), :] = x_ref[pl.ds(base, ROW_GROUP), :] + y

    @pl.when(i == 0)
    def _():
        def first(g, c):
            start_group(dest_ref, 0, pl.multiple_of(g * ROW_GROUP, ROW_GROUP))
            return c

        lax.fori_loop(0, n_groups, first, 0)

    for kk in range(TOP_K):
        pltpu.make_async_copy(yb_ref.at[pl.ds(0, TOK_TILE)], rows_ref.at[slot, kk], sems.at[slot]).wait()

    @pl.when(i < last)
    def _():
        def both(g, c):
            base = pl.multiple_of(g * ROW_GROUP, ROW_GROUP)
            start_group(next_ref, 1 - slot, base)
            reduce_group(base)
            return c

        lax.fori_loop(0, n_groups, both, 0)

    @pl.when(i == last)
    def _():
        def only(g, c):
            reduce_group(pl.multiple_of(g * ROW_GROUP, ROW_GROUP))
            return c

        lax.fori_loop(0, n_groups, only, 0)

    o_ref[...] = _rmsnorm(o_ref[...], g_ref[...])


def _combine(dest_blocks, x2, gates_t, g, yb):
    T, D = x2.shape
    n_tiles = T // TOK_TILE
    return pl.pallas_call(
        _combine_body,
        grid=(n_tiles,),
        in_specs=[
            pl.BlockSpec((1, 1, TOP_K * TOK_TILE), lambda i: (i, 0, 0), memory_space=pltpu.SMEM),
            pl.BlockSpec((1, 1, TOP_K * TOK_TILE), lambda i: (jnp.minimum(i + 1, n_tiles - 1), 0, 0),
                         memory_space=pltpu.SMEM),
            pl.BlockSpec((TOK_TILE, D), lambda i: (i, 0)),
            pl.BlockSpec((TOK_TILE, TOP_K), lambda i: (i, 0)),
            _const_spec((1, D)),
            pl.BlockSpec(memory_space=pl.ANY),
        ],
        out_specs=pl.BlockSpec((TOK_TILE, D), lambda i: (i, 0)),
        out_shape=jax.ShapeDtypeStruct((T, D), F32),
        scratch_shapes=[pltpu.VMEM((2, TOP_K, TOK_TILE, D), F32), pltpu.SemaphoreType.DMA((2,))],
        compiler_params=pltpu.CompilerParams(
            dimension_semantics=("arbitrary",), vmem_limit_bytes=VMEM_LIMIT),
        name="combine",
    )(dest_blocks, dest_blocks, x2, gates_t, g, yb)


def _row(v):
    return v.reshape(1, -1).astype(F32)


def _layer(x, kv, p):
    B, S, D = x.shape
    T = B * S
    w_in = p["w_in"]
    wa = jnp.zeros((D, GLA_RANK_PAD), F32).at[:, :GLA_GATE_RANK].set(w_in[:, OFF_A:OFF_U])
    wa2 = jnp.zeros((GLA_RANK_PAD, GLA_WIDTH_K), F32).at[:GLA_GATE_RANK].set(p["gla_w_a2"])
    x1 = _mixer(
        x, _row(p["norm_mix_g"]),
        w_in[:, OFF_Q:OFF_A].astype(BF16), wa.astype(BF16), wa2.astype(BF16),
        _row(p["gla_b_a"]), _row(p["gla_out_g"]),
        w_in[:, OFF_U:OFF_SV].astype(BF16), w_in[:, OFF_SV:OFF_G].astype(BF16),
        _row(p["sgu_norm_g"]), _row(p["sgu_norm_b"]), p["sgu_w"].astype(F32), p["sgu_b"].T.astype(F32),
        w_in[:, OFF_G:].astype(BF16), p["w_proj_a"].astype(BF16), p["w_proj_b"].astype(BF16),
        p["w_mix_out"].astype(BF16))

    kmem, vmem = kv
    wr_t = p["w_router"].T.astype(F32)
    wr_hi = wr_t.astype(BF16)
    wr_lo = (wr_t - wr_hi.astype(F32)).astype(BF16)
    x2, h3, topi, gates, rank, counts = _xattn(
        x1, kmem, vmem, _row(p["norm_x_g"]), p["w_xq"].astype(BF16), p["w_xo"].astype(BF16),
        _row(p["norm_ffn_g"]), wr_hi, wr_lo, p["b_router"].reshape(N_EXPERTS, 1).astype(F32))

    counts = counts.reshape(N_EXPERTS)
    blocks_e = (counts + ROW_BLOCK - 1) // ROW_BLOCK
    padded = blocks_e * ROW_BLOCK
    pad_end = jnp.cumsum(padded)
    pad_start = pad_end - padded
    n_blocks = (T * TOP_K) // ROW_BLOCK + N_EXPERTS
    n_slots = (n_blocks + 1) * ROW_BLOCK
    n_used = (pad_end[-1] // ROW_BLOCK).astype(jnp.int32)
    steps_e = (blocks_e + 1) // 2
    step_end = jnp.cumsum(steps_e)
    n_steps = step_end[-1].astype(jnp.int32)
    max_steps = (n_blocks + N_EXPERTS) // 2
    sidx = jnp.minimum(jnp.arange(max_steps, dtype=jnp.int32), n_steps - 1)
    step_e = jnp.minimum(jnp.sum(step_end[None, :] <= sidx[:, None], axis=1), N_EXPERTS - 1).astype(jnp.int32)
    of_step = step_e[:, None] == jnp.arange(N_EXPERTS, dtype=jnp.int32)[None, :]

    def per_step(v):
        return jnp.sum(jnp.where(of_step, v[None, :], 0), axis=1)

    local = sidx - per_step(step_end - steps_e)
    step_blk = (per_step(pad_start // ROW_BLOCK) + 2 * local).astype(jnp.int32)
    step_pair = (2 * local + 1 < per_step(blocks_e)).astype(jnp.int32)
    ids = jnp.arange(N_EXPERTS, dtype=jnp.int32)
    has_rows = steps_e > 0
    run_e = jnp.cumsum(has_rows.astype(jnp.int32)) - 1
    later = (ids[None, :] > ids[:, None]) & has_rows[None, :]
    next_e = jnp.min(jnp.where(later, ids[None, :], N_EXPERTS), axis=1)
    step_run = per_step(run_e).astype(jnp.int32)
    step_next = per_step(next_e).astype(jnp.int32)
    e_ids = jnp.arange(N_EXPERTS, dtype=jnp.int32)[:, None, None]
    dest = jnp.sum(jnp.where(topi[None] == e_ids, pad_start[:, None, None], 0), axis=0) + rank
    def blocked(tile):
        return (dest.reshape(TOP_K, T // tile, tile).transpose(1, 0, 2)
                .reshape(T // tile, 1, TOP_K * tile).astype(jnp.int32))

    dest_blocks = blocked(TOK_TILE)
    zero_start = jnp.maximum(pad_end - ROW_BLOCK, 0).astype(jnp.int32)

    n_used = n_used.reshape(1)
    xb = _dispatch(zero_start, n_used, blocked(DISPATCH_TILE), h3.reshape(T, D), n_slots)
    yb = _experts(step_blk, step_pair, step_e, step_run, step_next, n_steps.reshape(1), n_used, xb,
                  p["w_e1"], p["b_e1"].reshape(N_EXPERTS, 1, 2 * D_FF),
                  p["w_e2"], p["b_e2"].reshape(N_EXPERTS, 1, D))
    return dest_blocks, x2.reshape(T, D), gates.T, yb


def kernel(x, mem, norm_mix_g, w_in, gla_w_a2, gla_b_a, gla_out_g, sgu_norm_g, sgu_norm_b, sgu_w, sgu_b, w_proj_a, w_proj_b, w_mix_out, norm_x_g, norm_mem_g, w_xq, w_xk, w_xv, w_xo, norm_ffn_g, w_router, b_router, w_e1, b_e1, w_e2, b_e2, norm_final_g):
    B, S, D = x.shape
    depth = w_in.shape[0]
    assert depth == 1, "the final norm is fused into the last layer's combine step"
    stacked = dict(norm_mix_g=norm_mix_g, w_in=w_in, gla_w_a2=gla_w_a2, gla_b_a=gla_b_a,
                   gla_out_g=gla_out_g, sgu_norm_g=sgu_norm_g, sgu_norm_b=sgu_norm_b, sgu_w=sgu_w,
                   sgu_b=sgu_b, w_proj_a=w_proj_a, w_proj_b=w_proj_b, w_mix_out=w_mix_out,
                   norm_x_g=norm_x_g, w_xq=w_xq, w_xo=w_xo, norm_ffn_g=norm_ffn_g,
                   w_router=w_router, b_router=b_router, w_e1=w_e1, b_e1=b_e1, w_e2=w_e2, b_e2=b_e2)
    p = {name: v[0] for name, v in stacked.items()}
    kv = _xattn_kv(mem, _row(norm_mem_g[0]), w_xk[0].astype(BF16), w_xv[0].astype(BF16))
    dest_blocks, x2, gates_t, yb = _layer(x, kv, p)
    out = _combine(dest_blocks, x2, gates_t, _row(norm_final_g), yb)
    return out.reshape(B, S, D)
```

```python
import functools

import jax
import jax.numpy as jnp
from jax import lax
from jax.experimental import pallas as pl
from jax.experimental.pallas import tpu as pltpu

F32 = jnp.float32
BF16 = jnp.bfloat16

D_MODEL = 1024
MEM_LEN = 256
EPS = 1e-5

GLA_HEADS = 4
GLA_DV = 128
GLA_DK = 64
GLA_WIDTH_K = GLA_HEADS * GLA_DK
GLA_WIDTH_V = GLA_HEADS * GLA_DV
GLA_GATE_RANK = 16
GLA_RANK_PAD = 128
GLA_TAU = 16.0
GLA_CHUNK = 64

SGU_WIDTH = 512
SGU_GROUPS = 4
SGU_GROUP_DIM = SGU_WIDTH // SGU_GROUPS
SGU_CHUNK = 128

OFF_Q = 0
OFF_K = OFF_Q + GLA_WIDTH_K
OFF_V = OFF_K + GLA_WIDTH_K
OFF_R = OFF_V + GLA_WIDTH_V
OFF_A = OFF_R + GLA_WIDTH_V
OFF_U = OFF_A + GLA_GATE_RANK
OFF_SV = OFF_U + SGU_WIDTH
OFF_G = OFF_SV + SGU_WIDTH

PAD_QKVR = 0
PAD_A = OFF_A
PAD_U = PAD_A + GLA_RANK_PAD
PAD_SV = PAD_U + SGU_WIDTH
PAD_G = PAD_SV + SGU_WIDTH

XATTN_HEADS = 4
XATTN_DH = D_MODEL // XATTN_HEADS

N_EXPERTS = 32
TOP_K = 4
D_FF = D_MODEL
SWIGLU_ALPHA = 1.702
SWIGLU_LIMIT = 7.0

SEQ_TILE = 512
XATTN_TILE = 1024
ROW_BLOCK = 256
TOK_TILE = 512
DISPATCH_TILE = 1024
VMEM_LIMIT = 56 * 1024 * 1024


def _rmsnorm(x, g):
    return x * lax.rsqrt(jnp.mean(x * x, axis=-1, keepdims=True) + EPS) * g


def _sigmoid(x):
    return 1.0 / (1.0 + jnp.exp(-x))


def _log_sigmoid(x):
    return jnp.minimum(x, 0.0) - jnp.log1p(jnp.exp(-jnp.abs(x)))


def _gelu(x):
    return 0.5 * x * (1.0 + lax.erf(x * (2.0 ** -0.5)))


def _dot(a, b):
    return jnp.dot(a, b, preferred_element_type=F32)


def _dot_nt(a, b):
    return lax.dot_general(a, b, (((1,), (1,)), ((), ())), preferred_element_type=F32)


def _dot_tn(a, b):
    return lax.dot_general(a, b, (((0,), (0,)), ((), ())), preferred_element_type=F32)


def _split_bf16(x):
    hi = x.astype(BF16)
    lo = (x - hi.astype(F32)).astype(BF16)
    return hi, lo


def _const_spec(shape):
    zeros = (0,) * len(shape)
    return pl.BlockSpec(shape, lambda *_: zeros, pipeline_mode=pl.Buffered(1))


def _xattn_kv_body(mem_ref, g_ref, wk_ref, wv_ref, k_ref, v_ref):
    m = _rmsnorm(mem_ref[0], g_ref[...]).astype(BF16)
    k_ref[0] = _dot(m, wk_ref[...]).astype(BF16)
    v_ref[0] = _dot(m, wv_ref[...]).astype(BF16)


def _xattn_kv(mem, g, wk, wv):
    B = mem.shape[0]
    return pl.pallas_call(
        _xattn_kv_body,
        grid=(B,),
        in_specs=[
            pl.BlockSpec((1, MEM_LEN, D_MODEL), lambda b: (b, 0, 0)),
            _const_spec((1, D_MODEL)),
            _const_spec((D_MODEL, D_MODEL)),
            _const_spec((D_MODEL, D_MODEL)),
        ],
        out_specs=[
            pl.BlockSpec((1, MEM_LEN, D_MODEL), lambda b: (b, 0, 0)),
            pl.BlockSpec((1, MEM_LEN, D_MODEL), lambda b: (b, 0, 0)),
        ],
        out_shape=[jax.ShapeDtypeStruct((B, MEM_LEN, D_MODEL), BF16)] * 2,
        compiler_params=pltpu.CompilerParams(
            dimension_semantics=("arbitrary",), vmem_limit_bytes=VMEM_LIMIT),
        name="xattn_kv",
    )(mem, g, wk, wv)


def _gla_branch(h, wqkvr_ref, wa_ref, wa2_ref, ba_ref, outg_ref, st_ref):
    ts = h.shape[0]
    n_chunks = ts // GLA_CHUNK
    p = _dot(h, wqkvr_ref[...])
    q = p[:, OFF_Q:OFF_K] * (GLA_DK ** -0.5)
    k = p[:, OFF_K:OFF_V]
    v = p[:, OFF_V:OFF_R].astype(BF16)
    r = p[:, OFF_R:OFF_A]

    a_low = _dot(h, wa_ref[...]).astype(BF16)
    log_a = _log_sigmoid(_dot(a_low, wa2_ref[...]) + ba_ref[...]) * (1.0 / GLA_TAU)

    row = lax.broadcasted_iota(jnp.int32, (ts, ts), 0)
    col = lax.broadcasted_iota(jnp.int32, (ts, ts), 1)
    tri = jnp.where((col <= row) & ((col >> 6) == (row >> 6)), 1.0, 0.0).astype(BF16)
    la_hi, la_lo = _split_bf16(log_a)
    cum = _dot(tri, la_hi) + _dot(tri, la_lo)

    q_in = (q * jnp.exp(cum)).astype(BF16)
    k_in = (k * jnp.exp(-cum)).astype(BF16)

    lane_k = lax.broadcasted_iota(jnp.int32, (1, GLA_WIDTH_K), 1)
    head_masks = [jnp.where((lane_k >> 6) == hh, 1.0, 0.0).astype(BF16) for hh in range(GLA_HEADS)]
    lane_v = lax.broadcasted_iota(jnp.int32, (1, GLA_WIDTH_V), 1)
    value_masks = [jnp.where((lane_v >> 7) == hh, 1.0, 0.0).astype(BF16) for hh in range(GLA_HEADS)]
    r4 = lax.broadcasted_iota(jnp.int32, (GLA_CHUNK, GLA_HEADS * GLA_CHUNK), 0)
    c4 = lax.broadcasted_iota(jnp.int32, (GLA_CHUNK, GLA_HEADS * GLA_CHUNK), 1)
    causal4 = (c4 & (GLA_CHUNK - 1)) <= r4
    sr = lax.broadcasted_iota(jnp.int32, (GLA_WIDTH_V, GLA_WIDTH_K), 0)
    sc = lax.broadcasted_iota(jnp.int32, (GLA_WIDTH_V, GLA_WIDTH_K), 1)
    state_mask = jnp.where((sr >> 7) == (sc >> 6), 1.0, 0.0)

    state = st_ref[...]
    outs = []
    for n in range(n_chunks):
        lo, hi = n * GLA_CHUNK, (n + 1) * GLA_CHUNK
        cum_c = cum[lo:hi]
        last = cum[hi - 1:hi]
        q_c = q_in[lo:hi]
        k_c = k_in[lo:hi]
        v_c = v[lo:hi]
        k_out = (k[lo:hi] * jnp.exp(last - cum_c)).astype(BF16)
        k4 = jnp.concatenate([k_c * head_masks[hh] for hh in range(GLA_HEADS)], axis=0)
        v4 = jnp.concatenate([v_c * value_masks[hh] for hh in range(GLA_HEADS)], axis=0)
        att4 = jnp.where(causal4, _dot_nt(q_c, k4), 0.0).astype(BF16)
        o_intra = _dot(att4, v4)
        o_inter = _dot_nt(q_c, state.astype(BF16))
        outs.append(o_intra + o_inter)
        state = state * jnp.exp(last) + _dot_tn(v_c, k_out) * state_mask
    st_ref[...] = state

    o = jnp.concatenate(outs, axis=0)
    outg = outg_ref[...]
    normed = []
    for hh in range(GLA_HEADS):
        sl = slice(hh * GLA_DV, (hh + 1) * GLA_DV)
        normed.append(_rmsnorm(o[:, sl], outg[:, sl]))
    o = jnp.concatenate(normed, axis=1)
    return o * (r * _sigmoid(r))


def _sgu_branch(h, wu_ref, wsv_ref, ng_ref, nb_ref, sw_ref, sb_ref):
    ts = h.shape[0]
    n_chunks = ts // SGU_CHUNK
    u = _gelu(_dot(h, wu_ref[...]))
    v = _gelu(_dot(h, wsv_ref[...]))
    mu = jnp.mean(v, axis=-1, keepdims=True)
    vc = v - mu
    var = jnp.mean(vc * vc, axis=-1, keepdims=True)
    v = (vc * lax.rsqrt(var + EPS) * ng_ref[...] + nb_ref[...]).astype(BF16)

    row = lax.broadcasted_iota(jnp.int32, (SGU_CHUNK, SGU_CHUNK), 0)
    col = lax.broadcasted_iota(jnp.int32, (SGU_CHUNK, SGU_CHUNK), 1)
    causal = col <= row
    sb = sb_ref[...]
    per_group = []
    for g in range(SGU_GROUPS):
        w = jnp.where(causal, sw_ref[g], 0.0).astype(BF16)
        gs = slice(g * SGU_GROUP_DIM, (g + 1) * SGU_GROUP_DIM)
        vcat = jnp.concatenate(
            [v[c * SGU_CHUNK:(c + 1) * SGU_CHUNK, gs] for c in range(n_chunks)], axis=1)
        per_group.append(_dot(w, vcat) + sb[:, g:g + 1])
    rows = []
    for c in range(n_chunks):
        cs = slice(c * SGU_GROUP_DIM, (c + 1) * SGU_GROUP_DIM)
        rows.append(jnp.concatenate([per_group[g][:, cs] for g in range(SGU_GROUPS)], axis=1))
    mixed = jnp.concatenate(rows, axis=0)
    return u * mixed


def _mixer_body(x_ref, g_ref, win_ref, wa2_ref, ba_ref, outg_ref,
                ng_ref, nb_ref, sw_ref, sb_ref,
                wpa_ref, wpb_ref, wmix_ref, o_ref, st_ref):
    @pl.when(pl.program_id(1) == 0)
    def _():
        st_ref[...] = jnp.zeros_like(st_ref)

    def cols(lo, n):
        return win_ref.at[:, pl.ds(lo, n)]

    wqkvr_ref = cols(PAD_QKVR, OFF_A)
    wa_ref = cols(PAD_A, GLA_RANK_PAD)
    wu_ref = cols(PAD_U, SGU_WIDTH)
    wsv_ref = cols(PAD_SV, SGU_WIDTH)
    wg_ref = cols(PAD_G, 2 * D_MODEL)

    x = x_ref[0]
    h = _rmsnorm(x, g_ref[...]).astype(BF16)
    ya_in = _gla_branch(h, wqkvr_ref, wa_ref, wa2_ref, ba_ref, outg_ref, st_ref)
    y_a = _dot(ya_in.astype(BF16), wpa_ref[...])
    yb_in = _sgu_branch(h, wu_ref, wsv_ref, ng_ref, nb_ref, sw_ref, sb_ref)
    y_b = _dot(yb_in.astype(BF16), wpb_ref[...])
    gates = _sigmoid(_dot(h, wg_ref[...]))
    mix = gates[:, :D_MODEL] * y_a + gates[:, D_MODEL:] * y_b
    o_ref[0] = x + _dot(mix.astype(BF16), wmix_ref[...])


def _mixer(x, g, win, wa2, ba, outg, ng, nb, sw, sb, wpa, wpb, wmix):
    B, S, D = x.shape
    consts = (g, win, wa2, ba, outg, ng, nb, sw, sb, wpa, wpb, wmix)
    return pl.pallas_call(
        _mixer_body,
        grid=(B, S // SEQ_TILE),
        in_specs=[pl.BlockSpec((1, SEQ_TILE, D), lambda b, j: (b, j, 0))]
                 + [_const_spec(c.shape) for c in consts],
        out_specs=pl.BlockSpec((1, SEQ_TILE, D), lambda b, j: (b, j, 0)),
        out_shape=jax.ShapeDtypeStruct((B, S, D), F32),
        scratch_shapes=[pltpu.VMEM((GLA_WIDTH_V, GLA_WIDTH_K), F32)],
        compiler_params=pltpu.CompilerParams(
            dimension_semantics=("arbitrary", "arbitrary"), vmem_limit_bytes=VMEM_LIMIT),
        name="mixer",
    )(x, *consts)


def _xattn_body(x_ref, k_ref, v_ref, gx_ref, wq_ref, wo_ref, gf_ref, wr_hi_ref, wr_lo_ref, br_ref,
                x2_ref, h3_ref, topi_ref, gate_ref, rank_ref, cnt_ref, carry_ref):
    first = (pl.program_id(0) == 0) & (pl.program_id(1) == 0)

    @pl.when(first)
    def _():
        carry_ref[...] = jnp.zeros_like(carry_ref)

    x = x_ref[0]
    ts = x.shape[0]
    h = _rmsnorm(x, gx_ref[...]).astype(BF16)
    q = _dot(h, wq_ref[...]).astype(BF16)
    km = k_ref[0]
    vm = v_ref[0]
    heads = []
    for hh in range(XATTN_HEADS):
        sl = slice(hh * XATTN_DH, (hh + 1) * XATTN_DH)
        s = _dot_nt(q[:, sl], km[:, sl]) * (XATTN_DH ** -0.5)
        s = s - jnp.max(s, axis=-1, keepdims=True)
        e = jnp.exp(s)
        p = e / jnp.sum(e, axis=-1, keepdims=True)
        heads.append(_dot(p.astype(BF16), vm[:, sl]).astype(BF16))
    o = jnp.concatenate(heads, axis=1)
    x2 = x + _dot(o, wo_ref[...])
    x2_ref[0] = x2

    h3 = _rmsnorm(x2, gf_ref[...])
    h3_ref[0] = h3

    h_hi, h_lo = _split_bf16(h3)
    logits = (_dot_nt(wr_hi_ref[...], h_hi) + _dot_nt(wr_hi_ref[...], h_lo)
              + _dot_nt(wr_lo_ref[...], h_hi)) + br_ref[...]

    e_iota = lax.broadcasted_iota(jnp.int32, (N_EXPERTS, ts), 0)
    work = logits
    vals, idxs, hots = [], [], []
    for _ in range(TOP_K):
        m = jnp.max(work, axis=0, keepdims=True)
        idx = jnp.min(jnp.where(work == m, e_iota, N_EXPERTS), axis=0, keepdims=True)
        hot = e_iota == idx
        vals.append(m)
        idxs.append(idx)
        hots.append(hot)
        work = jnp.where(hot, -jnp.inf, work)
    exps = [jnp.exp(vv - vals[0]) for vv in vals]
    denom = exps[0] + exps[1] + exps[2] + exps[3]
    gate_ref[...] = jnp.concatenate([ee / denom for ee in exps], axis=0)
    topi_ref[...] = jnp.concatenate(idxs, axis=0)

    multi = jnp.where(hots[0] | hots[1] | hots[2] | hots[3], 1.0, 0.0)
    srow = lax.broadcasted_iota(jnp.int32, (ts, ts), 0)
    scol = lax.broadcasted_iota(jnp.int32, (ts, ts), 1)
    strict = jnp.where(srow < scol, 1.0, 0.0).astype(BF16)
    before = _dot(multi.astype(BF16), strict) + carry_ref[...]
    ranks = [jnp.sum(jnp.where(hot, before, 0.0), axis=0, keepdims=True) for hot in hots]
    rank_ref[...] = jnp.concatenate(ranks, axis=0).astype(jnp.int32)
    carry = carry_ref[...] + jnp.sum(multi, axis=1, keepdims=True)
    carry_ref[...] = carry
    cnt_ref[...] = carry.astype(jnp.int32)


def _xattn(x1, kmem, vmem, gx, wq, wo, gf, wr_hi, wr_lo, br):
    B, S, D = x1.shape
    T = B * S
    nj = S // XATTN_TILE
    tok_spec = pl.BlockSpec((1, XATTN_TILE, D), lambda b, j: (b, j, 0))
    mem_spec = pl.BlockSpec((1, MEM_LEN, D), lambda b, j: (b, 0, 0))
    lane_spec = pl.BlockSpec((TOP_K, XATTN_TILE), lambda b, j: (0, b * nj + j))
    consts = (gx, wq, wo, gf, wr_hi, wr_lo, br)
    return pl.pallas_call(
        _xattn_body,
        grid=(B, nj),
        in_specs=[tok_spec, mem_spec, mem_spec] + [_const_spec(c.shape) for c in consts],
        out_specs=[tok_spec, tok_spec, lane_spec, lane_spec, lane_spec,
                   pl.BlockSpec((N_EXPERTS, 1), lambda b, j: (0, 0))],
        out_shape=[
            jax.ShapeDtypeStruct((B, S, D), F32),
            jax.ShapeDtypeStruct((B, S, D), F32),
            jax.ShapeDtypeStruct((TOP_K, T), jnp.int32),
            jax.ShapeDtypeStruct((TOP_K, T), F32),
            jax.ShapeDtypeStruct((TOP_K, T), jnp.int32),
            jax.ShapeDtypeStruct((N_EXPERTS, 1), jnp.int32),
        ],
        scratch_shapes=[pltpu.VMEM((N_EXPERTS, 1), F32)],
        compiler_params=pltpu.CompilerParams(
            dimension_semantics=("arbitrary", "arbitrary"), vmem_limit_bytes=VMEM_LIMIT),
        name="xattn_router",
    )(x1, kmem, vmem, *consts)


ROW_GROUP = 8


def _dispatch_body(zs_ref, nu_ref, dest_ref, h_ref, xb_ref, zbuf_ref, zsem, tsem, sem):
    n_blocks = xb_ref.shape[0] // ROW_BLOCK

    def zero_copy(start, zs):
        start = pl.multiple_of(start, ROW_BLOCK)
        return pltpu.make_async_copy(zbuf_ref, xb_ref.at[pl.ds(start, ROW_BLOCK)], zs)

    @pl.when(pl.program_id(0) == 0)
    def _():
        zbuf_ref[...] = jnp.zeros_like(zbuf_ref)
        for e in range(N_EXPERTS):
            zero_copy(zs_ref[e], zsem).start()

        def tail_start(blk, c):
            zero_copy(blk * ROW_BLOCK, tsem).start()
            return c

        lax.fori_loop(nu_ref[0], n_blocks, tail_start, 0)
        for e in range(N_EXPERTS):
            zero_copy(zs_ref[e], zsem).wait()

    def start(g, c):
        base = pl.multiple_of(g * ROW_GROUP, ROW_GROUP)
        tile = h_ref.at[pl.ds(base, ROW_GROUP)]
        for j in range(ROW_GROUP):
            for kk in range(TOP_K):
                d = dest_ref[0, 0, kk * DISPATCH_TILE + base + j]
                pltpu.make_async_copy(tile.at[pl.ds(j, 1)], xb_ref.at[pl.ds(d, 1)], sem).start(
                    priority=kk % 2)
        return c

    lax.fori_loop(0, DISPATCH_TILE // ROW_GROUP, start, 0)
    for kk in range(TOP_K):
        pltpu.make_async_copy(h_ref, xb_ref.at[pl.ds(0, DISPATCH_TILE)], sem).wait()

    @pl.when(pl.program_id(0) == 0)
    def _():
        def tail_wait(blk, c):
            zero_copy(blk * ROW_BLOCK, tsem).wait()
            return c

        lax.fori_loop(nu_ref[0], n_blocks, tail_wait, 0)


def _dispatch(zero_start, n_used, dest_blocks, h3, n_slots):
    T, D = h3.shape
    return pl.pallas_call(
        _dispatch_body,
        grid_spec=pltpu.PrefetchScalarGridSpec(
            num_scalar_prefetch=2,
            grid=(T // DISPATCH_TILE,),
            in_specs=[
                pl.BlockSpec((1, 1, TOP_K * DISPATCH_TILE), lambda i, zs, nu: (i, 0, 0),
                             memory_space=pltpu.SMEM),
                pl.BlockSpec((DISPATCH_TILE, D), lambda i, zs, nu: (i, 0)),
            ],
            out_specs=pl.BlockSpec(memory_space=pl.ANY),
            scratch_shapes=[pltpu.VMEM((ROW_BLOCK, D), F32), pltpu.SemaphoreType.DMA,
                            pltpu.SemaphoreType.DMA, pltpu.SemaphoreType.DMA],
        ),
        out_shape=jax.ShapeDtypeStruct((n_slots, D), F32),
        compiler_params=pltpu.CompilerParams(
            dimension_semantics=("arbitrary",), vmem_limit_bytes=VMEM_LIMIT),
        name="dispatch",
    )(zero_start, n_used, dest_blocks, h3)


def _expert_mlp(xb, w1b_ref, b1_ref, w2b_ref, b2_ref):
    hcat = _dot(xb.astype(BF16), w1b_ref[...]) + b1_ref[0]
    glu = jnp.minimum(hcat[:, :D_FF], SWIGLU_LIMIT)
    lin = jnp.clip(hcat[:, D_FF:], -SWIGLU_LIMIT, SWIGLU_LIMIT)
    act = (lin + 1.0) * glu * _sigmoid(SWIGLU_ALPHA * glu)
    return _dot(act.astype(BF16), w2b_ref[...]) + b2_ref[0]


def _experts_body(blk_ref, pair_ref, se_ref, run_ref, nxt_ref, ns_ref, nu_ref,
                  x_ref, w1_hbm, b1_ref, w2_hbm, b2_ref,
                  y_hbm, w1b_ref, w2b_ref, wf1_ref, wf2_ref, ybuf_ref, zbuf_ref, sems, wsems, zsem):
    s = pl.program_id(0)
    n_steps = ns_ref[0]
    last = pl.num_programs(0) - 1
    total_blocks = y_hbm.shape[0] // ROW_BLOCK

    def tail_copy(blk):
        row0 = pl.multiple_of(blk * ROW_BLOCK, ROW_BLOCK)
        return pltpu.make_async_copy(zbuf_ref, y_hbm.at[pl.ds(row0, ROW_BLOCK)], zsem)

    @pl.when(s == 0)
    def _():
        zbuf_ref[...] = jnp.zeros_like(zbuf_ref)

        def go(blk, c):
            tail_copy(blk).start()
            return c

        lax.fori_loop(nu_ref[0], total_blocks, go, 0)

    @pl.when(s == last)
    def _():
        def done(blk, c):
            tail_copy(blk).wait()
            return c

        lax.fori_loop(nu_ref[0], total_blocks, done, 0)

    def out_copy(step, half):
        slot = step % 2
        row0 = pl.multiple_of((blk_ref[step] + half) * ROW_BLOCK, ROW_BLOCK)
        return pltpu.make_async_copy(
            ybuf_ref.at[slot, pl.ds(half * ROW_BLOCK, ROW_BLOCK)],
            y_hbm.at[pl.ds(row0, ROW_BLOCK)], sems.at[slot])

    def wait_step(step):
        out_copy(step, 0).wait()

        @pl.when(pair_ref[step] == 1)
        def _():
            out_copy(step, 1).wait()

    def weight_copies(e, wslot):
        return (pltpu.make_async_copy(w1_hbm.at[e], wf1_ref.at[wslot], wsems.at[wslot]),
                pltpu.make_async_copy(w2_hbm.at[e], wf2_ref.at[wslot], wsems.at[wslot]))

    @pl.when(s < n_steps)
    def _():
        prev = se_ref[jnp.maximum(s - 1, 0)]
        wslot = run_ref[s] % 2

        @pl.when(s == 0)
        def _():
            for cp in weight_copies(se_ref[0], 0):
                cp.start()

        @pl.when((s == 0) | (se_ref[s] != prev))
        def _():
            for cp in weight_copies(se_ref[s], wslot):
                cp.wait()
            w1b_ref[...] = wf1_ref[wslot].astype(BF16)
            w2b_ref[...] = wf2_ref[wslot].astype(BF16)

            @pl.when(nxt_ref[s] < N_EXPERTS)
            def _():
                for cp in weight_copies(nxt_ref[s], 1 - wslot):
                    cp.start()

        slot = s % 2

        @pl.when(pair_ref[s] == 1)
        def _():
            ybuf_ref[slot] = _expert_mlp(x_ref[...], w1b_ref, b1_ref, w2b_ref, b2_ref)

        @pl.when(pair_ref[s] == 0)
        def _():
            ybuf_ref[slot, :ROW_BLOCK] = _expert_mlp(x_ref[:ROW_BLOCK], w1b_ref, b1_ref, w2b_ref, b2_ref)

    @pl.when((s >= 1) & (s - 1 < n_steps))
    def _():
        wait_step(s - 1)

    @pl.when(s < n_steps)
    def _():
        out_copy(s, 0).start()

        @pl.when(pair_ref[s] == 1)
        def _():
            out_copy(s, 1).start()

        @pl.when(s == last)
        def _():
            wait_step(s)


def _experts(step_blk, step_pair, step_e, step_run, step_next, n_steps, n_used, xb, w1, b1, w2, b2):
    n_rows, D = xb.shape

    def x_map(s, blk, pair, se, run, nxt, ns, nu):
        return (blk[s] * ROW_BLOCK, 0)

    def b_map(s, blk, pair, se, run, nxt, ns, nu):
        return (se[s], 0, 0)

    return pl.pallas_call(
        _experts_body,
        grid_spec=pltpu.PrefetchScalarGridSpec(
            num_scalar_prefetch=7,
            grid=(step_blk.shape[0],),
            in_specs=[
                pl.BlockSpec((pl.Element(2 * ROW_BLOCK), pl.Element(D)), x_map),
                pl.BlockSpec(memory_space=pl.ANY),
                pl.BlockSpec((1, 1, 2 * D_FF), b_map),
                pl.BlockSpec(memory_space=pl.ANY),
                pl.BlockSpec((1, 1, D), b_map),
            ],
            out_specs=pl.BlockSpec(memory_space=pl.ANY),
            scratch_shapes=[pltpu.VMEM((D, 2 * D_FF), BF16), pltpu.VMEM((D_FF, D), BF16),
                            pltpu.VMEM((2, D, 2 * D_FF), F32), pltpu.VMEM((2, D_FF, D), F32),
                            pltpu.VMEM((2, 2 * ROW_BLOCK, D), F32), pltpu.VMEM((ROW_BLOCK, D), F32),
                            pltpu.SemaphoreType.DMA((2,)), pltpu.SemaphoreType.DMA((2,)),
                            pltpu.SemaphoreType.DMA],
        ),
        out_shape=jax.ShapeDtypeStruct((n_rows, D), F32),
        compiler_params=pltpu.CompilerParams(
            dimension_semantics=("arbitrary",), vmem_limit_bytes=VMEM_LIMIT),
        name="experts",
    )(step_blk, step_pair, step_e, step_run, step_next, n_steps, n_used, xb, w1, b1, w2, b2)


def _combine_body(dest_ref, next_ref, x_ref, gate_ref, g_ref, yb_ref, o_ref, rows_ref, sems):
    i = pl.program_id(0)
    last = pl.num_programs(0) - 1
    slot = i % 2
    n_groups = TOK_TILE // ROW_GROUP

    def start_group(idx_ref, sl, base):
        for kk in range(TOP_K):
            tile = rows_ref.at[sl, kk, pl.ds(base, ROW_GROUP)]
            for j in range(ROW_GROUP):
                d = idx_ref[0, 0, kk * TOK_TILE + base + j]
                pltpu.make_async_copy(yb_ref.at[pl.ds(d, 1)], tile.at[pl.ds(j, 1)],
                                      sems.at[sl]).start(priority=kk % 2)

    def reduce_group(base):
        gates = gate_ref[pl.ds(base, ROW_GROUP), :]
        y = gates[:, 0:1] * rows_ref[slot, 0, pl.ds(base, ROW_GROUP), :]
        for kk in range(1, TOP_K):
            y = y + gates[:, kk:kk + 1] * rows_ref[slot, kk, pl.ds(base, ROW_GROUP), :]
        o_ref[pl.ds(base, ROW_GROUP), :] = x_ref[pl.ds(base, ROW_GROUP), :] + y

    @pl.when(i == 0)
    def _():
        def first(g, c):
            start_group(dest_ref, 0, pl.multiple_of(g * ROW_GROUP, ROW_GROUP))
            return c

        lax.fori_loop(0, n_groups, first, 0)

    for kk in range(TOP_K):
        pltpu.make_async_copy(yb_ref.at[pl.ds(0, TOK_TILE)], rows_ref.at[slot, kk], sems.at[slot]).wait()

    @pl.when(i < last)
    def _():
        def both(g, c):
            base = pl.multiple_of(g * ROW_GROUP, ROW_GROUP)
            start_group(next_ref, 1 - slot, base)
            reduce_group(base)
            return c

        lax.fori_loop(0, n_groups, both, 0)

    @pl.when(i == last)
    def _():
        def only(g, c):
            reduce_group(pl.multiple_of(g * ROW_GROUP, ROW_GROUP))
            return c

        lax.fori_loop(0, n_groups, only, 0)

    o_ref[...] = _rmsnorm(o_ref[...], g_ref[...])


def _combine(dest_blocks, x2, gates_t, g, yb):
    T, D = x2.shape
    n_tiles = T // TOK_TILE
    return pl.pallas_call(
        _combine_body,
        grid=(n_tiles,),
        in_specs=[
            pl.BlockSpec((1, 1, TOP_K * TOK_TILE), lambda i: (i, 0, 0), memory_space=pltpu.SMEM),
            pl.BlockSpec((1, 1, TOP_K * TOK_TILE), lambda i: (jnp.minimum(i + 1, n_tiles - 1), 0, 0),
                         memory_space=pltpu.SMEM),
            pl.BlockSpec((TOK_TILE, D), lambda i: (i, 0)),
            pl.BlockSpec((TOK_TILE, TOP_K), lambda i: (i, 0)),
            _const_spec((1, D)),
            pl.BlockSpec(memory_space=pl.ANY),
        ],
        out_specs=pl.BlockSpec((TOK_TILE, D), lambda i: (i, 0)),
        out_shape=jax.ShapeDtypeStruct((T, D), F32),
        scratch_shapes=[pltpu.VMEM((2, TOP_K, TOK_TILE, D), F32), pltpu.SemaphoreType.DMA((2,))],
        compiler_params=pltpu.CompilerParams(
            dimension_semantics=("arbitrary",), vmem_limit_bytes=VMEM_LIMIT),
        name="combine",
    )(dest_blocks, dest_blocks, x2, gates_t, g, yb)


def _row(v):
    return v.reshape(1, -1).astype(F32)


def _layer(x, kv, p):
    B, S, D = x.shape
    T = B * S
    w_in = p["w_in"]
    wa2 = jnp.zeros((GLA_RANK_PAD, GLA_WIDTH_K), F32).at[:GLA_GATE_RANK].set(p["gla_w_a2"])
    win = jnp.concatenate(
        [w_in[:, :OFF_U], jnp.zeros((D, GLA_RANK_PAD - GLA_GATE_RANK), w_in.dtype), w_in[:, OFF_U:]],
        axis=1).astype(BF16)
    x1 = _mixer(
        x, _row(p["norm_mix_g"]), win, wa2.astype(BF16),
        _row(p["gla_b_a"]), _row(p["gla_out_g"]),
        _row(p["sgu_norm_g"]), _row(p["sgu_norm_b"]), p["sgu_w"].astype(F32), p["sgu_b"].T.astype(F32),
        p["w_proj_a"].astype(BF16), p["w_proj_b"].astype(BF16), p["w_mix_out"].astype(BF16))

    kmem, vmem = kv
    wr_t = p["w_router"].T.astype(F32)
    wr_hi = wr_t.astype(BF16)
    wr_lo = (wr_t - wr_hi.astype(F32)).astype(BF16)
    x2, h3, topi, gates, rank, counts = _xattn(
        x1, kmem, vmem, _row(p["norm_x_g"]), p["w_xq"].astype(BF16), p["w_xo"].astype(BF16),
        _row(p["norm_ffn_g"]), wr_hi, wr_lo, p["b_router"].reshape(N_EXPERTS, 1).astype(F32))

    counts = counts.reshape(N_EXPERTS)
    blocks_e = (counts + ROW_BLOCK - 1) // ROW_BLOCK
    padded = blocks_e * ROW_BLOCK
    pad_end = jnp.cumsum(padded)
    pad_start = pad_end - padded
    n_blocks = (T * TOP_K) // ROW_BLOCK + N_EXPERTS
    n_slots = (n_blocks + 1) * ROW_BLOCK
    n_used = (pad_end[-1] // ROW_BLOCK).astype(jnp.int32)
    steps_e = (blocks_e + 1) // 2
    step_end = jnp.cumsum(steps_e)
    n_steps = step_end[-1].astype(jnp.int32)
    max_steps = (n_blocks + N_EXPERTS) // 2
    sidx = jnp.minimum(jnp.arange(max_steps, dtype=jnp.int32), n_steps - 1)
    step_e = jnp.minimum(jnp.sum(step_end[None, :] <= sidx[:, None], axis=1), N_EXPERTS - 1).astype(jnp.int32)
    of_step = step_e[:, None] == jnp.arange(N_EXPERTS, dtype=jnp.int32)[None, :]

    def per_step(v):
        return jnp.sum(jnp.where(of_step, v[None, :], 0), axis=1)

    local = sidx - per_step(step_end - steps_e)
    step_blk = (per_step(pad_start // ROW_BLOCK) + 2 * local).astype(jnp.int32)
    step_pair = (2 * local + 1 < per_step(blocks_e)).astype(jnp.int32)
    ids = jnp.arange(N_EXPERTS, dtype=jnp.int32)
    has_rows = steps_e > 0
    run_e = jnp.cumsum(has_rows.astype(jnp.int32)) - 1
    later = (ids[None, :] > ids[:, None]) & has_rows[None, :]
    next_e = jnp.min(jnp.where(later, ids[None, :], N_EXPERTS), axis=1)
    step_run = per_step(run_e).astype(jnp.int32)
    step_next = per_step(next_e).astype(jnp.int32)
    e_ids = jnp.arange(N_EXPERTS, dtype=jnp.int32)[:, None, None]
    dest = jnp.sum(jnp.where(topi[None] == e_ids, pad_start[:, None, None], 0), axis=0) + rank
    def blocked(tile):
        return (dest.reshape(TOP_K, T // tile, tile).transpose(1, 0, 2)
                .reshape(T // tile, 1, TOP_K * tile).astype(jnp.int32))

    dest_blocks = blocked(TOK_TILE)
    zero_start = jnp.maximum(pad_end - ROW_BLOCK, 0).astype(jnp.int32)

    n_used = n_used.reshape(1)
    xb = _dispatch(zero_start, n_used, blocked(DISPATCH_TILE), h3.reshape(T, D), n_slots)
    yb = _experts(step_blk, step_pair, step_e, step_run, step_next, n_steps.reshape(1), n_used, xb,
                  p["w_e1"], p["b_e1"].reshape(N_EXPERTS, 1, 2 * D_FF),
                  p["w_e2"], p["b_e2"].reshape(N_EXPERTS, 1, D))
    return dest_blocks, x2.reshape(T, D), gates.T, yb


def kernel(x, mem, norm_mix_g, w_in, gla_w_a2, gla_b_a, gla_out_g, sgu_norm_g, sgu_norm_b, sgu_w, sgu_b, w_proj_a, w_proj_b, w_mix_out, norm_x_g, norm_mem_g, w_xq, w_xk, w_xv, w_xo, norm_ffn_g, w_router, b_router, w_e1, b_e1, w_e2, b_e2, norm_final_g):
    B, S, D = x.shape
    depth = w_in.shape[0]
    assert depth == 1, "the final norm is fused into the last layer's combine step"
    stacked = dict(norm_mix_g=norm_mix_g, w_in=w_in, gla_w_a2=gla_w_a2, gla_b_a=gla_b_a,
                   gla_out_g=gla_out_g, sgu_norm_g=sgu_norm_g, sgu_norm_b=sgu_norm_b, sgu_w=sgu_w,
                   sgu_b=sgu_b, w_proj_a=w_proj_a, w_proj_b=w_proj_b, w_mix_out=w_mix_out,
                   norm_x_g=norm_x_g, w_xq=w_xq, w_xo=w_xo, norm_ffn_g=norm_ffn_g,
                   w_router=w_router, b_router=b_router, w_e1=w_e1, b_e1=b_e1, w_e2=w_e2, b_e2=b_e2)
    p = {name: v[0] for name, v in stacked.items()}
    kv = _xattn_kv(mem, _row(norm_mem_g[0]), w_xk[0].astype(BF16), w_xv[0].astype(BF16))
    dest_blocks, x2, gates_t, yb = _layer(x, kv, p)
    out = _combine(dest_blocks, x2, gates_t, _row(norm_final_g), yb)
    return out.reshape(B, S, D)
```

```python
import functools

import jax
import jax.numpy as jnp
from jax import lax
from jax.experimental import pallas as pl
from jax.experimental.pallas import tpu as pltpu

F32 = jnp.float32
BF16 = jnp.bfloat16

D_MODEL = 1024
MEM_LEN = 256
EPS = 1e-5

GLA_HEADS = 4
GLA_DV = 128
GLA_DK = 64
GLA_WIDTH_K = GLA_HEADS * GLA_DK
GLA_WIDTH_V = GLA_HEADS * GLA_DV
GLA_GATE_RANK = 16
GLA_RANK_PAD = 128
GLA_TAU = 16.0
GLA_CHUNK = 64

SGU_WIDTH = 512
SGU_GROUPS = 4
SGU_GROUP_DIM = SGU_WIDTH // SGU_GROUPS
SGU_CHUNK = 128

OFF_Q = 0
OFF_K = OFF_Q + GLA_WIDTH_K
OFF_V = OFF_K + GLA_WIDTH_K
OFF_R = OFF_V + GLA_WIDTH_V
OFF_A = OFF_R + GLA_WIDTH_V
OFF_U = OFF_A + GLA_GATE_RANK
OFF_SV = OFF_U + SGU_WIDTH
OFF_G = OFF_SV + SGU_WIDTH

XATTN_HEADS = 4
XATTN_DH = D_MODEL // XATTN_HEADS

N_EXPERTS = 32
TOP_K = 4
D_FF = D_MODEL
SWIGLU_ALPHA = 1.702
SWIGLU_LIMIT = 7.0

SEQ_TILE = 512
XATTN_TILE = 1024
ROW_BLOCK = 256
TOK_TILE = 512
DISPATCH_TILE = 1024
VMEM_LIMIT = 56 * 1024 * 1024


def _rmsnorm(x, g):
    return x * lax.rsqrt(jnp.mean(x * x, axis=-1, keepdims=True) + EPS) * g


def _sigmoid(x):
    return 1.0 / (1.0 + jnp.exp(-x))


def _log_sigmoid(x):
    return jnp.minimum(x, 0.0) - jnp.log1p(jnp.exp(-jnp.abs(x)))


def _gelu(x):
    return 0.5 * x * (1.0 + lax.erf(x * (2.0 ** -0.5)))


def _dot(a, b):
    return jnp.dot(a, b, preferred_element_type=F32)


def _dot_nt(a, b):
    return lax.dot_general(a, b, (((1,), (1,)), ((), ())), preferred_element_type=F32)


def _dot_tn(a, b):
    return lax.dot_general(a, b, (((0,), (0,)), ((), ())), preferred_element_type=F32)


def _split_bf16(x):
    hi = x.astype(BF16)
    lo = (x - hi.astype(F32)).astype(BF16)
    return hi, lo


def _const_spec(shape):
    zeros = (0,) * len(shape)
    return pl.BlockSpec(shape, lambda *_: zeros, pipeline_mode=pl.Buffered(1))


def _xattn_kv_body(mem_ref, g_ref, wk_ref, wv_ref, k_ref, v_ref):
    m = _rmsnorm(mem_ref[0], g_ref[...]).astype(BF16)
    k_ref[0] = _dot(m, wk_ref[...]).astype(BF16)
    v_ref[0] = _dot(m, wv_ref[...]).astype(BF16)


def _xattn_kv(mem, g, wk, wv):
    B = mem.shape[0]
    return pl.pallas_call(
        _xattn_kv_body,
        grid=(B,),
        in_specs=[
            pl.BlockSpec((1, MEM_LEN, D_MODEL), lambda b: (b, 0, 0)),
            _const_spec((1, D_MODEL)),
            _const_spec((D_MODEL, D_MODEL)),
            _const_spec((D_MODEL, D_MODEL)),
        ],
        out_specs=[
            pl.BlockSpec((1, MEM_LEN, D_MODEL), lambda b: (b, 0, 0)),
            pl.BlockSpec((1, MEM_LEN, D_MODEL), lambda b: (b, 0, 0)),
        ],
        out_shape=[jax.ShapeDtypeStruct((B, MEM_LEN, D_MODEL), BF16)] * 2,
        compiler_params=pltpu.CompilerParams(
            dimension_semantics=("arbitrary",), vmem_limit_bytes=VMEM_LIMIT),
        name="xattn_kv",
    )(mem, g, wk, wv)


def _gla_branch(h, wqkvr_ref, wa_ref, wa2_ref, ba_ref, outg_ref, st_ref):
    ts = h.shape[0]
    n_chunks = ts // GLA_CHUNK
    p = _dot(h, wqkvr_ref[...])
    q = p[:, OFF_Q:OFF_K] * (GLA_DK ** -0.5)
    k = p[:, OFF_K:OFF_V]
    v = p[:, OFF_V:OFF_R].astype(BF16)
    r = p[:, OFF_R:OFF_A]

    a_low = _dot(h, wa_ref[...]).astype(BF16)
    log_a = _log_sigmoid(_dot(a_low, wa2_ref[...]) + ba_ref[...]) * (1.0 / GLA_TAU)

    row = lax.broadcasted_iota(jnp.int32, (ts, ts), 0)
    col = lax.broadcasted_iota(jnp.int32, (ts, ts), 1)
    tri = jnp.where((col <= row) & ((col >> 6) == (row >> 6)), 1.0, 0.0).astype(BF16)
    la_hi, la_lo = _split_bf16(log_a)
    cum = _dot(tri, la_hi) + _dot(tri, la_lo)

    q_in = (q * jnp.exp(cum)).astype(BF16)
    k_in = (k * jnp.exp(-cum)).astype(BF16)

    lane_k = lax.broadcasted_iota(jnp.int32, (1, GLA_WIDTH_K), 1)
    head_masks = [jnp.where((lane_k >> 6) == hh, 1.0, 0.0).astype(BF16) for hh in range(GLA_HEADS)]
    lane_v = lax.broadcasted_iota(jnp.int32, (1, GLA_WIDTH_V), 1)
    value_masks = [jnp.where((lane_v >> 7) == hh, 1.0, 0.0).astype(BF16) for hh in range(GLA_HEADS)]
    r4 = lax.broadcasted_iota(jnp.int32, (GLA_CHUNK, GLA_HEADS * GLA_CHUNK), 0)
    c4 = lax.broadcasted_iota(jnp.int32, (GLA_CHUNK, GLA_HEADS * GLA_CHUNK), 1)
    causal4 = (c4 & (GLA_CHUNK - 1)) <= r4
    sr = lax.broadcasted_iota(jnp.int32, (GLA_WIDTH_V, GLA_WIDTH_K), 0)
    sc = lax.broadcasted_iota(jnp.int32, (GLA_WIDTH_V, GLA_WIDTH_K), 1)
    state_mask = jnp.where((sr >> 7) == (sc >> 6), 1.0, 0.0)

    state = st_ref[...]
    outs = []
    for n in range(n_chunks):
        lo, hi = n * GLA_CHUNK, (n + 1) * GLA_CHUNK
        cum_c = cum[lo:hi]
        last = cum[hi - 1:hi]
        q_c = q_in[lo:hi]
        k_c = k_in[lo:hi]
        v_c = v[lo:hi]
        k_out = (k[lo:hi] * jnp.exp(last - cum_c)).astype(BF16)
        k4 = jnp.concatenate([k_c * head_masks[hh] for hh in range(GLA_HEADS)], axis=0)
        v4 = jnp.concatenate([v_c * value_masks[hh] for hh in range(GLA_HEADS)], axis=0)
        att4 = jnp.where(causal4, _dot_nt(q_c, k4), 0.0).astype(BF16)
        o_intra = _dot(att4, v4)
        o_inter = _dot_nt(q_c, state.astype(BF16))
        outs.append(o_intra + o_inter)
        state = state * jnp.exp(last) + _dot_tn(v_c, k_out) * state_mask
    st_ref[...] = state

    o = jnp.concatenate(outs, axis=0)
    outg = outg_ref[...]
    normed = []
    for hh in range(GLA_HEADS):
        sl = slice(hh * GLA_DV, (hh + 1) * GLA_DV)
        normed.append(_rmsnorm(o[:, sl], outg[:, sl]))
    o = jnp.concatenate(normed, axis=1)
    return o * (r * _sigmoid(r))


def _sgu_branch(h, wu_ref, wsv_ref, ng_ref, nb_ref, sw_ref, sb_ref):
    ts = h.shape[0]
    n_chunks = ts // SGU_CHUNK
    u = _gelu(_dot(h, wu_ref[...]))
    v = _gelu(_dot(h, wsv_ref[...]))
    mu = jnp.mean(v, axis=-1, keepdims=True)
    vc = v - mu
    var = jnp.mean(vc * vc, axis=-1, keepdims=True)
    v = (vc * lax.rsqrt(var + EPS) * ng_ref[...] + nb_ref[...]).astype(BF16)

    row = lax.broadcasted_iota(jnp.int32, (SGU_CHUNK, SGU_CHUNK), 0)
    col = lax.broadcasted_iota(jnp.int32, (SGU_CHUNK, SGU_CHUNK), 1)
    causal = col <= row
    sb = sb_ref[...]
    per_group = []
    for g in range(SGU_GROUPS):
        w = jnp.where(causal, sw_ref[g], 0.0).astype(BF16)
        gs = slice(g * SGU_GROUP_DIM, (g + 1) * SGU_GROUP_DIM)
        vcat = jnp.concatenate(
            [v[c * SGU_CHUNK:(c + 1) * SGU_CHUNK, gs] for c in range(n_chunks)], axis=1)
        per_group.append(_dot(w, vcat) + sb[:, g:g + 1])
    rows = []
    for c in range(n_chunks):
        cs = slice(c * SGU_GROUP_DIM, (c + 1) * SGU_GROUP_DIM)
        rows.append(jnp.concatenate([per_group[g][:, cs] for g in range(SGU_GROUPS)], axis=1))
    mixed = jnp.concatenate(rows, axis=0)
    return u * mixed


def _mixer_body(x_ref, g_ref, wqkvr_ref, wa_ref, wa2_ref, ba_ref, outg_ref,
                wu_ref, wsv_ref, ng_ref, nb_ref, sw_ref, sb_ref,
                wg_ref, wpa_ref, wpb_ref, wmix_ref, o_ref, st_ref):
    @pl.when(pl.program_id(1) == 0)
    def _():
        st_ref[...] = jnp.zeros_like(st_ref)

    x = x_ref[0]
    h = _rmsnorm(x, g_ref[...]).astype(BF16)
    ya_in = _gla_branch(h, wqkvr_ref, wa_ref, wa2_ref, ba_ref, outg_ref, st_ref)
    y_a = _dot(ya_in.astype(BF16), wpa_ref[...])
    yb_in = _sgu_branch(h, wu_ref, wsv_ref, ng_ref, nb_ref, sw_ref, sb_ref)
    y_b = _dot(yb_in.astype(BF16), wpb_ref[...])
    gates = _sigmoid(_dot(h, wg_ref[...]))
    mix = gates[:, :D_MODEL] * y_a + gates[:, D_MODEL:] * y_b
    o_ref[0] = x + _dot(mix.astype(BF16), wmix_ref[...])


def _mixer(x, g, wqkvr, wa, wa2, ba, outg, wu, wsv, ng, nb, sw, sb, wg, wpa, wpb, wmix):
    B, S, D = x.shape
    consts = (g, wqkvr, wa, wa2, ba, outg, wu, wsv, ng, nb, sw, sb, wg, wpa, wpb, wmix)
    return pl.pallas_call(
        _mixer_body,
        grid=(B, S // SEQ_TILE),
        in_specs=[pl.BlockSpec((1, SEQ_TILE, D), lambda b, j: (b, j, 0))]
                 + [_const_spec(c.shape) for c in consts],
        out_specs=pl.BlockSpec((1, SEQ_TILE, D), lambda b, j: (b, j, 0)),
        out_shape=jax.ShapeDtypeStruct((B, S, D), F32),
        scratch_shapes=[pltpu.VMEM((GLA_WIDTH_V, GLA_WIDTH_K), F32)],
        compiler_params=pltpu.CompilerParams(
            dimension_semantics=("arbitrary", "arbitrary"), vmem_limit_bytes=VMEM_LIMIT),
        name="mixer",
    )(x, *consts)


def _xattn_body(x_ref, k_ref, v_ref, gx_ref, wq_ref, wo_ref, gf_ref, wr_hi_ref, wr_lo_ref, br_ref,
                x2_ref, h3_ref, topi_ref, gate_ref, rank_ref, cnt_ref, carry_ref):
    first = (pl.program_id(0) == 0) & (pl.program_id(1) == 0)

    @pl.when(first)
    def _():
        carry_ref[...] = jnp.zeros_like(carry_ref)

    x = x_ref[0]
    ts = x.shape[0]
    h = _rmsnorm(x, gx_ref[...]).astype(BF16)
    q = _dot(h, wq_ref[...]).astype(BF16)
    km = k_ref[0]
    vm = v_ref[0]
    heads = []
    for hh in range(XATTN_HEADS):
        sl = slice(hh * XATTN_DH, (hh + 1) * XATTN_DH)
        s = _dot_nt(q[:, sl], km[:, sl]) * (XATTN_DH ** -0.5)
        s = s - jnp.max(s, axis=-1, keepdims=True)
        e = jnp.exp(s)
        p = e / jnp.sum(e, axis=-1, keepdims=True)
        heads.append(_dot(p.astype(BF16), vm[:, sl]).astype(BF16))
    o = jnp.concatenate(heads, axis=1)
    x2 = x + _dot(o, wo_ref[...])
    x2_ref[0] = x2

    h3 = _rmsnorm(x2, gf_ref[...])
    h3_ref[0] = h3

    h_hi, h_lo = _split_bf16(h3)
    logits = (_dot_nt(wr_hi_ref[...], h_hi) + _dot_nt(wr_hi_ref[...], h_lo)
              + _dot_nt(wr_lo_ref[...], h_hi)) + br_ref[...]

    e_iota = lax.broadcasted_iota(jnp.int32, (N_EXPERTS, ts), 0)
    work = logits
    vals, idxs, hots = [], [], []
    for _ in range(TOP_K):
        m = jnp.max(work, axis=0, keepdims=True)
        idx = jnp.min(jnp.where(work == m, e_iota, N_EXPERTS), axis=0, keepdims=True)
        hot = e_iota == idx
        vals.append(m)
        idxs.append(idx)
        hots.append(hot)
        work = jnp.where(hot, -jnp.inf, work)
    exps = [jnp.exp(vv - vals[0]) for vv in vals]
    denom = exps[0] + exps[1] + exps[2] + exps[3]
    gate_ref[...] = jnp.concatenate([ee / denom for ee in exps], axis=0)
    topi_ref[...] = jnp.concatenate(idxs, axis=0)

    multi = jnp.where(hots[0] | hots[1] | hots[2] | hots[3], 1.0, 0.0)
    srow = lax.broadcasted_iota(jnp.int32, (ts, ts), 0)
    scol = lax.broadcasted_iota(jnp.int32, (ts, ts), 1)
    strict = jnp.where(srow < scol, 1.0, 0.0).astype(BF16)
    before = _dot(multi.astype(BF16), strict) + carry_ref[...]
    ranks = [jnp.sum(jnp.where(hot, before, 0.0), axis=0, keepdims=True) for hot in hots]
    rank_ref[...] = jnp.concatenate(ranks, axis=0).astype(jnp.int32)
    carry = carry_ref[...] + jnp.sum(multi, axis=1, keepdims=True)
    carry_ref[...] = carry
    cnt_ref[...] = carry.astype(jnp.int32)


def _xattn(x1, kmem, vmem, gx, wq, wo, gf, wr_hi, wr_lo, br):
    B, S, D = x1.shape
    T = B * S
    nj = S // XATTN_TILE
    tok_spec = pl.BlockSpec((1, XATTN_TILE, D), lambda b, j: (b, j, 0))
    mem_spec = pl.BlockSpec((1, MEM_LEN, D), lambda b, j: (b, 0, 0))
    lane_spec = pl.BlockSpec((TOP_K, XATTN_TILE), lambda b, j: (0, b * nj + j))
    consts = (gx, wq, wo, gf, wr_hi, wr_lo, br)
    return pl.pallas_call(
        _xattn_body,
        grid=(B, nj),
        in_specs=[tok_spec, mem_spec, mem_spec] + [_const_spec(c.shape) for c in consts],
        out_specs=[tok_spec, tok_spec, lane_spec, lane_spec, lane_spec,
                   pl.BlockSpec((N_EXPERTS, 1), lambda b, j: (0, 0))],
        out_shape=[
            jax.ShapeDtypeStruct((B, S, D), F32),
            jax.ShapeDtypeStruct((B, S, D), F32),
            jax.ShapeDtypeStruct((TOP_K, T), jnp.int32),
            jax.ShapeDtypeStruct((TOP_K, T), F32),
            jax.ShapeDtypeStruct((TOP_K, T), jnp.int32),
            jax.ShapeDtypeStruct((N_EXPERTS, 1), jnp.int32),
        ],
        scratch_shapes=[pltpu.VMEM((N_EXPERTS, 1), F32)],
        compiler_params=pltpu.CompilerParams(
            dimension_semantics=("arbitrary", "arbitrary"), vmem_limit_bytes=VMEM_LIMIT),
        name="xattn_router",
    )(x1, kmem, vmem, *consts)


ROW_GROUP = 8


def _dispatch_body(zs_ref, nu_ref, dest_ref, h_ref, xb_ref, zbuf_ref, zsem, tsem, sem):
    n_blocks = xb_ref.shape[0] // ROW_BLOCK

    def zero_copy(start, zs):
        start = pl.multiple_of(start, ROW_BLOCK)
        return pltpu.make_async_copy(zbuf_ref, xb_ref.at[pl.ds(start, ROW_BLOCK)], zs)

    @pl.when(pl.program_id(0) == 0)
    def _():
        zbuf_ref[...] = jnp.zeros_like(zbuf_ref)
        for e in range(N_EXPERTS):
            zero_copy(zs_ref[e], zsem).start()

        def tail_start(blk, c):
            zero_copy(blk * ROW_BLOCK, tsem).start()
            return c

        lax.fori_loop(nu_ref[0], n_blocks, tail_start, 0)
        for e in range(N_EXPERTS):
            zero_copy(zs_ref[e], zsem).wait()

    def start(g, c):
        base = pl.multiple_of(g * ROW_GROUP, ROW_GROUP)
        tile = h_ref.at[pl.ds(base, ROW_GROUP)]
        for j in range(ROW_GROUP):
            for kk in range(TOP_K):
                d = dest_ref[0, 0, kk * DISPATCH_TILE + base + j]
                pltpu.make_async_copy(tile.at[pl.ds(j, 1)], xb_ref.at[pl.ds(d, 1)], sem).start(
                    priority=kk % 2)
        return c

    lax.fori_loop(0, DISPATCH_TILE // ROW_GROUP, start, 0)
    for kk in range(TOP_K):
        pltpu.make_async_copy(h_ref, xb_ref.at[pl.ds(0, DISPATCH_TILE)], sem).wait()

    @pl.when(pl.program_id(0) == 0)
    def _():
        def tail_wait(blk, c):
            zero_copy(blk * ROW_BLOCK, tsem).wait()
            return c

        lax.fori_loop(nu_ref[0], n_blocks, tail_wait, 0)


def _dispatch(zero_start, n_used, dest_blocks, h3, n_slots):
    T, D = h3.shape
    return pl.pallas_call(
        _dispatch_body,
        grid_spec=pltpu.PrefetchScalarGridSpec(
            num_scalar_prefetch=2,
            grid=(T // DISPATCH_TILE,),
            in_specs=[
                pl.BlockSpec((1, 1, TOP_K * DISPATCH_TILE), lambda i, zs, nu: (i, 0, 0),
                             memory_space=pltpu.SMEM),
                pl.BlockSpec((DISPATCH_TILE, D), lambda i, zs, nu: (i, 0)),
            ],
            out_specs=pl.BlockSpec(memory_space=pl.ANY),
            scratch_shapes=[pltpu.VMEM((ROW_BLOCK, D), F32), pltpu.SemaphoreType.DMA,
                            pltpu.SemaphoreType.DMA, pltpu.SemaphoreType.DMA],
        ),
        out_shape=jax.ShapeDtypeStruct((n_slots, D), F32),
        compiler_params=pltpu.CompilerParams(
            dimension_semantics=("arbitrary",), vmem_limit_bytes=VMEM_LIMIT),
        name="dispatch",
    )(zero_start, n_used, dest_blocks, h3)


SLOT_MAP_TILE = 2048


def _slot_map_body(ps_ref, cnt_ref, pe_ref, dest_ref, inv_ref):
    i = pl.program_id(0)

    @pl.when(i == 0)
    def _():
        def zero(sidx, c):
            inv_ref[sidx] = 0
            return c

        for e in range(N_EXPERTS):
            lax.fori_loop(ps_ref[e] + cnt_ref[e], pe_ref[e], zero, 0)
        lax.fori_loop(pe_ref[N_EXPERTS - 1], inv_ref.shape[0], zero, 0)

    base = i * SLOT_MAP_TILE

    def fill(r, c):
        for kk in range(TOP_K):
            inv_ref[dest_ref[0, 0, kk * SLOT_MAP_TILE + r]] = base + r
        return c

    lax.fori_loop(0, SLOT_MAP_TILE, fill, 0, unroll=8)


def _slot_map(pad_start, counts, pad_end, dest_blocks, n_slots):
    n_tiles = dest_blocks.shape[0]
    return pl.pallas_call(
        _slot_map_body,
        grid_spec=pltpu.PrefetchScalarGridSpec(
            num_scalar_prefetch=3,
            grid=(n_tiles,),
            in_specs=[pl.BlockSpec((1, 1, TOP_K * SLOT_MAP_TILE), lambda i, a, b, c: (i, 0, 0),
                                   memory_space=pltpu.SMEM)],
            out_specs=pl.BlockSpec(memory_space=pltpu.SMEM),
        ),
        out_shape=jax.ShapeDtypeStruct((n_slots,), jnp.int32),
        compiler_params=pltpu.CompilerParams(dimension_semantics=("arbitrary",)),
        name="slot_map",
    )(pad_start, counts, pad_end, dest_blocks)


MLP_CHUNKS = 4


def _expert_mlp(xb, w1b_ref, b1_ref, w2b_ref, b2_ref, between):
    x = xb.astype(BF16)
    width = D_FF // MLP_CHUNKS
    y = None
    for c in range(MLP_CHUNKS):
        between(2 * c)
        gl = slice(c * width, (c + 1) * width)
        ln = slice(D_FF + c * width, D_FF + (c + 1) * width)
        glu = jnp.minimum(_dot(x, w1b_ref[:, gl]) + b1_ref[0][:, gl], SWIGLU_LIMIT)
        lin = jnp.clip(_dot(x, w1b_ref[:, ln]) + b1_ref[0][:, ln], -SWIGLU_LIMIT, SWIGLU_LIMIT)
        act = ((lin + 1.0) * glu * _sigmoid(SWIGLU_ALPHA * glu)).astype(BF16)
        between(2 * c + 1)
        part = _dot(act, w2b_ref[gl, :])
        y = part if y is None else y + part
    return y + b2_ref[0]


def _experts_body(blk_ref, pair_ref, se_ref, run_ref, nxt_ref, ns_ref, nu_ref, inv_ref,
                  h_hbm, w1_hbm, b1_ref, w2_hbm, b2_ref,
                  y_hbm, w1b_ref, w2b_ref, wf1_ref, wf2_ref, ybuf_ref, zbuf_ref, xbuf_ref,
                  sems, wsems, zsem, xsems):
    s = pl.program_id(0)
    n_steps = ns_ref[0]
    last = pl.num_programs(0) - 1
    total_blocks = y_hbm.shape[0] // ROW_BLOCK

    def tail_copy(blk):
        row0 = pl.multiple_of(blk * ROW_BLOCK, ROW_BLOCK)
        return pltpu.make_async_copy(zbuf_ref, y_hbm.at[pl.ds(row0, ROW_BLOCK)], zsem)

    @pl.when(s == 0)
    def _():
        zbuf_ref[...] = jnp.zeros_like(zbuf_ref)

        def go(blk, c):
            tail_copy(blk).start()
            return c

        lax.fori_loop(nu_ref[0], total_blocks, go, 0)

    @pl.when(s == last)
    def _():
        def done(blk, c):
            tail_copy(blk).wait()
            return c

        lax.fori_loop(nu_ref[0], total_blocks, done, 0)

    def out_copy(step, half):
        slot = step % 2
        row0 = pl.multiple_of((blk_ref[step] + half) * ROW_BLOCK, ROW_BLOCK)
        return pltpu.make_async_copy(
            ybuf_ref.at[slot, pl.ds(half * ROW_BLOCK, ROW_BLOCK)],
            y_hbm.at[pl.ds(row0, ROW_BLOCK)], sems.at[slot])

    def wait_step(step):
        out_copy(step, 0).wait()

        @pl.when(pair_ref[step] == 1)
        def _():
            out_copy(step, 1).wait()

    n_groups = 2 * MLP_CHUNKS
    group_rows = 2 * ROW_BLOCK // n_groups

    def gather_start(step, xs, group=None):
        row0 = blk_ref[step] * ROW_BLOCK
        rows = range(2 * ROW_BLOCK) if group is None else range(group * group_rows, (group + 1) * group_rows)
        for r in rows:
            pltpu.make_async_copy(h_hbm.at[pl.ds(inv_ref[row0 + r], 1)], xbuf_ref.at[xs, pl.ds(r, 1)],
                                  xsems.at[xs]).start(priority=r % 2)

    def gather_wait(xs):
        for half in range(2):
            pltpu.make_async_copy(h_hbm.at[pl.ds(0, ROW_BLOCK)],
                                  xbuf_ref.at[xs, pl.ds(half * ROW_BLOCK, ROW_BLOCK)], xsems.at[xs]).wait()

    def weight_copies(e, wslot):
        return (pltpu.make_async_copy(w1_hbm.at[e], wf1_ref.at[wslot], wsems.at[wslot]),
                pltpu.make_async_copy(w2_hbm.at[e], wf2_ref.at[wslot], wsems.at[wslot]))

    @pl.when(s < n_steps)
    def _():
        prev = se_ref[jnp.maximum(s - 1, 0)]
        wslot = run_ref[s] % 2

        @pl.when(s == 0)
        def _():
            for cp in weight_copies(se_ref[0], 0):
                cp.start()

        @pl.when((s == 0) | (se_ref[s] != prev))
        def _():
            for cp in weight_copies(se_ref[s], wslot):
                cp.wait()
            w1b_ref[...] = wf1_ref[wslot].astype(BF16)
            w2b_ref[...] = wf2_ref[wslot].astype(BF16)

            @pl.when(nxt_ref[s] < N_EXPERTS)
            def _():
                for cp in weight_copies(nxt_ref[s], 1 - wslot):
                    cp.start()

        slot = s % 2

        @pl.when(s == 0)
        def _():
            gather_start(0, 0)

        gather_wait(slot)
        following = jnp.minimum(s + 1, n_steps - 1)

        def between(group):
            gather_start(following, 1 - slot, group)

        @pl.when(pair_ref[s] == 1)
        def _():
            ybuf_ref[slot] = _expert_mlp(xbuf_ref[slot], w1b_ref, b1_ref, w2b_ref, b2_ref, between)

        @pl.when(pair_ref[s] == 0)
        def _():
            ybuf_ref[slot, :ROW_BLOCK] = _expert_mlp(xbuf_ref[slot, :ROW_BLOCK], w1b_ref, b1_ref,
                                                     w2b_ref, b2_ref, between)

        @pl.when(s == last)
        def _():
            gather_wait(1 - slot)

    @pl.when(s == n_steps)
    def _():
        gather_wait(s % 2)

    @pl.when((s >= 1) & (s - 1 < n_steps))
    def _():
        wait_step(s - 1)

    @pl.when(s < n_steps)
    def _():
        out_copy(s, 0).start()

        @pl.when(pair_ref[s] == 1)
        def _():
            out_copy(s, 1).start()

        @pl.when(s == last)
        def _():
            wait_step(s)


def _experts(step_blk, step_pair, step_e, step_run, step_next, n_steps, n_used, slot_tok, h3,
             w1, b1, w2, b2):
    n_rows = slot_tok.shape[0]
    D = h3.shape[1]

    def b_map(s, blk, pair, se, run, nxt, ns, nu, inv):
        return (se[s], 0, 0)

    return pl.pallas_call(
        _experts_body,
        grid_spec=pltpu.PrefetchScalarGridSpec(
            num_scalar_prefetch=8,
            grid=(step_blk.shape[0],),
            in_specs=[
                pl.BlockSpec(memory_space=pl.ANY),
                pl.BlockSpec(memory_space=pl.ANY),
                pl.BlockSpec((1, 1, 2 * D_FF), b_map),
                pl.BlockSpec(memory_space=pl.ANY),
                pl.BlockSpec((1, 1, D), b_map),
            ],
            out_specs=pl.BlockSpec(memory_space=pl.ANY),
            scratch_shapes=[pltpu.VMEM((D, 2 * D_FF), BF16), pltpu.VMEM((D_FF, D), BF16),
                            pltpu.VMEM((2, D, 2 * D_FF), F32), pltpu.VMEM((2, D_FF, D), F32),
                            pltpu.VMEM((2, 2 * ROW_BLOCK, D), F32), pltpu.VMEM((ROW_BLOCK, D), F32),
                            pltpu.VMEM((2, 2 * ROW_BLOCK, D), F32),
                            pltpu.SemaphoreType.DMA((2,)), pltpu.SemaphoreType.DMA((2,)),
                            pltpu.SemaphoreType.DMA, pltpu.SemaphoreType.DMA((2,))],
        ),
        out_shape=jax.ShapeDtypeStruct((n_rows, D), F32),
        compiler_params=pltpu.CompilerParams(
            dimension_semantics=("arbitrary",), vmem_limit_bytes=VMEM_LIMIT),
        name="experts",
    )(step_blk, step_pair, step_e, step_run, step_next, n_steps, n_used, slot_tok, h3, w1, b1, w2, b2)


def _combine_body(dest_ref, next_ref, x_ref, gate_ref, g_ref, yb_ref, o_ref, rows_ref, sems):
    i = pl.program_id(0)
    last = pl.num_programs(0) - 1
    slot = i % 2
    n_groups = TOK_TILE // ROW_GROUP

    def start_group(idx_ref, sl, base):
        for kk in range(TOP_K):
            tile = rows_ref.at[sl, kk, pl.ds(base, ROW_GROUP)]
            for j in range(ROW_GROUP):
                d = idx_ref[0, 0, kk * TOK_TILE + base + j]
                pltpu.make_async_copy(yb_ref.at[pl.ds(d, 1)], tile.at[pl.ds(j, 1)],
                                      sems.at[sl]).start(priority=kk % 2)

    def reduce_group(base):
        gates = gate_ref[pl.ds(base, ROW_GROUP), :]
        y = gates[:, 0:1] * rows_ref[slot, 0, pl.ds(base, ROW_GROUP), :]
        for kk in range(1, TOP_K):
            y = y + gates[:, kk:kk + 1] * rows_ref[slot, kk, pl.ds(base, ROW_GROUP), :]
        o_ref[pl.ds(base, ROW_GROUP), :] = x_ref[pl.ds(base, ROW_GROUP), :] + y

    @pl.when(i == 0)
    def _():
        def first(g, c):
            start_group(dest_ref, 0, pl.multiple_of(g * ROW_GROUP, ROW_GROUP))
            return c

        lax.fori_loop(0, n_groups, first, 0)

    for kk in range(TOP_K):
        pltpu.make_async_copy(yb_ref.at[pl.ds(0, TOK_TILE)], rows_ref.at[slot, kk], sems.at[slot]).wait()

    @pl.when(i < last)
    def _():
        def both(g, c):
            base = pl.multiple_of(g * ROW_GROUP, ROW_GROUP)
            start_group(next_ref, 1 - slot, base)
            reduce_group(base)
            return c

        lax.fori_loop(0, n_groups, both, 0)

    @pl.when(i == last)
    def _():
        def only(g, c):
            reduce_group(pl.multiple_of(g * ROW_GROUP, ROW_GROUP))
            return c

        lax.fori_loop(0, n_groups, only, 0)

    o_ref[...] = _rmsnorm(o_ref[...], g_ref[...])


def _combine(dest_blocks, x2, gates_t, g, yb):
    T, D = x2.shape
    n_tiles = T // TOK_TILE
    return pl.pallas_call(
        _combine_body,
        grid=(n_tiles,),
        in_specs=[
            pl.BlockSpec((1, 1, TOP_K * TOK_TILE), lambda i: (i, 0, 0), memory_space=pltpu.SMEM),
            pl.BlockSpec((1, 1, TOP_K * TOK_TILE), lambda i: (jnp.minimum(i + 1, n_tiles - 1), 0, 0),
                         memory_space=pltpu.SMEM),
            pl.BlockSpec((TOK_TILE, D), lambda i: (i, 0)),
            pl.BlockSpec((TOK_TILE, TOP_K), lambda i: (i, 0)),
            _const_spec((1, D)),
            pl.BlockSpec(memory_space=pl.ANY),
        ],
        out_specs=pl.BlockSpec((TOK_TILE, D), lambda i: (i, 0)),
        out_shape=jax.ShapeDtypeStruct((T, D), F32),
        scratch_shapes=[pltpu.VMEM((2, TOP_K, TOK_TILE, D), F32), pltpu.SemaphoreType.DMA((2,))],
        compiler_params=pltpu.CompilerParams(
            dimension_semantics=("arbitrary",), vmem_limit_bytes=VMEM_LIMIT),
        name="combine",
    )(dest_blocks, dest_blocks, x2, gates_t, g, yb)


def _row(v):
    return v.reshape(1, -1).astype(F32)


def _layer(x, kv, p):
    B, S, D = x.shape
    T = B * S
    w_in = p["w_in"]
    wa = jnp.zeros((D, GLA_RANK_PAD), F32).at[:, :GLA_GATE_RANK].set(w_in[:, OFF_A:OFF_U])
    wa2 = jnp.zeros((GLA_RANK_PAD, GLA_WIDTH_K), F32).at[:GLA_GATE_RANK].set(p["gla_w_a2"])
    x1 = _mixer(
        x, _row(p["norm_mix_g"]),
        w_in[:, OFF_Q:OFF_A].astype(BF16), wa.astype(BF16), wa2.astype(BF16),
        _row(p["gla_b_a"]), _row(p["gla_out_g"]),
        w_in[:, OFF_U:OFF_SV].astype(BF16), w_in[:, OFF_SV:OFF_G].astype(BF16),
        _row(p["sgu_norm_g"]), _row(p["sgu_norm_b"]), p["sgu_w"].astype(F32), p["sgu_b"].T.astype(F32),
        w_in[:, OFF_G:].astype(BF16), p["w_proj_a"].astype(BF16), p["w_proj_b"].astype(BF16),
        p["w_mix_out"].astype(BF16))

    kmem, vmem = kv
    wr_t = p["w_router"].T.astype(F32)
    wr_hi = wr_t.astype(BF16)
    wr_lo = (wr_t - wr_hi.astype(F32)).astype(BF16)
    x2, h3, topi, gates, rank, counts = _xattn(
        x1, kmem, vmem, _row(p["norm_x_g"]), p["w_xq"].astype(BF16), p["w_xo"].astype(BF16),
        _row(p["norm_ffn_g"]), wr_hi, wr_lo, p["b_router"].reshape(N_EXPERTS, 1).astype(F32))

    counts = counts.reshape(N_EXPERTS)
    blocks_e = (counts + ROW_BLOCK - 1) // ROW_BLOCK
    padded = blocks_e * ROW_BLOCK
    pad_end = jnp.cumsum(padded)
    pad_start = pad_end - padded
    n_blocks = (T * TOP_K) // ROW_BLOCK + N_EXPERTS
    n_slots = (n_blocks + 1) * ROW_BLOCK
    n_used = (pad_end[-1] // ROW_BLOCK).astype(jnp.int32)
    steps_e = (blocks_e + 1) // 2
    step_end = jnp.cumsum(steps_e)
    n_steps = step_end[-1].astype(jnp.int32)
    max_steps = (n_blocks + N_EXPERTS) // 2
    sidx = jnp.minimum(jnp.arange(max_steps, dtype=jnp.int32), n_steps - 1)
    step_e = jnp.minimum(jnp.sum(step_end[None, :] <= sidx[:, None], axis=1), N_EXPERTS - 1).astype(jnp.int32)
    of_step = step_e[:, None] == jnp.arange(N_EXPERTS, dtype=jnp.int32)[None, :]

    def per_step(v):
        return jnp.sum(jnp.where(of_step, v[None, :], 0), axis=1)

    local = sidx - per_step(step_end - steps_e)
    step_blk = (per_step(pad_start // ROW_BLOCK) + 2 * local).astype(jnp.int32)
    step_pair = (2 * local + 1 < per_step(blocks_e)).astype(jnp.int32)
    ids = jnp.arange(N_EXPERTS, dtype=jnp.int32)
    has_rows = steps_e > 0
    run_e = jnp.cumsum(has_rows.astype(jnp.int32)) - 1
    later = (ids[None, :] > ids[:, None]) & has_rows[None, :]
    next_e = jnp.min(jnp.where(later, ids[None, :], N_EXPERTS), axis=1)
    step_run = per_step(run_e).astype(jnp.int32)
    step_next = per_step(next_e).astype(jnp.int32)
    e_ids = jnp.arange(N_EXPERTS, dtype=jnp.int32)[:, None, None]
    dest = jnp.sum(jnp.where(topi[None] == e_ids, pad_start[:, None, None], 0), axis=0) + rank
    def blocked(tile):
        return (dest.reshape(TOP_K, T // tile, tile).transpose(1, 0, 2)
                .reshape(T // tile, 1, TOP_K * tile).astype(jnp.int32))

    dest_blocks = blocked(TOK_TILE)
    n_used = n_used.reshape(1)
    slot_tok = _slot_map(pad_start.astype(jnp.int32), counts.astype(jnp.int32), pad_end.astype(jnp.int32),
                         blocked(SLOT_MAP_TILE), n_slots)
    yb = _experts(step_blk, step_pair, step_e, step_run, step_next, n_steps.reshape(1), n_used,
                  slot_tok, h3.reshape(T, D),
                  p["w_e1"], p["b_e1"].reshape(N_EXPERTS, 1, 2 * D_FF),
                  p["w_e2"], p["b_e2"].reshape(N_EXPERTS, 1, D))
    return dest_blocks, x2.reshape(T, D), gates.T, yb


def kernel(x, mem, norm_mix_g, w_in, gla_w_a2, gla_b_a, gla_out_g, sgu_norm_g, sgu_norm_b, sgu_w, sgu_b, w_proj_a, w_proj_b, w_mix_out, norm_x_g, norm_mem_g, w_xq, w_xk, w_xv, w_xo, norm_ffn_g, w_router, b_router, w_e1, b_e1, w_e2, b_e2, norm_final_g):
    B, S, D = x.shape
    depth = w_in.shape[0]
    assert depth == 1, "the final norm is fused into the last layer's combine step"
    stacked = dict(norm_mix_g=norm_mix_g, w_in=w_in, gla_w_a2=gla_w_a2, gla_b_a=gla_b_a,
                   gla_out_g=gla_out_g, sgu_norm_g=sgu_norm_g, sgu_norm_b=sgu_norm_b, sgu_w=sgu_w,
                   sgu_b=sgu_b, w_proj_a=w_proj_a, w_proj_b=w_proj_b, w_mix_out=w_mix_out,
                   norm_x_g=norm_x_g, w_xq=w_xq, w_xo=w_xo, norm_ffn_g=norm_ffn_g,
                   w_router=w_router, b_router=b_router, w_e1=w_e1, b_e1=b_e1, w_e2=w_e2, b_e2=b_e2)
    p = {name: v[0] for name, v in stacked.items()}
    kv = _xattn_kv(mem, _row(norm_mem_g[0]), w_xk[0].astype(BF16), w_xv[0].astype(BF16))
    dest_blocks, x2, gates_t, yb = _layer(x, kv, p)
    out = _combine(dest_blocks, x2, gates_t, _row(norm_final_g), yb)
    return out.reshape(B, S, D)
```

```python
import functools

import jax
import jax.numpy as jnp
from jax import lax
from jax.experimental import pallas as pl
from jax.experimental.pallas import tpu as pltpu

F32 = jnp.float32
BF16 = jnp.bfloat16

D_MODEL = 1024
MEM_LEN = 256
EPS = 1e-5

GLA_HEADS = 4
GLA_DV = 128
GLA_DK = 64
GLA_WIDTH_K = GLA_HEADS * GLA_DK
GLA_WIDTH_V = GLA_HEADS * GLA_DV
GLA_GATE_RANK = 16
GLA_RANK_PAD = 128
GLA_TAU = 16.0
GLA_CHUNK = 64

SGU_WIDTH = 512
SGU_GROUPS = 4
SGU_GROUP_DIM = SGU_WIDTH // SGU_GROUPS
SGU_CHUNK = 128

OFF_Q = 0
OFF_K = OFF_Q + GLA_WIDTH_K
OFF_V = OFF_K + GLA_WIDTH_K
OFF_R = OFF_V + GLA_WIDTH_V
OFF_A = OFF_R + GLA_WIDTH_V
OFF_U = OFF_A + GLA_GATE_RANK
OFF_SV = OFF_U + SGU_WIDTH
OFF_G = OFF_SV + SGU_WIDTH

XATTN_HEADS = 4
XATTN_DH = D_MODEL // XATTN_HEADS

N_EXPERTS = 32
TOP_K = 4
D_FF = D_MODEL
SWIGLU_ALPHA = 1.702
SWIGLU_LIMIT = 7.0

SEQ_TILE = 512
XATTN_TILE = 1024
ROW_BLOCK = 256
TOK_TILE = 512
DISPATCH_TILE = 1024
VMEM_LIMIT = 56 * 1024 * 1024


def _rmsnorm(x, g):
    return x * lax.rsqrt(jnp.mean(x * x, axis=-1, keepdims=True) + EPS) * g


def _sigmoid(x):
    return 1.0 / (1.0 + jnp.exp(-x))


def _log_sigmoid(x):
    return jnp.minimum(x, 0.0) - jnp.log1p(jnp.exp(-jnp.abs(x)))


def _gelu(x):
    return 0.5 * x * (1.0 + lax.erf(x * (2.0 ** -0.5)))


def _dot(a, b):
    return jnp.dot(a, b, preferred_element_type=F32)


def _dot_nt(a, b):
    return lax.dot_general(a, b, (((1,), (1,)), ((), ())), preferred_element_type=F32)


def _dot_tn(a, b):
    return lax.dot_general(a, b, (((0,), (0,)), ((), ())), preferred_element_type=F32)


def _split_bf16(x):
    hi = x.astype(BF16)
    lo = (x - hi.astype(F32)).astype(BF16)
    return hi, lo


def _const_spec(shape):
    zeros = (0,) * len(shape)
    return pl.BlockSpec(shape, lambda *_: zeros, pipeline_mode=pl.Buffered(1))


def _xattn_kv_body(mem_ref, g_ref, wk_ref, wv_ref, k_ref, v_ref):
    m = _rmsnorm(mem_ref[0], g_ref[...]).astype(BF16)
    k_ref[0] = _dot(m, wk_ref[...]).astype(BF16)
    v_ref[0] = _dot(m, wv_ref[...]).astype(BF16)


def _xattn_kv(mem, g, wk, wv):
    B = mem.shape[0]
    return pl.pallas_call(
        _xattn_kv_body,
        grid=(B,),
        in_specs=[
            pl.BlockSpec((1, MEM_LEN, D_MODEL), lambda b: (b, 0, 0)),
            _const_spec((1, D_MODEL)),
            _const_spec((D_MODEL, D_MODEL)),
            _const_spec((D_MODEL, D_MODEL)),
        ],
        out_specs=[
            pl.BlockSpec((1, MEM_LEN, D_MODEL), lambda b: (b, 0, 0)),
            pl.BlockSpec((1, MEM_LEN, D_MODEL), lambda b: (b, 0, 0)),
        ],
        out_shape=[jax.ShapeDtypeStruct((B, MEM_LEN, D_MODEL), BF16)] * 2,
        compiler_params=pltpu.CompilerParams(
            dimension_semantics=("arbitrary",), vmem_limit_bytes=VMEM_LIMIT),
        name="xattn_kv",
    )(mem, g, wk, wv)


def _gla_branch(h, wqkvr_ref, wa_ref, wa2_ref, ba_ref, outg_ref, st_ref):
    ts = h.shape[0]
    n_chunks = ts // GLA_CHUNK
    p = _dot(h, wqkvr_ref[...])
    q = p[:, OFF_Q:OFF_K] * (GLA_DK ** -0.5)
    k = p[:, OFF_K:OFF_V]
    v = p[:, OFF_V:OFF_R].astype(BF16)
    r = p[:, OFF_R:OFF_A]

    a_low = _dot(h, wa_ref[...]).astype(BF16)
    log_a = _log_sigmoid(_dot(a_low, wa2_ref[...]) + ba_ref[...]) * (1.0 / GLA_TAU)

    row = lax.broadcasted_iota(jnp.int32, (ts, ts), 0)
    col = lax.broadcasted_iota(jnp.int32, (ts, ts), 1)
    tri = jnp.where((col <= row) & ((col >> 6) == (row >> 6)), 1.0, 0.0).astype(BF16)
    la_hi, la_lo = _split_bf16(log_a)
    cum = _dot(tri, la_hi) + _dot(tri, la_lo)

    q_in = (q * jnp.exp(cum)).astype(BF16)
    k_in = (k * jnp.exp(-cum)).astype(BF16)

    lane_k = lax.broadcasted_iota(jnp.int32, (1, GLA_WIDTH_K), 1)
    head_masks = [jnp.where((lane_k >> 6) == hh, 1.0, 0.0).astype(BF16) for hh in range(GLA_HEADS)]
    lane_v = lax.broadcasted_iota(jnp.int32, (1, GLA_WIDTH_V), 1)
    value_masks = [jnp.where((lane_v >> 7) == hh, 1.0, 0.0).astype(BF16) for hh in range(GLA_HEADS)]
    r4 = lax.broadcasted_iota(jnp.int32, (GLA_CHUNK, GLA_HEADS * GLA_CHUNK), 0)
    c4 = lax.broadcasted_iota(jnp.int32, (GLA_CHUNK, GLA_HEADS * GLA_CHUNK), 1)
    causal4 = (c4 & (GLA_CHUNK - 1)) <= r4
    sr = lax.broadcasted_iota(jnp.int32, (GLA_WIDTH_V, GLA_WIDTH_K), 0)
    sc = lax.broadcasted_iota(jnp.int32, (GLA_WIDTH_V, GLA_WIDTH_K), 1)
    state_mask = jnp.where((sr >> 7) == (sc >> 6), 1.0, 0.0)

    state = st_ref[...]
    outs = []
    for n in range(n_chunks):
        lo, hi = n * GLA_CHUNK, (n + 1) * GLA_CHUNK
        cum_c = cum[lo:hi]
        last = cum[hi - 1:hi]
        q_c = q_in[lo:hi]
        k_c = k_in[lo:hi]
        v_c = v[lo:hi]
        k_out = (k[lo:hi] * jnp.exp(last - cum_c)).astype(BF16)
        k4 = jnp.concatenate([k_c * head_masks[hh] for hh in range(GLA_HEADS)], axis=0)
        v4 = jnp.concatenate([v_c * value_masks[hh] for hh in range(GLA_HEADS)], axis=0)
        att4 = jnp.where(causal4, _dot_nt(q_c, k4), 0.0).astype(BF16)
        o_intra = _dot(att4, v4)
        o_inter = _dot_nt(q_c, state.astype(BF16))
        outs.append(o_intra + o_inter)
        state = state * jnp.exp(last) + _dot_tn(v_c, k_out) * state_mask
    st_ref[...] = state

    o = jnp.concatenate(outs, axis=0)
    outg = outg_ref[...]
    normed = []
    for hh in range(GLA_HEADS):
        sl = slice(hh * GLA_DV, (hh + 1) * GLA_DV)
        normed.append(_rmsnorm(o[:, sl], outg[:, sl]))
    o = jnp.concatenate(normed, axis=1)
    return o * (r * _sigmoid(r))


def _sgu_branch(h, wu_ref, wsv_ref, ng_ref, nb_ref, sw_ref, sb_ref):
    ts = h.shape[0]
    n_chunks = ts // SGU_CHUNK
    u = _gelu(_dot(h, wu_ref[...]))
    v = _gelu(_dot(h, wsv_ref[...]))
    mu = jnp.mean(v, axis=-1, keepdims=True)
    vc = v - mu
    var = jnp.mean(vc * vc, axis=-1, keepdims=True)
    v = (vc * lax.rsqrt(var + EPS) * ng_ref[...] + nb_ref[...]).astype(BF16)

    row = lax.broadcasted_iota(jnp.int32, (SGU_CHUNK, SGU_CHUNK), 0)
    col = lax.broadcasted_iota(jnp.int32, (SGU_CHUNK, SGU_CHUNK), 1)
    causal = col <= row
    sb = sb_ref[...]
    per_group = []
    for g in range(SGU_GROUPS):
        w = jnp.where(causal, sw_ref[g], 0.0).astype(BF16)
        gs = slice(g * SGU_GROUP_DIM, (g + 1) * SGU_GROUP_DIM)
        vcat = jnp.concatenate(
            [v[c * SGU_CHUNK:(c + 1) * SGU_CHUNK, gs] for c in range(n_chunks)], axis=1)
        per_group.append(_dot(w, vcat) + sb[:, g:g + 1])
    rows = []
    for c in range(n_chunks):
        cs = slice(c * SGU_GROUP_DIM, (c + 1) * SGU_GROUP_DIM)
        rows.append(jnp.concatenate([per_group[g][:, cs] for g in range(SGU_GROUPS)], axis=1))
    mixed = jnp.concatenate(rows, axis=0)
    return u * mixed


def _mixer_body(x_ref, g_ref, wqkvr_ref, wa_ref, wa2_ref, ba_ref, outg_ref,
                wu_ref, wsv_ref, ng_ref, nb_ref, sw_ref, sb_ref,
                wg_ref, wpa_ref, wpb_ref, wmix_ref, o_ref, st_ref):
    @pl.when(pl.program_id(1) == 0)
    def _():
        st_ref[...] = jnp.zeros_like(st_ref)

    x = x_ref[0]
    h = _rmsnorm(x, g_ref[...]).astype(BF16)
    ya_in = _gla_branch(h, wqkvr_ref, wa_ref, wa2_ref, ba_ref, outg_ref, st_ref)
    y_a = _dot(ya_in.astype(BF16), wpa_ref[...])
    yb_in = _sgu_branch(h, wu_ref, wsv_ref, ng_ref, nb_ref, sw_ref, sb_ref)
    y_b = _dot(yb_in.astype(BF16), wpb_ref[...])
    gates = _sigmoid(_dot(h, wg_ref[...]))
    mix = gates[:, :D_MODEL] * y_a + gates[:, D_MODEL:] * y_b
    o_ref[0] = x + _dot(mix.astype(BF16), wmix_ref[...])


def _mixer(x, g, wqkvr, wa, wa2, ba, outg, wu, wsv, ng, nb, sw, sb, wg, wpa, wpb, wmix):
    B, S, D = x.shape
    consts = (g, wqkvr, wa, wa2, ba, outg, wu, wsv, ng, nb, sw, sb, wg, wpa, wpb, wmix)
    return pl.pallas_call(
        _mixer_body,
        grid=(B, S // SEQ_TILE),
        in_specs=[pl.BlockSpec((1, SEQ_TILE, D), lambda b, j: (b, j, 0))]
                 + [_const_spec(c.shape) for c in consts],
        out_specs=pl.BlockSpec((1, SEQ_TILE, D), lambda b, j: (b, j, 0)),
        out_shape=jax.ShapeDtypeStruct((B, S, D), F32),
        scratch_shapes=[pltpu.VMEM((GLA_WIDTH_V, GLA_WIDTH_K), F32)],
        compiler_params=pltpu.CompilerParams(
            dimension_semantics=("arbitrary", "arbitrary"), vmem_limit_bytes=VMEM_LIMIT),
        name="mixer",
    )(x, *consts)


def _xattn_body(x_ref, mem_ref, gm_ref, wk_ref, wv_ref, gx_ref, wq_ref, wo_ref, gf_ref,
                wr_hi_ref, wr_lo_ref, br_ref,
                x2_ref, h3_ref, topi_ref, gate_ref, rank_ref, cnt_ref, carry_ref, k_ref, v_ref):
    first = (pl.program_id(0) == 0) & (pl.program_id(1) == 0)

    @pl.when(first)
    def _():
        carry_ref[...] = jnp.zeros_like(carry_ref)

    @pl.when(pl.program_id(1) == 0)
    def _():
        m = _rmsnorm(mem_ref[0], gm_ref[...]).astype(BF16)
        k_ref[0] = _dot(m, wk_ref[...]).astype(BF16)
        v_ref[0] = _dot(m, wv_ref[...]).astype(BF16)

    x = x_ref[0]
    ts = x.shape[0]
    h = _rmsnorm(x, gx_ref[...]).astype(BF16)
    q = _dot(h, wq_ref[...]).astype(BF16)
    km = k_ref[0]
    vm = v_ref[0]
    heads = []
    for hh in range(XATTN_HEADS):
        sl = slice(hh * XATTN_DH, (hh + 1) * XATTN_DH)
        s = _dot_nt(q[:, sl], km[:, sl]) * (XATTN_DH ** -0.5)
        s = s - jnp.max(s, axis=-1, keepdims=True)
        e = jnp.exp(s)
        p = e / jnp.sum(e, axis=-1, keepdims=True)
        heads.append(_dot(p.astype(BF16), vm[:, sl]).astype(BF16))
    o = jnp.concatenate(heads, axis=1)
    x2 = x + _dot(o, wo_ref[...])
    x2_ref[0] = x2

    h3 = _rmsnorm(x2, gf_ref[...])
    h3_ref[0] = h3

    h_hi, h_lo = _split_bf16(h3)
    logits = (_dot_nt(wr_hi_ref[...], h_hi) + _dot_nt(wr_hi_ref[...], h_lo)
              + _dot_nt(wr_lo_ref[...], h_hi)) + br_ref[...]

    e_iota = lax.broadcasted_iota(jnp.int32, (N_EXPERTS, ts), 0)
    work = logits
    vals, idxs, hots = [], [], []
    for _ in range(TOP_K):
        m = jnp.max(work, axis=0, keepdims=True)
        idx = jnp.min(jnp.where(work == m, e_iota, N_EXPERTS), axis=0, keepdims=True)
        hot = e_iota == idx
        vals.append(m)
        idxs.append(idx)
        hots.append(hot)
        work = jnp.where(hot, -jnp.inf, work)
    exps = [jnp.exp(vv - vals[0]) for vv in vals]
    denom = exps[0] + exps[1] + exps[2] + exps[3]
    gate_ref[...] = jnp.concatenate([ee / denom for ee in exps], axis=0)
    topi_ref[...] = jnp.concatenate(idxs, axis=0)

    multi = jnp.where(hots[0] | hots[1] | hots[2] | hots[3], 1.0, 0.0)
    srow = lax.broadcasted_iota(jnp.int32, (ts, ts), 0)
    scol = lax.broadcasted_iota(jnp.int32, (ts, ts), 1)
    strict = jnp.where(srow < scol, 1.0, 0.0).astype(BF16)
    before = _dot(multi.astype(BF16), strict) + carry_ref[...]
    ranks = [jnp.sum(jnp.where(hot, before, 0.0), axis=0, keepdims=True) for hot in hots]
    rank_ref[...] = jnp.concatenate(ranks, axis=0).astype(jnp.int32)
    carry = carry_ref[...] + jnp.sum(multi, axis=1, keepdims=True)
    carry_ref[...] = carry
    cnt_ref[...] = carry.astype(jnp.int32)


def _xattn(x1, mem, gm, wk, wv, gx, wq, wo, gf, wr_hi, wr_lo, br):
    B, S, D = x1.shape
    T = B * S
    nj = S // XATTN_TILE
    tok_spec = pl.BlockSpec((1, XATTN_TILE, D), lambda b, j: (b, j, 0))
    mem_spec = pl.BlockSpec((1, MEM_LEN, D), lambda b, j: (b, 0, 0))
    lane_spec = pl.BlockSpec((TOP_K, XATTN_TILE), lambda b, j: (0, b * nj + j))
    consts = (gm, wk, wv, gx, wq, wo, gf, wr_hi, wr_lo, br)
    return pl.pallas_call(
        _xattn_body,
        grid=(B, nj),
        in_specs=[tok_spec, mem_spec] + [_const_spec(c.shape) for c in consts],
        out_specs=[tok_spec, tok_spec, lane_spec, lane_spec, lane_spec,
                   pl.BlockSpec((N_EXPERTS, 1), lambda b, j: (0, 0))],
        out_shape=[
            jax.ShapeDtypeStruct((B, S, D), F32),
            jax.ShapeDtypeStruct((B, S, D), F32),
            jax.ShapeDtypeStruct((TOP_K, T), jnp.int32),
            jax.ShapeDtypeStruct((TOP_K, T), F32),
            jax.ShapeDtypeStruct((TOP_K, T), jnp.int32),
            jax.ShapeDtypeStruct((N_EXPERTS, 1), jnp.int32),
        ],
        scratch_shapes=[pltpu.VMEM((N_EXPERTS, 1), F32),
                        pltpu.VMEM((1, MEM_LEN, D), BF16), pltpu.VMEM((1, MEM_LEN, D), BF16)],
        compiler_params=pltpu.CompilerParams(
            dimension_semantics=("arbitrary", "arbitrary"), vmem_limit_bytes=VMEM_LIMIT),
        name="xattn_router",
    )(x1, mem, *consts)


ROW_GROUP = 8


def _dispatch_body(zs_ref, nu_ref, dest_ref, h_ref, xb_ref, zbuf_ref, zsem, tsem, sem):
    n_blocks = xb_ref.shape[0] // ROW_BLOCK

    def zero_copy(start, zs):
        start = pl.multiple_of(start, ROW_BLOCK)
        return pltpu.make_async_copy(zbuf_ref, xb_ref.at[pl.ds(start, ROW_BLOCK)], zs)

    @pl.when(pl.program_id(0) == 0)
    def _():
        zbuf_ref[...] = jnp.zeros_like(zbuf_ref)
        for e in range(N_EXPERTS):
            zero_copy(zs_ref[e], zsem).start()

        def tail_start(blk, c):
            zero_copy(blk * ROW_BLOCK, tsem).start()
            return c

        lax.fori_loop(nu_ref[0], n_blocks, tail_start, 0)
        for e in range(N_EXPERTS):
            zero_copy(zs_ref[e], zsem).wait()

    def start(g, c):
        base = pl.multiple_of(g * ROW_GROUP, ROW_GROUP)
        tile = h_ref.at[pl.ds(base, ROW_GROUP)]
        for j in range(ROW_GROUP):
            for kk in range(TOP_K):
                d = dest_ref[0, 0, kk * DISPATCH_TILE + base + j]
                pltpu.make_async_copy(tile.at[pl.ds(j, 1)], xb_ref.at[pl.ds(d, 1)], sem).start(
                    priority=kk % 2)
        return c

    lax.fori_loop(0, DISPATCH_TILE // ROW_GROUP, start, 0)
    for kk in range(TOP_K):
        pltpu.make_async_copy(h_ref, xb_ref.at[pl.ds(0, DISPATCH_TILE)], sem).wait()

    @pl.when(pl.program_id(0) == 0)
    def _():
        def tail_wait(blk, c):
            zero_copy(blk * ROW_BLOCK, tsem).wait()
            return c

        lax.fori_loop(nu_ref[0], n_blocks, tail_wait, 0)


def _dispatch(zero_start, n_used, dest_blocks, h3, n_slots):
    T, D = h3.shape
    return pl.pallas_call(
        _dispatch_body,
        grid_spec=pltpu.PrefetchScalarGridSpec(
            num_scalar_prefetch=2,
            grid=(T // DISPATCH_TILE,),
            in_specs=[
                pl.BlockSpec((1, 1, TOP_K * DISPATCH_TILE), lambda i, zs, nu: (i, 0, 0),
                             memory_space=pltpu.SMEM),
                pl.BlockSpec((DISPATCH_TILE, D), lambda i, zs, nu: (i, 0)),
            ],
            out_specs=pl.BlockSpec(memory_space=pl.ANY),
            scratch_shapes=[pltpu.VMEM((ROW_BLOCK, D), F32), pltpu.SemaphoreType.DMA,
                            pltpu.SemaphoreType.DMA, pltpu.SemaphoreType.DMA],
        ),
        out_shape=jax.ShapeDtypeStruct((n_slots, D), F32),
        compiler_params=pltpu.CompilerParams(
            dimension_semantics=("arbitrary",), vmem_limit_bytes=VMEM_LIMIT),
        name="dispatch",
    )(zero_start, n_used, dest_blocks, h3)


def _expert_mlp(xb, w1b_ref, b1_ref, w2b_ref, b2_ref):
    hcat = _dot(xb.astype(BF16), w1b_ref[...]) + b1_ref[0]
    glu = jnp.minimum(hcat[:, :D_FF], SWIGLU_LIMIT)
    lin = jnp.clip(hcat[:, D_FF:], -SWIGLU_LIMIT, SWIGLU_LIMIT)
    act = (lin + 1.0) * glu * _sigmoid(SWIGLU_ALPHA * glu)
    return _dot(act.astype(BF16), w2b_ref[...]) + b2_ref[0]


def _experts_body(blk_ref, pair_ref, se_ref, run_ref, nxt_ref, ns_ref, nu_ref,
                  x_ref, w1_hbm, b1_ref, w2_hbm, b2_ref,
                  y_hbm, w1b_ref, w2b_ref, wf1_ref, wf2_ref, ybuf_ref, zbuf_ref, sems, wsems, zsem):
    s = pl.program_id(0)
    n_steps = ns_ref[0]
    last = pl.num_programs(0) - 1
    total_blocks = y_hbm.shape[0] // ROW_BLOCK

    def tail_copy(blk):
        row0 = pl.multiple_of(blk * ROW_BLOCK, ROW_BLOCK)
        return pltpu.make_async_copy(zbuf_ref, y_hbm.at[pl.ds(row0, ROW_BLOCK)], zsem)

    @pl.when(s == 0)
    def _():
        zbuf_ref[...] = jnp.zeros_like(zbuf_ref)

        def go(blk, c):
            tail_copy(blk).start()
            return c

        lax.fori_loop(nu_ref[0], total_blocks, go, 0)

    @pl.when(s == last)
    def _():
        def done(blk, c):
            tail_copy(blk).wait()
            return c

        lax.fori_loop(nu_ref[0], total_blocks, done, 0)

    def out_copy(step, half):
        slot = step % 2
        row0 = pl.multiple_of((blk_ref[step] + half) * ROW_BLOCK, ROW_BLOCK)
        return pltpu.make_async_copy(
            ybuf_ref.at[slot, pl.ds(half * ROW_BLOCK, ROW_BLOCK)],
            y_hbm.at[pl.ds(row0, ROW_BLOCK)], sems.at[slot])

    def wait_step(step):
        out_copy(step, 0).wait()

        @pl.when(pair_ref[step] == 1)
        def _():
            out_copy(step, 1).wait()

    def weight_copies(e, wslot):
        return (pltpu.make_async_copy(w1_hbm.at[e], wf1_ref.at[wslot], wsems.at[wslot]),
                pltpu.make_async_copy(w2_hbm.at[e], wf2_ref.at[wslot], wsems.at[wslot]))

    @pl.when(s < n_steps)
    def _():
        prev = se_ref[jnp.maximum(s - 1, 0)]
        wslot = run_ref[s] % 2

        @pl.when(s == 0)
        def _():
            for cp in weight_copies(se_ref[0], 0):
                cp.start()

        @pl.when((s == 0) | (se_ref[s] != prev))
        def _():
            for cp in weight_copies(se_ref[s], wslot):
                cp.wait()
            w1b_ref[...] = wf1_ref[wslot].astype(BF16)
            w2b_ref[...] = wf2_ref[wslot].astype(BF16)

            @pl.when(nxt_ref[s] < N_EXPERTS)
            def _():
                for cp in weight_copies(nxt_ref[s], 1 - wslot):
                    cp.start()

        slot = s % 2

        @pl.when(pair_ref[s] == 1)
        def _():
            ybuf_ref[slot] = _expert_mlp(x_ref[...], w1b_ref, b1_ref, w2b_ref, b2_ref)

        @pl.when(pair_ref[s] == 0)
        def _():
            ybuf_ref[slot, :ROW_BLOCK] = _expert_mlp(x_ref[:ROW_BLOCK], w1b_ref, b1_ref, w2b_ref, b2_ref)

    @pl.when((s >= 1) & (s - 1 < n_steps))
    def _():
        wait_step(s - 1)

    @pl.when(s < n_steps)
    def _():
        out_copy(s, 0).start()

        @pl.when(pair_ref[s] == 1)
        def _():
            out_copy(s, 1).start()

        @pl.when(s == last)
        def _():
            wait_step(s)


def _experts(step_blk, step_pair, step_e, step_run, step_next, n_steps, n_used, xb, w1, b1, w2, b2):
    n_rows, D = xb.shape

    def x_map(s, blk, pair, se, run, nxt, ns, nu):
        return (blk[s] * ROW_BLOCK, 0)

    def b_map(s, blk, pair, se, run, nxt, ns, nu):
        return (se[s], 0, 0)

    return pl.pallas_call(
        _experts_body,
        grid_spec=pltpu.PrefetchScalarGridSpec(
            num_scalar_prefetch=7,
            grid=(step_blk.shape[0],),
            in_specs=[
                pl.BlockSpec((pl.Element(2 * ROW_BLOCK), pl.Element(D)), x_map),
                pl.BlockSpec(memory_space=pl.ANY),
                pl.BlockSpec((1, 1, 2 * D_FF), b_map),
                pl.BlockSpec(memory_space=pl.ANY),
                pl.BlockSpec((1, 1, D), b_map),
            ],
            out_specs=pl.BlockSpec(memory_space=pl.ANY),
            scratch_shapes=[pltpu.VMEM((D, 2 * D_FF), BF16), pltpu.VMEM((D_FF, D), BF16),
                            pltpu.VMEM((2, D, 2 * D_FF), F32), pltpu.VMEM((2, D_FF, D), F32),
                            pltpu.VMEM((2, 2 * ROW_BLOCK, D), F32), pltpu.VMEM((ROW_BLOCK, D), F32),
                            pltpu.SemaphoreType.DMA((2,)), pltpu.SemaphoreType.DMA((2,)),
                            pltpu.SemaphoreType.DMA],
        ),
        out_shape=jax.ShapeDtypeStruct((n_rows, D), F32),
        compiler_params=pltpu.CompilerParams(
            dimension_semantics=("arbitrary",), vmem_limit_bytes=VMEM_LIMIT),
        name="experts",
    )(step_blk, step_pair, step_e, step_run, step_next, n_steps, n_used, xb, w1, b1, w2, b2)


def _combine_body(dest_ref, next_ref, x_ref, gate_ref, g_ref, yb_ref, o_ref, rows_ref, sems):
    i = pl.program_id(0)
    last = pl.num_programs(0) - 1
    slot = i % 2
    n_groups = TOK_TILE // ROW_GROUP

    def start_group(idx_ref, sl, base):
        for kk in range(TOP_K):
            tile = rows_ref.at[sl, kk, pl.ds(base, ROW_GROUP)]
            for j in range(ROW_GROUP):
                d = idx_ref[0, 0, kk * TOK_TILE + base + j]
                pltpu.make_async_copy(yb_ref.at[pl.ds(d, 1)], tile.at[pl.ds(j, 1)],
                                      sems.at[sl]).start(priority=kk % 2)

    def reduce_group(base):
        gates = gate_ref[pl.ds(base, ROW_GROUP), :]
        y = gates[:, 0:1] * rows_ref[slot, 0, pl.ds(base, ROW_GROUP), :]
        for kk in range(1, TOP_K):
            y = y + gates[:, kk:kk + 1] * rows_ref[slot, kk, pl.ds(base, ROW_GROUP), :]
        o_ref[pl.ds(base, ROW_GROUP), :] = x_ref[pl.ds(base, ROW_GROUP), :] + y

    @pl.when(i == 0)
    def _():
        def first(g, c):
            start_group(dest_ref, 0, pl.multiple_of(g * ROW_GROUP, ROW_GROUP))
            return c

        lax.fori_loop(0, n_groups, first, 0)

    for kk in range(TOP_K):
        pltpu.make_async_copy(yb_ref.at[pl.ds(0, TOK_TILE)], rows_ref.at[slot, kk], sems.at[slot]).wait()

    @pl.when(i < last)
    def _():
        def both(g, c):
            base = pl.multiple_of(g * ROW_GROUP, ROW_GROUP)
            start_group(next_ref, 1 - slot, base)
            reduce_group(base)
            return c

        lax.fori_loop(0, n_groups, both, 0)

    @pl.when(i == last)
    def _():
        def only(g, c):
            reduce_group(pl.multiple_of(g * ROW_GROUP, ROW_GROUP))
            return c

        lax.fori_loop(0, n_groups, only, 0)

    o_ref[...] = _rmsnorm(o_ref[...], g_ref[...])


def _combine(dest_blocks, x2, gates_t, g, yb):
    T, D = x2.shape
    n_tiles = T // TOK_TILE
    return pl.pallas_call(
        _combine_body,
        grid=(n_tiles,),
        in_specs=[
            pl.BlockSpec((1, 1, TOP_K * TOK_TILE), lambda i: (i, 0, 0), memory_space=pltpu.SMEM),
            pl.BlockSpec((1, 1, TOP_K * TOK_TILE), lambda i: (jnp.minimum(i + 1, n_tiles - 1), 0, 0),
                         memory_space=pltpu.SMEM),
            pl.BlockSpec((TOK_TILE, D), lambda i: (i, 0)),
            pl.BlockSpec((TOK_TILE, TOP_K), lambda i: (i, 0)),
            _const_spec((1, D)),
            pl.BlockSpec(memory_space=pl.ANY),
        ],
        out_specs=pl.BlockSpec((TOK_TILE, D), lambda i: (i, 0)),
        out_shape=jax.ShapeDtypeStruct((T, D), F32),
        scratch_shapes=[pltpu.VMEM((2, TOP_K, TOK_TILE, D), F32), pltpu.SemaphoreType.DMA((2,))],
        compiler_params=pltpu.CompilerParams(
            dimension_semantics=("arbitrary",), vmem_limit_bytes=VMEM_LIMIT),
        name="combine",
    )(dest_blocks, dest_blocks, x2, gates_t, g, yb)


def _row(v):
    return v.reshape(1, -1).astype(F32)


def _layer(x, kv, p):
    B, S, D = x.shape
    T = B * S
    w_in = p["w_in"]
    wa = jnp.zeros((D, GLA_RANK_PAD), F32).at[:, :GLA_GATE_RANK].set(w_in[:, OFF_A:OFF_U])
    wa2 = jnp.zeros((GLA_RANK_PAD, GLA_WIDTH_K), F32).at[:GLA_GATE_RANK].set(p["gla_w_a2"])
    x1 = _mixer(
        x, _row(p["norm_mix_g"]),
        w_in[:, OFF_Q:OFF_A].astype(BF16), wa.astype(BF16), wa2.astype(BF16),
        _row(p["gla_b_a"]), _row(p["gla_out_g"]),
        w_in[:, OFF_U:OFF_SV].astype(BF16), w_in[:, OFF_SV:OFF_G].astype(BF16),
        _row(p["sgu_norm_g"]), _row(p["sgu_norm_b"]), p["sgu_w"].astype(F32), p["sgu_b"].T.astype(F32),
        w_in[:, OFF_G:].astype(BF16), p["w_proj_a"].astype(BF16), p["w_proj_b"].astype(BF16),
        p["w_mix_out"].astype(BF16))

    mem, gm, wk, wv = kv
    wr_t = p["w_router"].T.astype(F32)
    wr_hi = wr_t.astype(BF16)
    wr_lo = (wr_t - wr_hi.astype(F32)).astype(BF16)
    x2, h3, topi, gates, rank, counts = _xattn(
        x1, mem, gm, wk, wv, _row(p["norm_x_g"]), p["w_xq"].astype(BF16), p["w_xo"].astype(BF16),
        _row(p["norm_ffn_g"]), wr_hi, wr_lo, p["b_router"].reshape(N_EXPERTS, 1).astype(F32))

    counts = counts.reshape(N_EXPERTS)
    blocks_e = (counts + ROW_BLOCK - 1) // ROW_BLOCK
    padded = blocks_e * ROW_BLOCK
    pad_end = jnp.cumsum(padded)
    pad_start = pad_end - padded
    n_blocks = (T * TOP_K) // ROW_BLOCK + N_EXPERTS
    n_slots = (n_blocks + 1) * ROW_BLOCK
    n_used = (pad_end[-1] // ROW_BLOCK).astype(jnp.int32)
    steps_e = (blocks_e + 1) // 2
    step_end = jnp.cumsum(steps_e)
    n_steps = step_end[-1].astype(jnp.int32)
    max_steps = (n_blocks + N_EXPERTS) // 2
    sidx = jnp.minimum(jnp.arange(max_steps, dtype=jnp.int32), n_steps - 1)
    step_e = jnp.minimum(jnp.sum(step_end[None, :] <= sidx[:, None], axis=1), N_EXPERTS - 1).astype(jnp.int32)
    of_step = step_e[:, None] == jnp.arange(N_EXPERTS, dtype=jnp.int32)[None, :]

    def per_step(v):
        return jnp.sum(jnp.where(of_step, v[None, :], 0), axis=1)

    local = sidx - per_step(step_end - steps_e)
    step_blk = (per_step(pad_start // ROW_BLOCK) + 2 * local).astype(jnp.int32)
    step_pair = (2 * local + 1 < per_step(blocks_e)).astype(jnp.int32)
    ids = jnp.arange(N_EXPERTS, dtype=jnp.int32)
    has_rows = steps_e > 0
    run_e = jnp.cumsum(has_rows.astype(jnp.int32)) - 1
    later = (ids[None, :] > ids[:, None]) & has_rows[None, :]
    next_e = jnp.min(jnp.where(later, ids[None, :], N_EXPERTS), axis=1)
    step_run = per_step(run_e).astype(jnp.int32)
    step_next = per_step(next_e).astype(jnp.int32)
    e_ids = jnp.arange(N_EXPERTS, dtype=jnp.int32)[:, None, None]
    dest = jnp.sum(jnp.where(topi[None] == e_ids, pad_start[:, None, None], 0), axis=0) + rank
    def blocked(tile):
        return (dest.reshape(TOP_K, T // tile, tile).transpose(1, 0, 2)
                .reshape(T // tile, 1, TOP_K * tile).astype(jnp.int32))

    dest_blocks = blocked(TOK_TILE)
    zero_start = jnp.maximum(pad_end - ROW_BLOCK, 0).astype(jnp.int32)

    n_used = n_used.reshape(1)
    xb = _dispatch(zero_start, n_used, blocked(DISPATCH_TILE), h3.reshape(T, D), n_slots)
    yb = _experts(step_blk, step_pair, step_e, step_run, step_next, n_steps.reshape(1), n_used, xb,
                  p["w_e1"], p["b_e1"].reshape(N_EXPERTS, 1, 2 * D_FF),
                  p["w_e2"], p["b_e2"].reshape(N_EXPERTS, 1, D))
    return dest_blocks, x2.reshape(T, D), gates.T, yb


def kernel(x, mem, norm_mix_g, w_in, gla_w_a2, gla_b_a, gla_out_g, sgu_norm_g, sgu_norm_b, sgu_w, sgu_b, w_proj_a, w_proj_b, w_mix_out, norm_x_g, norm_mem_g, w_xq, w_xk, w_xv, w_xo, norm_ffn_g, w_router, b_router, w_e1, b_e1, w_e2, b_e2, norm_final_g):
    B, S, D = x.shape
    depth = w_in.shape[0]
    assert depth == 1, "the final norm is fused into the last layer's combine step"
    stacked = dict(norm_mix_g=norm_mix_g, w_in=w_in, gla_w_a2=gla_w_a2, gla_b_a=gla_b_a,
                   gla_out_g=gla_out_g, sgu_norm_g=sgu_norm_g, sgu_norm_b=sgu_norm_b, sgu_w=sgu_w,
                   sgu_b=sgu_b, w_proj_a=w_proj_a, w_proj_b=w_proj_b, w_mix_out=w_mix_out,
                   norm_x_g=norm_x_g, w_xq=w_xq, w_xo=w_xo, norm_ffn_g=norm_ffn_g,
                   w_router=w_router, b_router=b_router, w_e1=w_e1, b_e1=b_e1, w_e2=w_e2, b_e2=b_e2)
    p = {name: v[0] for name, v in stacked.items()}
    kv = (mem, _row(norm_mem_g[0]), w_xk[0].astype(BF16), w_xv[0].astype(BF16))
    dest_blocks, x2, gates_t, yb = _layer(x, kv, p)
    out = _combine(dest_blocks, x2, gates_t, _row(norm_final_g), yb)
    return out.reshape(B, S, D)
```

```python
import functools

import jax
import jax.numpy as jnp
from jax import lax
from jax.experimental import pallas as pl
from jax.experimental.pallas import tpu as pltpu

F32 = jnp.float32
BF16 = jnp.bfloat16

D_MODEL = 1024
MEM_LEN = 256
EPS = 1e-5

GLA_HEADS = 4
GLA_DV = 128
GLA_DK = 64
GLA_WIDTH_K = GLA_HEADS * GLA_DK
GLA_WIDTH_V = GLA_HEADS * GLA_DV
GLA_GATE_RANK = 16
GLA_RANK_PAD = 128
GLA_TAU = 16.0
GLA_CHUNK = 64

SGU_WIDTH = 512
SGU_GROUPS = 4
SGU_GROUP_DIM = SGU_WIDTH // SGU_GROUPS
SGU_CHUNK = 128

OFF_Q = 0
OFF_K = OFF_Q + GLA_WIDTH_K
OFF_V = OFF_K + GLA_WIDTH_K
OFF_R = OFF_V + GLA_WIDTH_V
OFF_A = OFF_R + GLA_WIDTH_V
OFF_U = OFF_A + GLA_GATE_RANK
OFF_SV = OFF_U + SGU_WIDTH
OFF_G = OFF_SV + SGU_WIDTH

XATTN_HEADS = 4
XATTN_DH = D_MODEL // XATTN_HEADS

N_EXPERTS = 32
TOP_K = 4
D_FF = D_MODEL
SWIGLU_ALPHA = 1.702
SWIGLU_LIMIT = 7.0

SEQ_TILE = 512
XATTN_TILE = 1024
ROW_BLOCK = 256
TOK_TILE = 512
DISPATCH_TILE = 1024
VMEM_LIMIT = 56 * 1024 * 1024


def _rmsnorm(x, g):
    return x * lax.rsqrt(jnp.mean(x * x, axis=-1, keepdims=True) + EPS) * g


def _sigmoid(x):
    return 1.0 / (1.0 + jnp.exp(-x))


def _log_sigmoid(x):
    return jnp.minimum(x, 0.0) - jnp.log1p(jnp.exp(-jnp.abs(x)))


def _gelu(x):
    return 0.5 * x * (1.0 + lax.erf(x * (2.0 ** -0.5)))


def _dot(a, b):
    return jnp.dot(a, b, preferred_element_type=F32)


def _dot_nt(a, b):
    return lax.dot_general(a, b, (((1,), (1,)), ((), ())), preferred_element_type=F32)


def _dot_tn(a, b):
    return lax.dot_general(a, b, (((0,), (0,)), ((), ())), preferred_element_type=F32)


def _split_bf16(x):
    hi = x.astype(BF16)
    lo = (x - hi.astype(F32)).astype(BF16)
    return hi, lo


def _const_spec(shape):
    zeros = (0,) * len(shape)
    return pl.BlockSpec(shape, lambda *_: zeros, pipeline_mode=pl.Buffered(1))


def _xattn_kv_body(mem_ref, g_ref, wk_ref, wv_ref, k_ref, v_ref):
    m = _rmsnorm(mem_ref[0], g_ref[...]).astype(BF16)
    k_ref[0] = _dot(m, wk_ref[...]).astype(BF16)
    v_ref[0] = _dot(m, wv_ref[...]).astype(BF16)


def _xattn_kv(mem, g, wk, wv):
    B = mem.shape[0]
    return pl.pallas_call(
        _xattn_kv_body,
        grid=(B,),
        in_specs=[
            pl.BlockSpec((1, MEM_LEN, D_MODEL), lambda b: (b, 0, 0)),
            _const_spec((1, D_MODEL)),
            _const_spec((D_MODEL, D_MODEL)),
            _const_spec((D_MODEL, D_MODEL)),
        ],
        out_specs=[
            pl.BlockSpec((1, MEM_LEN, D_MODEL), lambda b: (b, 0, 0)),
            pl.BlockSpec((1, MEM_LEN, D_MODEL), lambda b: (b, 0, 0)),
        ],
        out_shape=[jax.ShapeDtypeStruct((B, MEM_LEN, D_MODEL), BF16)] * 2,
        compiler_params=pltpu.CompilerParams(
            dimension_semantics=("arbitrary",), vmem_limit_bytes=VMEM_LIMIT),
        name="xattn_kv",
    )(mem, g, wk, wv)


def _gla_branch(h, wqkvr_ref, wa_ref, wa2_ref, ba_ref, outg_ref, st_ref):
    ts = h.shape[0]
    n_chunks = ts // GLA_CHUNK
    p = _dot(h, wqkvr_ref[...])
    q = p[:, OFF_Q:OFF_K] * (GLA_DK ** -0.5)
    k = p[:, OFF_K:OFF_V]
    v = p[:, OFF_V:OFF_R].astype(BF16)
    r = p[:, OFF_R:OFF_A]

    a_low = _dot(h, wa_ref[...]).astype(BF16)
    log_a = _log_sigmoid(_dot(a_low, wa2_ref[...]) + ba_ref[...]) * (1.0 / GLA_TAU)

    row = lax.broadcasted_iota(jnp.int32, (ts, ts), 0)
    col = lax.broadcasted_iota(jnp.int32, (ts, ts), 1)
    tri = jnp.where((col <= row) & ((col >> 6) == (row >> 6)), 1.0, 0.0).astype(BF16)
    la_hi, la_lo = _split_bf16(log_a)
    cum = _dot(tri, la_hi) + _dot(tri, la_lo)

    q_in = (q * jnp.exp(cum)).astype(BF16)
    k_in = (k * jnp.exp(-cum)).astype(BF16)

    lane_k = lax.broadcasted_iota(jnp.int32, (1, GLA_WIDTH_K), 1)
    head_masks = [jnp.where((lane_k >> 6) == hh, 1.0, 0.0).astype(BF16) for hh in range(GLA_HEADS)]
    lane_v = lax.broadcasted_iota(jnp.int32, (1, GLA_WIDTH_V), 1)
    value_masks = [jnp.where((lane_v >> 7) == hh, 1.0, 0.0).astype(BF16) for hh in range(GLA_HEADS)]
    r4 = lax.broadcasted_iota(jnp.int32, (GLA_CHUNK, GLA_HEADS * GLA_CHUNK), 0)
    c4 = lax.broadcasted_iota(jnp.int32, (GLA_CHUNK, GLA_HEADS * GLA_CHUNK), 1)
    causal4 = (c4 & (GLA_CHUNK - 1)) <= r4
    sr = lax.broadcasted_iota(jnp.int32, (GLA_WIDTH_V, GLA_WIDTH_K), 0)
    sc = lax.broadcasted_iota(jnp.int32, (GLA_WIDTH_V, GLA_WIDTH_K), 1)
    state_mask = jnp.where((sr >> 7) == (sc >> 6), 1.0, 0.0)

    state = st_ref[...]
    outs = []
    for n in range(n_chunks):
        lo, hi = n * GLA_CHUNK, (n + 1) * GLA_CHUNK
        cum_c = cum[lo:hi]
        last = cum[hi - 1:hi]
        q_c = q_in[lo:hi]
        k_c = k_in[lo:hi]
        v_c = v[lo:hi]
        k_out = (k[lo:hi] * jnp.exp(last - cum_c)).astype(BF16)
        k4 = jnp.concatenate([k_c * head_masks[hh] for hh in range(GLA_HEADS)], axis=0)
        v4 = jnp.concatenate([v_c * value_masks[hh] for hh in range(GLA_HEADS)], axis=0)
        att4 = jnp.where(causal4, _dot_nt(q_c, k4), 0.0).astype(BF16)
        o_intra = _dot(att4, v4)
        o_inter = _dot_nt(q_c, state.astype(BF16))
        outs.append(o_intra + o_inter)
        state = state * jnp.exp(last) + _dot_tn(v_c, k_out) * state_mask
    st_ref[...] = state

    o = jnp.concatenate(outs, axis=0)
    outg = outg_ref[...]
    normed = []
    for hh in range(GLA_HEADS):
        sl = slice(hh * GLA_DV, (hh + 1) * GLA_DV)
        normed.append(_rmsnorm(o[:, sl], outg[:, sl]))
    o = jnp.concatenate(normed, axis=1)
    return o * (r * _sigmoid(r))


def _sgu_branch(h, wu_ref, wsv_ref, ng_ref, nb_ref, sw_ref, sb_ref):
    ts = h.shape[0]
    n_chunks = ts // SGU_CHUNK
    u = _gelu(_dot(h, wu_ref[...]))
    v = _gelu(_dot(h, wsv_ref[...]))
    mu = jnp.mean(v, axis=-1, keepdims=True)
    vc = v - mu
    var = jnp.mean(vc * vc, axis=-1, keepdims=True)
    v = (vc * lax.rsqrt(var + EPS) * ng_ref[...] + nb_ref[...]).astype(BF16)

    row = lax.broadcasted_iota(jnp.int32, (SGU_CHUNK, SGU_CHUNK), 0)
    col = lax.broadcasted_iota(jnp.int32, (SGU_CHUNK, SGU_CHUNK), 1)
    causal = col <= row
    sb = sb_ref[...]
    per_group = []
    for g in range(SGU_GROUPS):
        w = jnp.where(causal, sw_ref[g], 0.0).astype(BF16)
        gs = slice(g * SGU_GROUP_DIM, (g + 1) * SGU_GROUP_DIM)
        vcat = jnp.concatenate(
            [v[c * SGU_CHUNK:(c + 1) * SGU_CHUNK, gs] for c in range(n_chunks)], axis=1)
        per_group.append(_dot(w, vcat) + sb[:, g:g + 1])
    rows = []
    for c in range(n_chunks):
        cs = slice(c * SGU_GROUP_DIM, (c + 1) * SGU_GROUP_DIM)
        rows.append(jnp.concatenate([per_group[g][:, cs] for g in range(SGU_GROUPS)], axis=1))
    mixed = jnp.concatenate(rows, axis=0)
    return u * mixed


def _mixer_body(x_ref, g_ref, wqkvr_ref, wa_ref, wa2_ref, ba_ref, outg_ref,
                wu_ref, wsv_ref, ng_ref, nb_ref, sw_ref, sb_ref,
                wg_ref, wpa_ref, wpb_ref, wmix_ref, o_ref, st_ref):
    @pl.when(pl.program_id(1) == 0)
    def _():
        st_ref[...] = jnp.zeros_like(st_ref)

    x = x_ref[0]
    h = _rmsnorm(x, g_ref[...]).astype(BF16)
    yb_in = _sgu_branch(h, wu_ref, wsv_ref, ng_ref, nb_ref, sw_ref, sb_ref)
    y_b = _dot(yb_in.astype(BF16), wpb_ref[...])
    gates = _sigmoid(_dot(h, wg_ref[...]))
    ya_in = _gla_branch(h, wqkvr_ref, wa_ref, wa2_ref, ba_ref, outg_ref, st_ref)
    y_a = _dot(ya_in.astype(BF16), wpa_ref[...])
    mix = gates[:, :D_MODEL] * y_a + gates[:, D_MODEL:] * y_b
    o_ref[0] = x + _dot(mix.astype(BF16), wmix_ref[...])


def _mixer(x, g, wqkvr, wa, wa2, ba, outg, wu, wsv, ng, nb, sw, sb, wg, wpa, wpb, wmix):
    B, S, D = x.shape
    consts = (g, wqkvr, wa, wa2, ba, outg, wu, wsv, ng, nb, sw, sb, wg, wpa, wpb, wmix)
    return pl.pallas_call(
        _mixer_body,
        grid=(B, S // SEQ_TILE),
        in_specs=[pl.BlockSpec((1, SEQ_TILE, D), lambda b, j: (b, j, 0))]
                 + [_const_spec(c.shape) for c in consts],
        out_specs=pl.BlockSpec((1, SEQ_TILE, D), lambda b, j: (b, j, 0)),
        out_shape=jax.ShapeDtypeStruct((B, S, D), F32),
        scratch_shapes=[pltpu.VMEM((GLA_WIDTH_V, GLA_WIDTH_K), F32)],
        compiler_params=pltpu.CompilerParams(
            dimension_semantics=("arbitrary", "arbitrary"), vmem_limit_bytes=VMEM_LIMIT),
        name="mixer",
    )(x, *consts)


def _xattn_body(x_ref, k_ref, v_ref, gx_ref, wq_ref, wo_ref, gf_ref, wr_hi_ref, wr_lo_ref, br_ref,
                x2_ref, h3_ref, topi_ref, gate_ref, rank_ref, cnt_ref, carry_ref):
    first = (pl.program_id(0) == 0) & (pl.program_id(1) == 0)

    @pl.when(first)
    def _():
        carry_ref[...] = jnp.zeros_like(carry_ref)

    x = x_ref[0]
    ts = x.shape[0]
    h = _rmsnorm(x, gx_ref[...]).astype(BF16)
    q = _dot(h, wq_ref[...]).astype(BF16)
    km = k_ref[0]
    vm = v_ref[0]
    heads = []
    for hh in range(XATTN_HEADS):
        sl = slice(hh * XATTN_DH, (hh + 1) * XATTN_DH)
        s = _dot_nt(q[:, sl], km[:, sl]) * (XATTN_DH ** -0.5)
        s = s - jnp.max(s, axis=-1, keepdims=True)
        e = jnp.exp(s)
        p = e / jnp.sum(e, axis=-1, keepdims=True)
        heads.append(_dot(p.astype(BF16), vm[:, sl]).astype(BF16))
    o = jnp.concatenate(heads, axis=1)
    x2 = x + _dot(o, wo_ref[...])
    x2_ref[0] = x2

    h3 = _rmsnorm(x2, gf_ref[...])
    h3_ref[0] = h3

    h_hi, h_lo = _split_bf16(h3)
    logits = (_dot_nt(wr_hi_ref[...], h_hi) + _dot_nt(wr_hi_ref[...], h_lo)
              + _dot_nt(wr_lo_ref[...], h_hi)) + br_ref[...]

    e_iota = lax.broadcasted_iota(jnp.int32, (N_EXPERTS, ts), 0)
    work = logits
    vals, idxs, hots = [], [], []
    for _ in range(TOP_K):
        m = jnp.max(work, axis=0, keepdims=True)
        idx = jnp.min(jnp.where(work == m, e_iota, N_EXPERTS), axis=0, keepdims=True)
        hot = e_iota == idx
        vals.append(m)
        idxs.append(idx)
        hots.append(hot)
        work = jnp.where(hot, -jnp.inf, work)
    exps = [jnp.exp(vv - vals[0]) for vv in vals]
    denom = exps[0] + exps[1] + exps[2] + exps[3]
    gate_ref[...] = jnp.concatenate([ee / denom for ee in exps], axis=0)
    topi_ref[...] = jnp.concatenate(idxs, axis=0)

    multi = jnp.where(hots[0] | hots[1] | hots[2] | hots[3], 1.0, 0.0)
    srow = lax.broadcasted_iota(jnp.int32, (ts, ts), 0)
    scol = lax.broadcasted_iota(jnp.int32, (ts, ts), 1)
    strict = jnp.where(srow < scol, 1.0, 0.0).astype(BF16)
    before = _dot(multi.astype(BF16), strict) + carry_ref[...]
    ranks = [jnp.sum(jnp.where(hot, before, 0.0), axis=0, keepdims=True) for hot in hots]
    rank_ref[...] = jnp.concatenate(ranks, axis=0).astype(jnp.int32)
    carry = carry_ref[...] + jnp.sum(multi, axis=1, keepdims=True)
    carry_ref[...] = carry
    cnt_ref[...] = carry.astype(jnp.int32)


def _xattn(x1, kmem, vmem, gx, wq, wo, gf, wr_hi, wr_lo, br):
    B, S, D = x1.shape
    T = B * S
    nj = S // XATTN_TILE
    tok_spec = pl.BlockSpec((1, XATTN_TILE, D), lambda b, j: (b, j, 0))
    mem_spec = pl.BlockSpec((1, MEM_LEN, D), lambda b, j: (b, 0, 0))
    lane_spec = pl.BlockSpec((TOP_K, XATTN_TILE), lambda b, j: (0, b * nj + j))
    consts = (gx, wq, wo, gf, wr_hi, wr_lo, br)
    return pl.pallas_call(
        _xattn_body,
        grid=(B, nj),
        in_specs=[tok_spec, mem_spec, mem_spec] + [_const_spec(c.shape) for c in consts],
        out_specs=[tok_spec, tok_spec, lane_spec, lane_spec, lane_spec,
                   pl.BlockSpec((N_EXPERTS, 1), lambda b, j: (0, 0))],
        out_shape=[
            jax.ShapeDtypeStruct((B, S, D), F32),
            jax.ShapeDtypeStruct((B, S, D), F32),
            jax.ShapeDtypeStruct((TOP_K, T), jnp.int32),
            jax.ShapeDtypeStruct((TOP_K, T), F32),
            jax.ShapeDtypeStruct((TOP_K, T), jnp.int32),
            jax.ShapeDtypeStruct((N_EXPERTS, 1), jnp.int32),
        ],
        scratch_shapes=[pltpu.VMEM((N_EXPERTS, 1), F32)],
        compiler_params=pltpu.CompilerParams(
            dimension_semantics=("arbitrary", "arbitrary"), vmem_limit_bytes=VMEM_LIMIT),
        name="xattn_router",
    )(x1, kmem, vmem, *consts)


ROW_GROUP = 8


def _dispatch_body(zs_ref, nu_ref, dest_ref, h_ref, xb_ref, zbuf_ref, zsem, tsem, sem):
    n_blocks = xb_ref.shape[0] // ROW_BLOCK

    def zero_copy(start, zs):
        start = pl.multiple_of(start, ROW_BLOCK)
        return pltpu.make_async_copy(zbuf_ref, xb_ref.at[pl.ds(start, ROW_BLOCK)], zs)

    @pl.when(pl.program_id(0) == 0)
    def _():
        zbuf_ref[...] = jnp.zeros_like(zbuf_ref)
        for e in range(N_EXPERTS):
            zero_copy(zs_ref[e], zsem).start()

        def tail_start(blk, c):
            zero_copy(blk * ROW_BLOCK, tsem).start()
            return c

        lax.fori_loop(nu_ref[0], n_blocks, tail_start, 0)
        for e in range(N_EXPERTS):
            zero_copy(zs_ref[e], zsem).wait()

    def start(g, c):
        base = pl.multiple_of(g * ROW_GROUP, ROW_GROUP)
        tile = h_ref.at[pl.ds(base, ROW_GROUP)]
        for j in range(ROW_GROUP):
            for kk in range(TOP_K):
                d = dest_ref[0, 0, kk * DISPATCH_TILE + base + j]
                pltpu.make_async_copy(tile.at[pl.ds(j, 1)], xb_ref.at[pl.ds(d, 1)], sem).start(
                    priority=kk % 2)
        return c

    lax.fori_loop(0, DISPATCH_TILE // ROW_GROUP, start, 0)
    for kk in range(TOP_K):
        pltpu.make_async_copy(h_ref, xb_ref.at[pl.ds(0, DISPATCH_TILE)], sem).wait()

    @pl.when(pl.program_id(0) == 0)
    def _():
        def tail_wait(blk, c):
            zero_copy(blk * ROW_BLOCK, tsem).wait()
            return c

        lax.fori_loop(nu_ref[0], n_blocks, tail_wait, 0)


def _dispatch(zero_start, n_used, dest_blocks, h3, n_slots):
    T, D = h3.shape
    return pl.pallas_call(
        _dispatch_body,
        grid_spec=pltpu.PrefetchScalarGridSpec(
            num_scalar_prefetch=2,
            grid=(T // DISPATCH_TILE,),
            in_specs=[
                pl.BlockSpec((1, 1, TOP_K * DISPATCH_TILE), lambda i, zs, nu: (i, 0, 0),
                             memory_space=pltpu.SMEM),
                pl.BlockSpec((DISPATCH_TILE, D), lambda i, zs, nu: (i, 0)),
            ],
            out_specs=pl.BlockSpec(memory_space=pl.ANY),
            scratch_shapes=[pltpu.VMEM((ROW_BLOCK, D), F32), pltpu.SemaphoreType.DMA,
                            pltpu.SemaphoreType.DMA, pltpu.SemaphoreType.DMA],
        ),
        out_shape=jax.ShapeDtypeStruct((n_slots, D), F32),
        compiler_params=pltpu.CompilerParams(
            dimension_semantics=("arbitrary",), vmem_limit_bytes=VMEM_LIMIT),
        name="dispatch",
    )(zero_start, n_used, dest_blocks, h3)


def _expert_mlp(xb, w1b_ref, b1_ref, w2b_ref, b2_ref):
    hcat = _dot(xb.astype(BF16), w1b_ref[...]) + b1_ref[0]
    glu = jnp.minimum(hcat[:, :D_FF], SWIGLU_LIMIT)
    lin = jnp.clip(hcat[:, D_FF:], -SWIGLU_LIMIT, SWIGLU_LIMIT)
    act = (lin + 1.0) * glu * _sigmoid(SWIGLU_ALPHA * glu)
    return _dot(act.astype(BF16), w2b_ref[...]) + b2_ref[0]


def _experts_body(blk_ref, pair_ref, se_ref, run_ref, nxt_ref, ns_ref, nu_ref,
                  x_ref, w1_hbm, b1_ref, w2_hbm, b2_ref,
                  y_hbm, w1b_ref, w2b_ref, wf1_ref, wf2_ref, ybuf_ref, zbuf_ref, sems, wsems, zsem):
    s = pl.program_id(0)
    n_steps = ns_ref[0]
    last = pl.num_programs(0) - 1
    total_blocks = y_hbm.shape[0] // ROW_BLOCK

    def tail_copy(blk):
        row0 = pl.multiple_of(blk * ROW_BLOCK, ROW_BLOCK)
        return pltpu.make_async_copy(zbuf_ref, y_hbm.at[pl.ds(row0, ROW_BLOCK)], zsem)

    @pl.when(s == 0)
    def _():
        zbuf_ref[...] = jnp.zeros_like(zbuf_ref)

        def go(blk, c):
            tail_copy(blk).start()
            return c

        lax.fori_loop(nu_ref[0], total_blocks, go, 0)

    @pl.when(s == last)
    def _():
        def done(blk, c):
            tail_copy(blk).wait()
            return c

        lax.fori_loop(nu_ref[0], total_blocks, done, 0)

    def out_copy(step, half):
        slot = step % 2
        row0 = pl.multiple_of((blk_ref[step] + half) * ROW_BLOCK, ROW_BLOCK)
        return pltpu.make_async_copy(
            ybuf_ref.at[slot, pl.ds(half * ROW_BLOCK, ROW_BLOCK)],
            y_hbm.at[pl.ds(row0, ROW_BLOCK)], sems.at[slot])

    def wait_step(step):
        out_copy(step, 0).wait()

        @pl.when(pair_ref[step] == 1)
        def _():
            out_copy(step, 1).wait()

    def weight_copies(e, wslot):
        return (pltpu.make_async_copy(w1_hbm.at[e], wf1_ref.at[wslot], wsems.at[wslot]),
                pltpu.make_async_copy(w2_hbm.at[e], wf2_ref.at[wslot], wsems.at[wslot]))

    @pl.when(s < n_steps)
    def _():
        prev = se_ref[jnp.maximum(s - 1, 0)]
        wslot = run_ref[s] % 2

        @pl.when(s == 0)
        def _():
            for cp in weight_copies(se_ref[0], 0):
                cp.start()

        @pl.when((s == 0) | (se_ref[s] != prev))
        def _():
            for cp in weight_copies(se_ref[s], wslot):
                cp.wait()
            w1b_ref[...] = wf1_ref[wslot].astype(BF16)
            w2b_ref[...] = wf2_ref[wslot].astype(BF16)

            @pl.when(nxt_ref[s] < N_EXPERTS)
            def _():
                for cp in weight_copies(nxt_ref[s], 1 - wslot):
                    cp.start()

        slot = s % 2

        @pl.when(pair_ref[s] == 1)
        def _():
            ybuf_ref[slot] = _expert_mlp(x_ref[...], w1b_ref, b1_ref, w2b_ref, b2_ref)

        @pl.when(pair_ref[s] == 0)
        def _():
            ybuf_ref[slot, :ROW_BLOCK] = _expert_mlp(x_ref[:ROW_BLOCK], w1b_ref, b1_ref, w2b_ref, b2_ref)

    @pl.when((s >= 1) & (s - 1 < n_steps))
    def _():
        wait_step(s - 1)

    @pl.when(s < n_steps)
    def _():
        out_copy(s, 0).start()

        @pl.when(pair_ref[s] == 1)
        def _():
            out_copy(s, 1).start()

        @pl.when(s == last)
        def _():
            wait_step(s)


def _experts(step_blk, step_pair, step_e, step_run, step_next, n_steps, n_used, xb, w1, b1, w2, b2):
    n_rows, D = xb.shape

    def x_map(s, blk, pair, se, run, nxt, ns, nu):
        return (blk[s] * ROW_BLOCK, 0)

    def b_map(s, blk, pair, se, run, nxt, ns, nu):
        return (se[s], 0, 0)

    return pl.pallas_call(
        _experts_body,
        grid_spec=pltpu.PrefetchScalarGridSpec(
            num_scalar_prefetch=7,
            grid=(step_blk.shape[0],),
            in_specs=[
                pl.BlockSpec((pl.Element(2 * ROW_BLOCK), pl.Element(D)), x_map),
                pl.BlockSpec(memory_space=pl.ANY),
                pl.BlockSpec((1, 1, 2 * D_FF), b_map),
                pl.BlockSpec(memory_space=pl.ANY),
                pl.BlockSpec((1, 1, D), b_map),
            ],
            out_specs=pl.BlockSpec(memory_space=pl.ANY),
            scratch_shapes=[pltpu.VMEM((D, 2 * D_FF), BF16), pltpu.VMEM((D_FF, D), BF16),
                            pltpu.VMEM((2, D, 2 * D_FF), F32), pltpu.VMEM((2, D_FF, D), F32),
                            pltpu.VMEM((2, 2 * ROW_BLOCK, D), F32), pltpu.VMEM((ROW_BLOCK, D), F32),
                            pltpu.SemaphoreType.DMA((2,)), pltpu.SemaphoreType.DMA((2,)),
                            pltpu.SemaphoreType.DMA],
        ),
        out_shape=jax.ShapeDtypeStruct((n_rows, D), F32),
        compiler_params=pltpu.CompilerParams(
            dimension_semantics=("arbitrary",), vmem_limit_bytes=VMEM_LIMIT),
        name="experts",
    )(step_blk, step_pair, step_e, step_run, step_next, n_steps, n_used, xb, w1, b1, w2, b2)


def _combine_body(dest_ref, next_ref, x_ref, gate_ref, g_ref, yb_ref, o_ref, rows_ref, sems):
    i = pl.program_id(0)
    last = pl.num_programs(0) - 1
    slot = i % 2
    n_groups = TOK_TILE // ROW_GROUP

    def start_group(idx_ref, sl, base):
        for kk in range(TOP_K):
            tile = rows_ref.at[sl, kk, pl.ds(base, ROW_GROUP)]
            for j in range(ROW_GROUP):
                d = idx_ref[0, 0, kk * TOK_TILE + base + j]
                pltpu.make_async_copy(yb_ref.at[pl.ds(d, 1)], tile.at[pl.ds(j, 1)],
                                      sems.at[sl]).start(priority=kk % 2)

    def reduce_group(base):
        gates = gate_ref[pl.ds(base, ROW_GROUP), :]
        y = gates[:, 0:1] * rows_ref[slot, 0, pl.ds(base, ROW_GROUP), :]
        for kk in range(1, TOP_K):
            y = y + gates[:, kk:kk + 1] * rows_ref[slot, kk, pl.ds(base, ROW_GROUP), :]
        o_ref[pl.ds(base, ROW_GROUP), :] = x_ref[pl.ds(base, ROW_GROUP), :] + y

    @pl.when(i == 0)
    def _():
        def first(g, c):
            start_group(dest_ref, 0, pl.multiple_of(g * ROW_GROUP, ROW_GROUP))
            return c

        lax.fori_loop(0, n_groups, first, 0)

    for kk in range(TOP_K):
        pltpu.make_async_copy(yb_ref.at[pl.ds(0, TOK_TILE)], rows_ref.at[slot, kk], sems.at[slot]).wait()

    @pl.when(i < last)
    def _():
        def both(g, c):
            base = pl.multiple_of(g * ROW_GROUP, ROW_GROUP)
            start_group(next_ref, 1 - slot, base)
            reduce_group(base)
            return c

        lax.fori_loop(0, n_groups, both, 0)

    @pl.when(i == last)
    def _():
        def only(g, c):
            reduce_group(pl.multiple_of(g * ROW_GROUP, ROW_GROUP))
            return c

        lax.fori_loop(0, n_groups, only, 0)

    o_ref[...] = _rmsnorm(o_ref[...], g_ref[...])


def _combine(dest_blocks, x2, gates_t, g, yb):
    T, D = x2.shape
    n_tiles = T // TOK_TILE
    return pl.pallas_call(
        _combine_body,
        grid=(n_tiles,),
        in_specs=[
            pl.BlockSpec((1, 1, TOP_K * TOK_TILE), lambda i: (i, 0, 0), memory_space=pltpu.SMEM),
            pl.BlockSpec((1, 1, TOP_K * TOK_TILE), lambda i: (jnp.minimum(i + 1, n_tiles - 1), 0, 0),
                         memory_space=pltpu.SMEM),
            pl.BlockSpec((TOK_TILE, D), lambda i: (i, 0)),
            pl.BlockSpec((TOK_TILE, TOP_K), lambda i: (i, 0)),
            _const_spec((1, D)),
            pl.BlockSpec(memory_space=pl.ANY),
        ],
        out_specs=pl.BlockSpec((TOK_TILE, D), lambda i: (i, 0)),
        out_shape=jax.ShapeDtypeStruct((T, D), F32),
        scratch_shapes=[pltpu.VMEM((2, TOP_K, TOK_TILE, D), F32), pltpu.SemaphoreType.DMA((2,))],
        compiler_params=pltpu.CompilerParams(
            dimension_semantics=("arbitrary",), vmem_limit_bytes=VMEM_LIMIT),
        name="combine",
    )(dest_blocks, dest_blocks, x2, gates_t, g, yb)


def _row(v):
    return v.reshape(1, -1).astype(F32)


def _layer(x, kv, p):
    B, S, D = x.shape
    T = B * S
    w_in = p["w_in"]
    wa = jnp.zeros((D, GLA_RANK_PAD), F32).at[:, :GLA_GATE_RANK].set(w_in[:, OFF_A:OFF_U])
    wa2 = jnp.zeros((GLA_RANK_PAD, GLA_WIDTH_K), F32).at[:GLA_GATE_RANK].set(p["gla_w_a2"])
    x1 = _mixer(
        x, _row(p["norm_mix_g"]),
        w_in[:, OFF_Q:OFF_A].astype(BF16), wa.astype(BF16), wa2.astype(BF16),
        _row(p["gla_b_a"]), _row(p["gla_out_g"]),
        w_in[:, OFF_U:OFF_SV].astype(BF16), w_in[:, OFF_SV:OFF_G].astype(BF16),
        _row(p["sgu_norm_g"]), _row(p["sgu_norm_b"]), p["sgu_w"].astype(F32), p["sgu_b"].T.astype(F32),
        w_in[:, OFF_G:].astype(BF16), p["w_proj_a"].astype(BF16), p["w_proj_b"].astype(BF16),
        p["w_mix_out"].astype(BF16))

    kmem, vmem = kv
    wr_t = p["w_router"].T.astype(F32)
    wr_hi = wr_t.astype(BF16)
    wr_lo = (wr_t - wr_hi.astype(F32)).astype(BF16)
    x2, h3, topi, gates, rank, counts = _xattn(
        x1, kmem, vmem, _row(p["norm_x_g"]), p["w_xq"].astype(BF16), p["w_xo"].astype(BF16),
        _row(p["norm_ffn_g"]), wr_hi, wr_lo, p["b_router"].reshape(N_EXPERTS, 1).astype(F32))

    counts = counts.reshape(N_EXPERTS)
    blocks_e = (counts + ROW_BLOCK - 1) // ROW_BLOCK
    padded = blocks_e * ROW_BLOCK
    pad_end = jnp.cumsum(padded)
    pad_start = pad_end - padded
    n_blocks = (T * TOP_K) // ROW_BLOCK + N_EXPERTS
    n_slots = (n_blocks + 1) * ROW_BLOCK
    n_used = (pad_end[-1] // ROW_BLOCK).astype(jnp.int32)
    steps_e = (blocks_e + 1) // 2
    step_end = jnp.cumsum(steps_e)
    n_steps = step_end[-1].astype(jnp.int32)
    max_steps = (n_blocks + N_EXPERTS) // 2
    sidx = jnp.minimum(jnp.arange(max_steps, dtype=jnp.int32), n_steps - 1)
    step_e = jnp.minimum(jnp.sum(step_end[None, :] <= sidx[:, None], axis=1), N_EXPERTS - 1).astype(jnp.int32)
    of_step = step_e[:, None] == jnp.arange(N_EXPERTS, dtype=jnp.int32)[None, :]

    def per_step(v):
        return jnp.sum(jnp.where(of_step, v[None, :], 0), axis=1)

    local = sidx - per_step(step_end - steps_e)
    step_blk = (per_step(pad_start // ROW_BLOCK) + 2 * local).astype(jnp.int32)
    step_pair = (2 * local + 1 < per_step(blocks_e)).astype(jnp.int32)
    ids = jnp.arange(N_EXPERTS, dtype=jnp.int32)
    has_rows = steps_e > 0
    run_e = jnp.cumsum(has_rows.astype(jnp.int32)) - 1
    later = (ids[None, :] > ids[:, None]) & has_rows[None, :]
    next_e = jnp.min(jnp.where(later, ids[None, :], N_EXPERTS), axis=1)
    step_run = per_step(run_e).astype(jnp.int32)
    step_next = per_step(next_e).astype(jnp.int32)
    e_ids = jnp.arange(N_EXPERTS, dtype=jnp.int32)[:, None, None]
    dest = jnp.sum(jnp.where(topi[None] == e_ids, pad_start[:, None, None], 0), axis=0) + rank
    def blocked(tile):
        return (dest.reshape(TOP_K, T // tile, tile).transpose(1, 0, 2)
                .reshape(T // tile, 1, TOP_K * tile).astype(jnp.int32))

    dest_blocks = blocked(TOK_TILE)
    zero_start = jnp.maximum(pad_end - ROW_BLOCK, 0).astype(jnp.int32)

    n_used = n_used.reshape(1)
    xb = _dispatch(zero_start, n_used, blocked(DISPATCH_TILE), h3.reshape(T, D), n_slots)
    yb = _experts(step_blk, step_pair, step_e, step_run, step_next, n_steps.reshape(1), n_used, xb,
                  p["w_e1"], p["b_e1"].reshape(N_EXPERTS, 1, 2 * D_FF),
                  p["w_e2"], p["b_e2"].reshape(N_EXPERTS, 1, D))
    return dest_blocks, x2.reshape(T, D), gates.T, yb


def kernel(x, mem, norm_mix_g, w_in, gla_w_a2, gla_b_a, gla_out_g, sgu_norm_g, sgu_norm_b, sgu_w, sgu_b, w_proj_a, w_proj_b, w_mix_out, norm_x_g, norm_mem_g, w_xq, w_xk, w_xv, w_xo, norm_ffn_g, w_router, b_router, w_e1, b_e1, w_e2, b_e2, norm_final_g):
    B, S, D = x.shape
    depth = w_in.shape[0]
    assert depth == 1, "the final norm is fused into the last layer's combine step"
    stacked = dict(norm_mix_g=norm_mix_g, w_in=w_in, gla_w_a2=gla_w_a2, gla_b_a=gla_b_a,
                   gla_out_g=gla_out_g, sgu_norm_g=sgu_norm_g, sgu_norm_b=sgu_norm_b, sgu_w=sgu_w,
                   sgu_b=sgu_b, w_proj_a=w_proj_a, w_proj_b=w_proj_b, w_mix_out=w_mix_out,
                   norm_x_g=norm_x_g, w_xq=w_xq, w_xo=w_xo, norm_ffn_g=norm_ffn_g,
                   w_router=w_router, b_router=b_router, w_e1=w_e1, b_e1=b_e1, w_e2=w_e2, b_e2=b_e2)
    p = {name: v[0] for name, v in stacked.items()}
    kv = _xattn_kv(mem, _row(norm_mem_g[0]), w_xk[0].astype(BF16), w_xv[0].astype(BF16))
    dest_blocks, x2, gates_t, yb = _layer(x, kv, p)
    out = _combine(dest_blocks, x2, gates_t, _row(norm_final_g), yb)
    return out.reshape(B, S, D)
```

```python
import functools

import jax
import jax.numpy as jnp
from jax import lax
from jax.experimental import pallas as pl
from jax.experimental.pallas import tpu as pltpu

F32 = jnp.float32
BF16 = jnp.bfloat16

D_MODEL = 1024
MEM_LEN = 256
EPS = 1e-5

GLA_HEADS = 4
GLA_DV = 128
GLA_DK = 64
GLA_WIDTH_K = GLA_HEADS * GLA_DK
GLA_WIDTH_V = GLA_HEADS * GLA_DV
GLA_GATE_RANK = 16
GLA_RANK_PAD = 128
GLA_TAU = 16.0
GLA_CHUNK = 64

SGU_WIDTH = 512
SGU_GROUPS = 4
SGU_GROUP_DIM = SGU_WIDTH // SGU_GROUPS
SGU_CHUNK = 128

OFF_Q = 0
OFF_K = OFF_Q + GLA_WIDTH_K
OFF_V = OFF_K + GLA_WIDTH_K
OFF_R = OFF_V + GLA_WIDTH_V
OFF_A = OFF_R + GLA_WIDTH_V
OFF_U = OFF_A + GLA_GATE_RANK
OFF_SV = OFF_U + SGU_WIDTH
OFF_G = OFF_SV + SGU_WIDTH

XATTN_HEADS = 4
XATTN_DH = D_MODEL // XATTN_HEADS

N_EXPERTS = 32
TOP_K = 4
D_FF = D_MODEL
SWIGLU_ALPHA = 1.702
SWIGLU_LIMIT = 7.0

SEQ_TILE = 512
XATTN_TILE = 1024
ROW_BLOCK = 256
TOK_TILE = 512
DISPATCH_TILE = 1024
VMEM_LIMIT = 56 * 1024 * 1024


def _rmsnorm(x, g):
    return x * lax.rsqrt(jnp.mean(x * x, axis=-1, keepdims=True) + EPS) * g


def _sigmoid(x):
    return 1.0 / (1.0 + jnp.exp(-x))


def _log_sigmoid(x):
    return jnp.minimum(x, 0.0) - jnp.log1p(jnp.exp(-jnp.abs(x)))


def _gelu(x):
    return 0.5 * x * (1.0 + lax.erf(x * (2.0 ** -0.5)))


def _dot(a, b):
    return jnp.dot(a, b, preferred_element_type=F32)


def _dot_nt(a, b):
    return lax.dot_general(a, b, (((1,), (1,)), ((), ())), preferred_element_type=F32)


def _dot_tn(a, b):
    return lax.dot_general(a, b, (((0,), (0,)), ((), ())), preferred_element_type=F32)


def _split_bf16(x):
    hi = x.astype(BF16)
    lo = (x - hi.astype(F32)).astype(BF16)
    return hi, lo


def _const_spec(shape):
    zeros = (0,) * len(shape)
    return pl.BlockSpec(shape, lambda *_: zeros, pipeline_mode=pl.Buffered(1))


def _xattn_kv_body(mem_ref, g_ref, wk_ref, wv_ref, k_ref, v_ref):
    m = _rmsnorm(mem_ref[0], g_ref[...]).astype(BF16)
    k_ref[0] = _dot(m, wk_ref[...]).astype(BF16)
    v_ref[0] = _dot(m, wv_ref[...]).astype(BF16)


def _xattn_kv(mem, g, wk, wv):
    B = mem.shape[0]
    return pl.pallas_call(
        _xattn_kv_body,
        grid=(B,),
        in_specs=[
            pl.BlockSpec((1, MEM_LEN, D_MODEL), lambda b: (b, 0, 0)),
            _const_spec((1, D_MODEL)),
            _const_spec((D_MODEL, D_MODEL)),
            _const_spec((D_MODEL, D_MODEL)),
        ],
        out_specs=[
            pl.BlockSpec((1, MEM_LEN, D_MODEL), lambda b: (b, 0, 0)),
            pl.BlockSpec((1, MEM_LEN, D_MODEL), lambda b: (b, 0, 0)),
        ],
        out_shape=[jax.ShapeDtypeStruct((B, MEM_LEN, D_MODEL), BF16)] * 2,
        compiler_params=pltpu.CompilerParams(
            dimension_semantics=("arbitrary",), vmem_limit_bytes=VMEM_LIMIT),
        name="xattn_kv",
    )(mem, g, wk, wv)


def _gla_branch(h, wqkvr_ref, wa_ref, wa2_ref, ba_ref, outg_ref, st_ref):
    ts = h.shape[0]
    n_chunks = ts // GLA_CHUNK
    p = _dot(h, wqkvr_ref[...])
    q = p[:, OFF_Q:OFF_K] * (GLA_DK ** -0.5)
    k = p[:, OFF_K:OFF_V]
    v = p[:, OFF_V:OFF_R].astype(BF16)
    r = p[:, OFF_R:OFF_A]

    a_low = _dot(h, wa_ref[...]).astype(BF16)
    log_a = _log_sigmoid(_dot(a_low, wa2_ref[...]) + ba_ref[...]) * (1.0 / GLA_TAU)

    row = lax.broadcasted_iota(jnp.int32, (ts, ts), 0)
    col = lax.broadcasted_iota(jnp.int32, (ts, ts), 1)
    tri = jnp.where((col <= row) & ((col >> 6) == (row >> 6)), 1.0, 0.0).astype(BF16)
    la_hi, la_lo = _split_bf16(log_a)
    cum = _dot(tri, la_hi) + _dot(tri, la_lo)

    q_in = (q * jnp.exp(cum)).astype(BF16)
    k_in = (k * jnp.exp(-cum)).astype(BF16)

    lane_k = lax.broadcasted_iota(jnp.int32, (1, GLA_WIDTH_K), 1)
    head_masks = [jnp.where((lane_k >> 6) == hh, 1.0, 0.0).astype(BF16) for hh in range(GLA_HEADS)]
    lane_v = lax.broadcasted_iota(jnp.int32, (1, GLA_WIDTH_V), 1)
    value_masks = [jnp.where((lane_v >> 7) == hh, 1.0, 0.0).astype(BF16) for hh in range(GLA_HEADS)]
    r4 = lax.broadcasted_iota(jnp.int32, (GLA_CHUNK, GLA_HEADS * GLA_CHUNK), 0)
    c4 = lax.broadcasted_iota(jnp.int32, (GLA_CHUNK, GLA_HEADS * GLA_CHUNK), 1)
    causal4 = (c4 & (GLA_CHUNK - 1)) <= r4
    sr = lax.broadcasted_iota(jnp.int32, (GLA_WIDTH_V, GLA_WIDTH_K), 0)
    sc = lax.broadcasted_iota(jnp.int32, (GLA_WIDTH_V, GLA_WIDTH_K), 1)
    state_mask = jnp.where((sr >> 7) == (sc >> 6), 1.0, 0.0)

    state = st_ref[...]
    outs = []
    for n in range(n_chunks):
        lo, hi = n * GLA_CHUNK, (n + 1) * GLA_CHUNK
        cum_c = cum[lo:hi]
        last = cum[hi - 1:hi]
        q_c = q_in[lo:hi]
        k_c = k_in[lo:hi]
        v_c = v[lo:hi]
        k_out = (k[lo:hi] * jnp.exp(last - cum_c)).astype(BF16)
        k4 = jnp.concatenate([k_c * head_masks[hh] for hh in range(GLA_HEADS)], axis=0)
        v4 = jnp.concatenate([v_c * value_masks[hh] for hh in range(GLA_HEADS)], axis=0)
        att4 = jnp.where(causal4, _dot_nt(q_c, k4), 0.0).astype(BF16)
        o_intra = _dot(att4, v4)
        o_inter = _dot_nt(q_c, state.astype(BF16))
        outs.append(o_intra + o_inter)
        state = state * jnp.exp(last) + _dot_tn(v_c, k_out) * state_mask
    st_ref[...] = state

    o = jnp.concatenate(outs, axis=0)
    outg = outg_ref[...]
    normed = []
    for hh in range(GLA_HEADS):
        sl = slice(hh * GLA_DV, (hh + 1) * GLA_DV)
        normed.append(_rmsnorm(o[:, sl], outg[:, sl]))
    o = jnp.concatenate(normed, axis=1)
    return o * (r * _sigmoid(r))


def _sgu_branch(h, wu_ref, wsv_ref, ng_ref, nb_ref, sw_ref, sb_ref):
    ts = h.shape[0]
    n_chunks = ts // SGU_CHUNK
    u = _gelu(_dot(h, wu_ref[...]))
    v = _gelu(_dot(h, wsv_ref[...]))
    mu = jnp.mean(v, axis=-1, keepdims=True)
    vc = v - mu
    var = jnp.mean(vc * vc, axis=-1, keepdims=True)
    v = (vc * lax.rsqrt(var + EPS) * ng_ref[...] + nb_ref[...]).astype(BF16)

    row = lax.broadcasted_iota(jnp.int32, (SGU_CHUNK, SGU_CHUNK), 0)
    col = lax.broadcasted_iota(jnp.int32, (SGU_CHUNK, SGU_CHUNK), 1)
    causal = col <= row
    sb = sb_ref[...]
    per_group = []
    for g in range(SGU_GROUPS):
        w = jnp.where(causal, sw_ref[g], 0.0).astype(BF16)
        gs = slice(g * SGU_GROUP_DIM, (g + 1) * SGU_GROUP_DIM)
        vcat = jnp.concatenate(
            [v[c * SGU_CHUNK:(c + 1) * SGU_CHUNK, gs] for c in range(n_chunks)], axis=1)
        per_group.append(_dot(w, vcat) + sb[:, g:g + 1])
    rows = []
    for c in range(n_chunks):
        cs = slice(c * SGU_GROUP_DIM, (c + 1) * SGU_GROUP_DIM)
        rows.append(jnp.concatenate([per_group[g][:, cs] for g in range(SGU_GROUPS)], axis=1))
    mixed = jnp.concatenate(rows, axis=0)
    return u * mixed


def _mixer_body(x_ref, g_ref, wqkvr_ref, wa_ref, wa2_ref, ba_ref, outg_ref,
                wu_ref, wsv_ref, ng_ref, nb_ref, sw_ref, sb_ref,
                wg_ref, wpa_ref, wpb_ref, wmix_ref, o_ref, st_ref):
    @pl.when(pl.program_id(1) == 0)
    def _():
        st_ref[...] = jnp.zeros_like(st_ref)

    x = x_ref[0]
    h = _rmsnorm(x, g_ref[...]).astype(BF16)
    yb_in = _sgu_branch(h, wu_ref, wsv_ref, ng_ref, nb_ref, sw_ref, sb_ref)
    gates = _sigmoid(_dot(h, wg_ref[...]))
    ya_in = _gla_branch(h, wqkvr_ref, wa_ref, wa2_ref, ba_ref, outg_ref, st_ref)
    y_b = _dot(yb_in.astype(BF16), wpb_ref[...])
    y_a = _dot(ya_in.astype(BF16), wpa_ref[...])
    mix = gates[:, :D_MODEL] * y_a + gates[:, D_MODEL:] * y_b
    o_ref[0] = x + _dot(mix.astype(BF16), wmix_ref[...])


def _mixer(x, g, wqkvr, wa, wa2, ba, outg, wu, wsv, ng, nb, sw, sb, wg, wpa, wpb, wmix):
    B, S, D = x.shape
    consts = (g, wqkvr, wa, wa2, ba, outg, wu, wsv, ng, nb, sw, sb, wg, wpa, wpb, wmix)
    return pl.pallas_call(
        _mixer_body,
        grid=(B, S // SEQ_TILE),
        in_specs=[pl.BlockSpec((1, SEQ_TILE, D), lambda b, j: (b, j, 0))]
                 + [_const_spec(c.shape) for c in consts],
        out_specs=pl.BlockSpec((1, SEQ_TILE, D), lambda b, j: (b, j, 0)),
        out_shape=jax.ShapeDtypeStruct((B, S, D), F32),
        scratch_shapes=[pltpu.VMEM((GLA_WIDTH_V, GLA_WIDTH_K), F32)],
        compiler_params=pltpu.CompilerParams(
            dimension_semantics=("arbitrary", "arbitrary"), vmem_limit_bytes=VMEM_LIMIT),
        name="mixer",
    )(x, *consts)


def _xattn_body(x_ref, k_ref, v_ref, gx_ref, wq_ref, wo_ref, gf_ref, wr_hi_ref, wr_lo_ref, br_ref,
                x2_ref, h3_ref, topi_ref, gate_ref, rank_ref, cnt_ref, carry_ref):
    first = (pl.program_id(0) == 0) & (pl.program_id(1) == 0)

    @pl.when(first)
    def _():
        carry_ref[...] = jnp.zeros_like(carry_ref)

    x = x_ref[0]
    ts = x.shape[0]
    h = _rmsnorm(x, gx_ref[...]).astype(BF16)
    q = _dot(h, wq_ref[...]).astype(BF16)
    km = k_ref[0]
    vm = v_ref[0]
    heads = []
    for hh in range(XATTN_HEADS):
        sl = slice(hh * XATTN_DH, (hh + 1) * XATTN_DH)
        s = _dot_nt(q[:, sl], km[:, sl]) * (XATTN_DH ** -0.5)
        s = s - jnp.max(s, axis=-1, keepdims=True)
        e = jnp.exp(s)
        p = e / jnp.sum(e, axis=-1, keepdims=True)
        heads.append(_dot(p.astype(BF16), vm[:, sl]).astype(BF16))
    o = jnp.concatenate(heads, axis=1)
    x2 = x + _dot(o, wo_ref[...])
    x2_ref[0] = x2

    h3 = _rmsnorm(x2, gf_ref[...])
    h3_ref[0] = h3

    h_hi, h_lo = _split_bf16(h3)
    logits = (_dot_nt(wr_hi_ref[...], h_hi) + _dot_nt(wr_hi_ref[...], h_lo)
              + _dot_nt(wr_lo_ref[...], h_hi)) + br_ref[...]

    e_iota = lax.broadcasted_iota(jnp.int32, (N_EXPERTS, ts), 0)
    work = logits
    vals, idxs, hots = [], [], []
    for _ in range(TOP_K):
        m = jnp.max(work, axis=0, keepdims=True)
        idx = jnp.min(jnp.where(work == m, e_iota, N_EXPERTS), axis=0, keepdims=True)
        hot = e_iota == idx
        vals.append(m)
        idxs.append(idx)
        hots.append(hot)
        work = jnp.where(hot, -jnp.inf, work)
    exps = [jnp.exp(vv - vals[0]) for vv in vals]
    denom = exps[0] + exps[1] + exps[2] + exps[3]
    gate_ref[...] = jnp.concatenate([ee / denom for ee in exps], axis=0)
    topi_ref[...] = jnp.concatenate(idxs, axis=0)

    multi = jnp.where(hots[0] | hots[1] | hots[2] | hots[3], 1.0, 0.0)
    srow = lax.broadcasted_iota(jnp.int32, (ts, ts), 0)
    scol = lax.broadcasted_iota(jnp.int32, (ts, ts), 1)
    strict = jnp.where(srow < scol, 1.0, 0.0).astype(BF16)
    before = _dot(multi.astype(BF16), strict) + carry_ref[...]
    ranks = [jnp.sum(jnp.where(hot, before, 0.0), axis=0, keepdims=True) for hot in hots]
    rank_ref[...] = jnp.concatenate(ranks, axis=0).astype(jnp.int32)
    carry = carry_ref[...] + jnp.sum(multi, axis=1, keepdims=True)
    carry_ref[...] = carry
    cnt_ref[...] = carry.astype(jnp.int32)


def _xattn(x1, kmem, vmem, gx, wq, wo, gf, wr_hi, wr_lo, br):
    B, S, D = x1.shape
    T = B * S
    nj = S // XATTN_TILE
    tok_spec = pl.BlockSpec((1, XATTN_TILE, D), lambda b, j: (b, j, 0))
    mem_spec = pl.BlockSpec((1, MEM_LEN, D), lambda b, j: (b, 0, 0))
    lane_spec = pl.BlockSpec((TOP_K, XATTN_TILE), lambda b, j: (0, b * nj + j))
    consts = (gx, wq, wo, gf, wr_hi, wr_lo, br)
    return pl.pallas_call(
        _xattn_body,
        grid=(B, nj),
        in_specs=[tok_spec, mem_spec, mem_spec] + [_const_spec(c.shape) for c in consts],
        out_specs=[tok_spec, tok_spec, lane_spec, lane_spec, lane_spec,
                   pl.BlockSpec((N_EXPERTS, 1), lambda b, j: (0, 0))],
        out_shape=[
            jax.ShapeDtypeStruct((B, S, D), F32),
            jax.ShapeDtypeStruct((B, S, D), F32),
            jax.ShapeDtypeStruct((TOP_K, T), jnp.int32),
            jax.ShapeDtypeStruct((TOP_K, T), F32),
            jax.ShapeDtypeStruct((TOP_K, T), jnp.int32),
            jax.ShapeDtypeStruct((N_EXPERTS, 1), jnp.int32),
        ],
        scratch_shapes=[pltpu.VMEM((N_EXPERTS, 1), F32)],
        compiler_params=pltpu.CompilerParams(
            dimension_semantics=("arbitrary", "arbitrary"), vmem_limit_bytes=VMEM_LIMIT),
        name="xattn_router",
    )(x1, kmem, vmem, *consts)


ROW_GROUP = 8


def _dispatch_body(zs_ref, nu_ref, dest_ref, h_ref, xb_ref, zbuf_ref, zsem, tsem, sem):
    n_blocks = xb_ref.shape[0] // ROW_BLOCK

    def zero_copy(start, zs):
        start = pl.multiple_of(start, ROW_BLOCK)
        return pltpu.make_async_copy(zbuf_ref, xb_ref.at[pl.ds(start, ROW_BLOCK)], zs)

    @pl.when(pl.program_id(0) == 0)
    def _():
        zbuf_ref[...] = jnp.zeros_like(zbuf_ref)
        for e in range(N_EXPERTS):
            zero_copy(zs_ref[e], zsem).start()

        def tail_start(blk, c):
            zero_copy(blk * ROW_BLOCK, tsem).start()
            return c

        lax.fori_loop(nu_ref[0], n_blocks, tail_start, 0)
        for e in range(N_EXPERTS):
            zero_copy(zs_ref[e], zsem).wait()

    def start(g, c):
        base = pl.multiple_of(g * ROW_GROUP, ROW_GROUP)
        tile = h_ref.at[pl.ds(base, ROW_GROUP)]
        for j in range(ROW_GROUP):
            for kk in range(TOP_K):
                d = dest_ref[0, 0, kk * DISPATCH_TILE + base + j]
                pltpu.make_async_copy(tile.at[pl.ds(j, 1)], xb_ref.at[pl.ds(d, 1)], sem).start(
                    priority=kk % 2)
        return c

    lax.fori_loop(0, DISPATCH_TILE // ROW_GROUP, start, 0)
    for kk in range(TOP_K):
        pltpu.make_async_copy(h_ref, xb_ref.at[pl.ds(0, DISPATCH_TILE)], sem).wait()

    @pl.when(pl.program_id(0) == 0)
    def _():
        def tail_wait(blk, c):
            zero_copy(blk * ROW_BLOCK, tsem).wait()
            return c

        lax.fori_loop(nu_ref[0], n_blocks, tail_wait, 0)


def _dispatch(zero_start, n_used, dest_blocks, h3, n_slots):
    T, D = h3.shape
    return pl.pallas_call(
        _dispatch_body,
        grid_spec=pltpu.PrefetchScalarGridSpec(
            num_scalar_prefetch=2,
            grid=(T // DISPATCH_TILE,),
            in_specs=[
                pl.BlockSpec((1, 1, TOP_K * DISPATCH_TILE), lambda i, zs, nu: (i, 0, 0),
                             memory_space=pltpu.SMEM),
                pl.BlockSpec((DISPATCH_TILE, D), lambda i, zs, nu: (i, 0)),
            ],
            out_specs=pl.BlockSpec(memory_space=pl.ANY),
            scratch_shapes=[pltpu.VMEM((ROW_BLOCK, D), F32), pltpu.SemaphoreType.DMA,
                            pltpu.SemaphoreType.DMA, pltpu.SemaphoreType.DMA],
        ),
        out_shape=jax.ShapeDtypeStruct((n_slots, D), F32),
        compiler_params=pltpu.CompilerParams(
            dimension_semantics=("arbitrary",), vmem_limit_bytes=VMEM_LIMIT),
        name="dispatch",
    )(zero_start, n_used, dest_blocks, h3)


def _expert_mlp(xb, w1b_ref, b1_ref, w2b_ref, b2_ref):
    hcat = _dot(xb.astype(BF16), w1b_ref[...]) + b1_ref[0]
    glu = jnp.minimum(hcat[:, :D_FF], SWIGLU_LIMIT)
    lin = jnp.clip(hcat[:, D_FF:], -SWIGLU_LIMIT, SWIGLU_LIMIT)
    act = (lin + 1.0) * glu * _sigmoid(SWIGLU_ALPHA * glu)
    return _dot(act.astype(BF16), w2b_ref[...]) + b2_ref[0]


def _experts_body(blk_ref, pair_ref, se_ref, run_ref, nxt_ref, ns_ref, nu_ref,
                  x_ref, w1_hbm, b1_ref, w2_hbm, b2_ref,
                  y_hbm, w1b_ref, w2b_ref, wf1_ref, wf2_ref, ybuf_ref, zbuf_ref, sems, wsems, zsem):
    s = pl.program_id(0)
    n_steps = ns_ref[0]
    last = pl.num_programs(0) - 1
    total_blocks = y_hbm.shape[0] // ROW_BLOCK

    def tail_copy(blk):
        row0 = pl.multiple_of(blk * ROW_BLOCK, ROW_BLOCK)
        return pltpu.make_async_copy(zbuf_ref, y_hbm.at[pl.ds(row0, ROW_BLOCK)], zsem)

    @pl.when(s == 0)
    def _():
        zbuf_ref[...] = jnp.zeros_like(zbuf_ref)

        def go(blk, c):
            tail_copy(blk).start()
            return c

        lax.fori_loop(nu_ref[0], total_blocks, go, 0)

    @pl.when(s == last)
    def _():
        def done(blk, c):
            tail_copy(blk).wait()
            return c

        lax.fori_loop(nu_ref[0], total_blocks, done, 0)

    def out_copy(step, half):
        slot = step % 2
        row0 = pl.multiple_of((blk_ref[step] + half) * ROW_BLOCK, ROW_BLOCK)
        return pltpu.make_async_copy(
            ybuf_ref.at[slot, pl.ds(half * ROW_BLOCK, ROW_BLOCK)],
            y_hbm.at[pl.ds(row0, ROW_BLOCK)], sems.at[slot])

    def wait_step(step):
        out_copy(step, 0).wait()

        @pl.when(pair_ref[step] == 1)
        def _():
            out_copy(step, 1).wait()

    def weight_copies(e, wslot):
        return (pltpu.make_async_copy(w1_hbm.at[e], wf1_ref.at[wslot], wsems.at[wslot]),
                pltpu.make_async_copy(w2_hbm.at[e], wf2_ref.at[wslot], wsems.at[wslot]))

    @pl.when(s < n_steps)
    def _():
        prev = se_ref[jnp.maximum(s - 1, 0)]
        wslot = run_ref[s] % 2

        @pl.when(s == 0)
        def _():
            for cp in weight_copies(se_ref[0], 0):
                cp.start()

        @pl.when((s == 0) | (se_ref[s] != prev))
        def _():
            for cp in weight_copies(se_ref[s], wslot):
                cp.wait()
            w1b_ref[...] = wf1_ref[wslot].astype(BF16)
            w2b_ref[...] = wf2_ref[wslot].astype(BF16)

            @pl.when(nxt_ref[s] < N_EXPERTS)
            def _():
                for cp in weight_copies(nxt_ref[s], 1 - wslot):
                    cp.start()

        slot = s % 2

        @pl.when(pair_ref[s] == 1)
        def _():
            ybuf_ref[slot] = _expert_mlp(x_ref[...], w1b_ref, b1_ref, w2b_ref, b2_ref)

        @pl.when(pair_ref[s] == 0)
        def _():
            ybuf_ref[slot, :ROW_BLOCK] = _expert_mlp(x_ref[:ROW_BLOCK], w1b_ref, b1_ref, w2b_ref, b2_ref)

    @pl.when((s >= 1) & (s - 1 < n_steps))
    def _():
        wait_step(s - 1)

    @pl.when(s < n_steps)
    def _():
        out_copy(s, 0).start()

        @pl.when(pair_ref[s] == 1)
        def _():
            out_copy(s, 1).start()

        @pl.when(s == last)
        def _():
            wait_step(s)


def _experts(step_blk, step_pair, step_e, step_run, step_next, n_steps, n_used, xb, w1, b1, w2, b2):
    n_rows, D = xb.shape

    def x_map(s, blk, pair, se, run, nxt, ns, nu):
        return (blk[s] * ROW_BLOCK, 0)

    def b_map(s, blk, pair, se, run, nxt, ns, nu):
        return (se[s], 0, 0)

    return pl.pallas_call(
        _experts_body,
        grid_spec=pltpu.PrefetchScalarGridSpec(
            num_scalar_prefetch=7,
            grid=(step_blk.shape[0],),
            in_specs=[
                pl.BlockSpec((pl.Element(2 * ROW_BLOCK), pl.Element(D)), x_map),
                pl.BlockSpec(memory_space=pl.ANY),
                pl.BlockSpec((1, 1, 2 * D_FF), b_map),
                pl.BlockSpec(memory_space=pl.ANY),
                pl.BlockSpec((1, 1, D), b_map),
            ],
            out_specs=pl.BlockSpec(memory_space=pl.ANY),
            scratch_shapes=[pltpu.VMEM((D, 2 * D_FF), BF16), pltpu.VMEM((D_FF, D), BF16),
                            pltpu.VMEM((2, D, 2 * D_FF), F32), pltpu.VMEM((2, D_FF, D), F32),
                            pltpu.VMEM((2, 2 * ROW_BLOCK, D), F32), pltpu.VMEM((ROW_BLOCK, D), F32),
                            pltpu.SemaphoreType.DMA((2,)), pltpu.SemaphoreType.DMA((2,)),
                            pltpu.SemaphoreType.DMA],
        ),
        out_shape=jax.ShapeDtypeStruct((n_rows, D), F32),
        compiler_params=pltpu.CompilerParams(
            dimension_semantics=("arbitrary",), vmem_limit_bytes=VMEM_LIMIT),
        name="experts",
    )(step_blk, step_pair, step_e, step_run, step_next, n_steps, n_used, xb, w1, b1, w2, b2)


def _combine_body(dest_ref, next_ref, x_ref, gate_ref, g_ref, yb_ref, o_ref, rows_ref, sems):
    i = pl.program_id(0)
    last = pl.num_programs(0) - 1
    slot = i % 2
    n_groups = TOK_TILE // ROW_GROUP

    def start_group(idx_ref, sl, base):
        for kk in range(TOP_K):
            tile = rows_ref.at[sl, kk, pl.ds(base, ROW_GROUP)]
            for j in range(ROW_GROUP):
                d = idx_ref[0, 0, kk * TOK_TILE + base + j]
                pltpu.make_async_copy(yb_ref.at[pl.ds(d, 1)], tile.at[pl.ds(j, 1)],
                                      sems.at[sl]).start(priority=kk % 2)

    def reduce_group(base):
        gates = gate_ref[pl.ds(base, ROW_GROUP), :]
        y = gates[:, 0:1] * rows_ref[slot, 0, pl.ds(base, ROW_GROUP), :]
        for kk in range(1, TOP_K):
            y = y + gates[:, kk:kk + 1] * rows_ref[slot, kk, pl.ds(base, ROW_GROUP), :]
        o_ref[pl.ds(base, ROW_GROUP), :] = x_ref[pl.ds(base, ROW_GROUP), :] + y

    @pl.when(i == 0)
    def _():
        def first(g, c):
            start_group(dest_ref, 0, pl.multiple_of(g * ROW_GROUP, ROW_GROUP))
            return c

        lax.fori_loop(0, n_groups, first, 0)

    for kk in range(TOP_K):
        pltpu.make_async_copy(yb_ref.at[pl.ds(0, TOK_TILE)], rows_ref.at[slot, kk], sems.at[slot]).wait()

    @pl.when(i < last)
    def _():
        def both(g, c):
            base = pl.multiple_of(g * ROW_GROUP, ROW_GROUP)
            start_group(next_ref, 1 - slot, base)
            reduce_group(base)
            return c

        lax.fori_loop(0, n_groups, both, 0)

    @pl.when(i == last)
    def _():
        def only(g, c):
            reduce_group(pl.multiple_of(g * ROW_GROUP, ROW_GROUP))
            return c

        lax.fori_loop(0, n_groups, only, 0)

    o_ref[...] = _rmsnorm(o_ref[...], g_ref[...])


def _combine(dest_blocks, x2, gates_t, g, yb):
    T, D = x2.shape
    n_tiles = T // TOK_TILE
    return pl.pallas_call(
        _combine_body,
        grid=(n_tiles,),
        in_specs=[
            pl.BlockSpec((1, 1, TOP_K * TOK_TILE), lambda i: (i, 0, 0), memory_space=pltpu.SMEM),
            pl.BlockSpec((1, 1, TOP_K * TOK_TILE), lambda i: (jnp.minimum(i + 1, n_tiles - 1), 0, 0),
                         memory_space=pltpu.SMEM),
            pl.BlockSpec((TOK_TILE, D), lambda i: (i, 0)),
            pl.BlockSpec((TOK_TILE, TOP_K), lambda i: (i, 0)),
            _const_spec((1, D)),
            pl.BlockSpec(memory_space=pl.ANY),
        ],
        out_specs=pl.BlockSpec((TOK_TILE, D), lambda i: (i, 0)),
        out_shape=jax.ShapeDtypeStruct((T, D), F32),
        scratch_shapes=[pltpu.VMEM((2, TOP_K, TOK_TILE, D), F32), pltpu.SemaphoreType.DMA((2,))],
        compiler_params=pltpu.CompilerParams(
            dimension_semantics=("arbitrary",), vmem_limit_bytes=VMEM_LIMIT),
        name="combine",
    )(dest_blocks, dest_blocks, x2, gates_t, g, yb)


def _row(v):
    return v.reshape(1, -1).astype(F32)


def _layer(x, kv, p):
    B, S, D = x.shape
    T = B * S
    w_in = p["w_in"]
    wa = jnp.zeros((D, GLA_RANK_PAD), F32).at[:, :GLA_GATE_RANK].set(w_in[:, OFF_A:OFF_U])
    wa2 = jnp.zeros((GLA_RANK_PAD, GLA_WIDTH_K), F32).at[:GLA_GATE_RANK].set(p["gla_w_a2"])
    x1 = _mixer(
        x, _row(p["norm_mix_g"]),
        w_in[:, OFF_Q:OFF_A].astype(BF16), wa.astype(BF16), wa2.astype(BF16),
        _row(p["gla_b_a"]), _row(p["gla_out_g"]),
        w_in[:, OFF_U:OFF_SV].astype(BF16), w_in[:, OFF_SV:OFF_G].astype(BF16),
        _row(p["sgu_norm_g"]), _row(p["sgu_norm_b"]), p["sgu_w"].astype(F32), p["sgu_b"].T.astype(F32),
        w_in[:, OFF_G:].astype(BF16), p["w_proj_a"].astype(BF16), p["w_proj_b"].astype(BF16),
        p["w_mix_out"].astype(BF16))

    kmem, vmem = kv
    wr_t = p["w_router"].T.astype(F32)
    wr_hi = wr_t.astype(BF16)
    wr_lo = (wr_t - wr_hi.astype(F32)).astype(BF16)
    x2, h3, topi, gates, rank, counts = _xattn(
        x1, kmem, vmem, _row(p["norm_x_g"]), p["w_xq"].astype(BF16), p["w_xo"].astype(BF16),
        _row(p["norm_ffn_g"]), wr_hi, wr_lo, p["b_router"].reshape(N_EXPERTS, 1).astype(F32))

    counts = counts.reshape(N_EXPERTS)
    blocks_e = (counts + ROW_BLOCK - 1) // ROW_BLOCK
    padded = blocks_e * ROW_BLOCK
    pad_end = jnp.cumsum(padded)
    pad_start = pad_end - padded
    n_blocks = (T * TOP_K) // ROW_BLOCK + N_EXPERTS
    n_slots = (n_blocks + 1) * ROW_BLOCK
    n_used = (pad_end[-1] // ROW_BLOCK).astype(jnp.int32)
    steps_e = (blocks_e + 1) // 2
    step_end = jnp.cumsum(steps_e)
    n_steps = step_end[-1].astype(jnp.int32)
    max_steps = (n_blocks + N_EXPERTS) // 2
    sidx = jnp.minimum(jnp.arange(max_steps, dtype=jnp.int32), n_steps - 1)
    step_e = jnp.minimum(jnp.sum(step_end[None, :] <= sidx[:, None], axis=1), N_EXPERTS - 1).astype(jnp.int32)
    of_step = step_e[:, None] == jnp.arange(N_EXPERTS, dtype=jnp.int32)[None, :]

    def per_step(v):
        return jnp.sum(jnp.where(of_step, v[None, :], 0), axis=1)

    local = sidx - per_step(step_end - steps_e)
    step_blk = (per_step(pad_start // ROW_BLOCK) + 2 * local).astype(jnp.int32)
    step_pair = (2 * local + 1 < per_step(blocks_e)).astype(jnp.int32)
    ids = jnp.arange(N_EXPERTS, dtype=jnp.int32)
    has_rows = steps_e > 0
    run_e = jnp.cumsum(has_rows.astype(jnp.int32)) - 1
    later = (ids[None, :] > ids[:, None]) & has_rows[None, :]
    next_e = jnp.min(jnp.where(later, ids[None, :], N_EXPERTS), axis=1)
    step_run = per_step(run_e).astype(jnp.int32)
    step_next = per_step(next_e).astype(jnp.int32)
    e_ids = jnp.arange(N_EXPERTS, dtype=jnp.int32)[:, None, None]
    dest = jnp.sum(jnp.where(topi[None] == e_ids, pad_start[:, None, None], 0), axis=0) + rank
    def blocked(tile):
        return (dest.reshape(TOP_K, T // tile, tile).transpose(1, 0, 2)
                .reshape(T // tile, 1, TOP_K * tile).astype(jnp.int32))

    dest_blocks = blocked(TOK_TILE)
    zero_start = jnp.maximum(pad_end - ROW_BLOCK, 0).astype(jnp.int32)

    n_used = n_used.reshape(1)
    xb = _dispatch(zero_start, n_used, blocked(DISPATCH_TILE), h3.reshape(T, D), n_slots)
    yb = _experts(step_blk, step_pair, step_e, step_run, step_next, n_steps.reshape(1), n_used, xb,
                  p["w_e1"], p["b_e1"].reshape(N_EXPERTS, 1, 2 * D_FF),
                  p["w_e2"], p["b_e2"].reshape(N_EXPERTS, 1, D))
    return dest_blocks, x2.reshape(T, D), gates.T, yb


def kernel(x, mem, norm_mix_g, w_in, gla_w_a2, gla_b_a, gla_out_g, sgu_norm_g, sgu_norm_b, sgu_w, sgu_b, w_proj_a, w_proj_b, w_mix_out, norm_x_g, norm_mem_g, w_xq, w_xk, w_xv, w_xo, norm_ffn_g, w_router, b_router, w_e1, b_e1, w_e2, b_e2, norm_final_g):
    B, S, D = x.shape
    depth = w_in.shape[0]
    assert depth == 1, "the final norm is fused into the last layer's combine step"
    stacked = dict(norm_mix_g=norm_mix_g, w_in=w_in, gla_w_a2=gla_w_a2, gla_b_a=gla_b_a,
                   gla_out_g=gla_out_g, sgu_norm_g=sgu_norm_g, sgu_norm_b=sgu_norm_b, sgu_w=sgu_w,
                   sgu_b=sgu_b, w_proj_a=w_proj_a, w_proj_b=w_proj_b, w_mix_out=w_mix_out,
                   norm_x_g=norm_x_g, w_xq=w_xq, w_xo=w_xo, norm_ffn_g=norm_ffn_g,
                   w_router=w_router, b_router=b_router, w_e1=w_e1, b_e1=b_e1, w_e2=w_e2, b_e2=b_e2)
    p = {name: v[0] for name, v in stacked.items()}
    kv = _xattn_kv(mem, _row(norm_mem_g[0]), w_xk[0].astype(BF16), w_xv[0].astype(BF16))
    dest_blocks, x2, gates_t, yb = _layer(x, kv, p)
    out = _combine(dest_blocks, x2, gates_t, _row(norm_final_g), yb)
    return out.reshape(B, S, D)
```
